```python
import math
import jax, jax.numpy as jnp
from jax import lax
import numpy as np


D_MODEL = 2048
BATCH = 4
SEQ = 2048
DEPTH = 1
DEC_BATCH = 128
DEC_SEQ = 8
PAST_LEN = 16384
PAGE_SIZE = 128

MIX_W = D_MODEL
A_W = MIX_W // 2
A_HD = 64
A_HEADS = A_W // A_HD
W_LORA = 64
A_LORA = 64
G_LORA = 160
RWKV_PROJ = 3 * A_W + W_LORA + A_LORA + G_LORA
R_W = MIX_W - A_W
R_HEADS = 8
R_HD = R_W // R_HEADS
RET_PROJ = 4 * R_W
PROJ = RWKV_PROJ + RET_PROJ
RET_CHUNK = 128
N_GROUPS = 4
E_PER_GROUP = 8
N_EXPERTS = N_GROUPS * E_PER_GROUP
TOP_K = 2
D_EXPERT = D_MODEL // 4
MOE_BLOCK = 128
NORM_EPS = 1e-6
LNX_EPS = 64e-5
DECAY_SCALE = math.exp(-0.5)

kernel_name = 'hymba_rwkv7_retnet_hmoe_step'


def _rmsnorm(x, g):
    xf = x.astype(jnp.float32)
    y = xf * lax.rsqrt(jnp.mean(xf * xf, axis=-1, keepdims=True) + NORM_EPS)
    return (y * g.astype(jnp.float32)).astype(x.dtype)


def _wkv_step(S, inp):
    r, w, k, v, kk, b = inp
    sa = jnp.einsum('bhvk,bhk->bhv', S, -kk)
    S = S * w[:, :, None, :] + sa[..., :, None] * b[:, :, None, :] + v[..., :, None] * k[:, :, None, :]
    y = jnp.einsum('bhvk,bhk->bhv', S, r)
    return S, y


def _rwkv7(xm, S0, decay_w0, decay_up, iclr_a0, iclr_up, gate_up, k_k, k_a, r_k, lnx_g, lnx_b):
    B, T, _ = xm.shape
    f32 = jnp.float32
    r, k, v, wd, ad, gd = jnp.split(
        xm, [A_W, 2 * A_W, 3 * A_W, 3 * A_W + W_LORA, 3 * A_W + W_LORA + A_LORA], axis=-1)
    w = jnp.exp(-DECAY_SCALE * jax.nn.sigmoid((decay_w0 + jnp.tanh(wd) @ decay_up).astype(f32)))
    a = jax.nn.sigmoid((iclr_a0 + ad @ iclr_up).astype(f32))
    g = jax.nn.sigmoid(gd) @ gate_up
    hd = lambda t: t.astype(f32).reshape(B, T, A_HEADS, A_HD)
    hp = lambda t: t.astype(f32).reshape(A_HEADS, A_HD)
    r, k, v, w, a = hd(r), hd(k), hd(v), hd(w), hd(a)
    kk = k * hp(k_k)
    kk = kk / jnp.maximum(jnp.sqrt(jnp.sum(kk * kk, axis=-1, keepdims=True)), 1e-12)
    k = k * (1.0 + (a - 1.0) * hp(k_a))
    tm = lambda t: jnp.moveaxis(t, 1, 0)
    S_T, y = lax.scan(_wkv_step, S0.astype(f32), (tm(r), tm(w), tm(k), tm(v), tm(kk), tm(kk * a)))
    y = jnp.moveaxis(y, 0, 1)
    mu = jnp.mean(y, axis=-1, keepdims=True)
    var = jnp.mean(jnp.square(y - mu), axis=-1, keepdims=True)
    y = (y - mu) * lax.rsqrt(var + LNX_EPS) * hp(lnx_g) + hp(lnx_b)
    y = y + jnp.sum(r * k * r_k.astype(f32), axis=-1, keepdims=True) * v
    out = y.reshape(B, T, A_W).astype(xm.dtype) * g
    return out, S_T.astype(S0.dtype)


def _rotary(t, cos, sin):
    t1, t2 = t[..., 0::2], t[..., 1::2]
    return jnp.stack([t1 * cos - t2 * sin, t1 * sin + t2 * cos], axis=-1).reshape(t.shape)


def _retention(p, pos0, R0):
    B, T, _ = p.shape
    f32 = jnp.float32
    q, k, v, gate = jnp.split(p, 4, axis=-1)
    hd = lambda t: t.astype(f32).reshape(B, T, R_HEADS, R_HD)
    pos = pos0 + jnp.arange(T, dtype=f32)
    theta = 1.0 / (10000.0 ** jnp.linspace(0.0, 1.0, R_HD // 2, dtype=f32))
    ang = pos[:, None] * theta[None, :]
    cos, sin = jnp.cos(ang)[None, :, None, :], jnp.sin(ang)[None, :, None, :]
    q = _rotary(hd(q), cos, sin)
    k = _rotary(hd(k), cos, sin) * (R_HD ** -0.5)
    v = hd(v)
    C = math.gcd(T, RET_CHUNK)
    nC = T // C
    to_chunks = lambda t: t.reshape(B, nC, C, R_HEADS, R_HD).transpose(1, 0, 3, 2, 4)
    lg = jnp.log1p(-(2.0 ** (-5.0 - jnp.arange(R_HEADS, dtype=f32))))
    idx = jnp.arange(C, dtype=f32)
    diff = idx[:, None] - idx[None, :]
    causal = diff >= 0
    decay_in = jnp.where(causal, jnp.exp(jnp.where(causal, diff, 0.0)[None] * lg[:, None, None]), 0.0)
    decay_q = jnp.exp((idx + 1.0)[None, :] * lg[:, None])
    decay_k = jnp.exp((C - 1.0 - idx)[None, :] * lg[:, None])
    decay_c = jnp.exp(C * lg)

    def step(R, inp):
        qc, kc, vc = inp
        scores = jnp.einsum('bhid,bhjd->bhij', qc, kc) * decay_in
        o = (jnp.einsum('bhij,bhje->bhie', scores, vc)
             + jnp.einsum('bhid,bhde->bhie', qc, R) * decay_q[None, :, :, None])
        R = R * decay_c[None, :, None, None] + jnp.einsum(
            'bhjd,bhje->bhde', kc * decay_k[None, :, :, None], vc)
        return R, o

    R_T, o = lax.scan(step, R0.astype(f32), (to_chunks(q), to_chunks(k), to_chunks(v)))
    o = o.transpose(1, 0, 3, 2, 4).reshape(B, T, R_HEADS, R_HD)
    o = o * lax.rsqrt(jnp.mean(o * o, axis=-1, keepdims=True) + NORM_EPS)
    out = jax.nn.silu(gate) * o.reshape(B, T, R_W).astype(p.dtype)
    return out, R_T.astype(R0.dtype)


def _grouped_experts(hf, expert, gates, w_gate, w_up, w_down):
    n, D = hf.shape
    A = n * TOP_K
    flat_e = expert.reshape(-1)
    flat_t = jnp.repeat(jnp.arange(n, dtype=jnp.int32), TOP_K)
    flat_g = gates.reshape(-1)
    order = jnp.argsort(flat_e)
    e_sorted = flat_e[order]
    counts = jnp.bincount(flat_e, length=N_EXPERTS).astype(jnp.int32)
    padded = (counts + MOE_BLOCK - 1) // MOE_BLOCK * MOE_BLOCK
    start = jnp.cumsum(counts) - counts
    pend = jnp.cumsum(padded)
    pstart = pend - padded
    dest = (pstart[e_sorted] + jnp.arange(A, dtype=jnp.int32) - start[e_sorted]).astype(jnp.int32)
    n_blk = -(-A // MOE_BLOCK) + N_EXPERTS
    P = n_blk * MOE_BLOCK
    buf_t = jnp.full((P,), n, dtype=jnp.int32).at[dest].set(flat_t[order])
    buf_g = jnp.zeros((P,), hf.dtype).at[dest].set(flat_g[order].astype(hf.dtype))
    blk_start = jnp.arange(n_blk, dtype=pend.dtype) * MOE_BLOCK
    blk_e = jnp.minimum(jnp.searchsorted(pend, blk_start, side='right'), N_EXPERTS - 1)
    x_pad = jnp.concatenate([hf, jnp.zeros((1, D), hf.dtype)], axis=0)
    xb = x_pad[buf_t].reshape(n_blk, MOE_BLOCK, D)

    def expert_block(args):
        xblk, e = args
        return (jax.nn.silu(xblk @ w_gate[e]) * (xblk @ w_up[e])) @ w_down[e]

    yb = lax.map(expert_block, (xb, blk_e)).reshape(P, D)
    out = jax.ops.segment_sum(yb * buf_g[:, None], buf_t, num_segments=n + 1)
    return out[:n]


def _hier_moe(h, router_coarse, router_fine, exp_w_gate, exp_w_up, exp_w_down):
    shp = h.shape
    hf = h.reshape(-1, shp[-1])
    logits_c = (hf @ router_coarse).astype(jnp.float32)
    p_c = jax.nn.softmax(logits_c, axis=-1)
    grp = jnp.argmax(logits_c, axis=-1)
    p_grp = jnp.take_along_axis(p_c, grp[:, None], axis=1)
    logits_f = jnp.einsum('nd,gde->nge', hf, router_fine).astype(jnp.float32)
    lf = jnp.take_along_axis(logits_f, grp[:, None, None], axis=1)[:, 0]
    p_f = jax.nn.softmax(lf, axis=-1)
    top_p, top_i = lax.top_k(p_f, TOP_K)
    gates = p_grp * top_p / jnp.sum(top_p, axis=-1, keepdims=True)
    expert = (grp[:, None] * E_PER_GROUP + top_i).astype(jnp.int32)
    out = _grouped_experts(hf, expert, gates, exp_w_gate, exp_w_up, exp_w_down)
    return out.reshape(shp)


def _layer(x, pos0, shift0, wkv0, ret0, norm_mix_g, w_in, mu_shift, decay_w0, decay_up, iclr_a0,
           iclr_up, gate_up, k_k, k_a, r_k, lnx_g, lnx_b, w_out, norm_ffn_g, router_coarse,
           router_fine, exp_w_gate, exp_w_up, exp_w_down):
    h = _rmsnorm(x, norm_mix_g)
    proj = h @ w_in
    p_rwkv, p_ret = proj[..., :RWKV_PROJ], proj[..., RWKV_PROJ:]
    prev = jnp.concatenate([shift0[:, None, :].astype(p_rwkv.dtype), p_rwkv[:, :-1]], axis=1)
    new_shift = p_rwkv[:, -1].astype(shift0.dtype)
    xm = p_rwkv + (prev - p_rwkv) * mu_shift
    y_a, new_wkv = _rwkv7(xm, wkv0, decay_w0, decay_up, iclr_a0, iclr_up, gate_up,
                          k_k, k_a, r_k, lnx_g, lnx_b)
    y_b, new_ret = _retention(p_ret, pos0, ret0)
    x = x + jnp.concatenate([y_a, y_b], axis=-1) @ w_out
    x = x + _hier_moe(_rmsnorm(x, norm_ffn_g), router_coarse, router_fine,
                      exp_w_gate, exp_w_up, exp_w_down)
    return x, new_shift, new_wkv, new_ret


def setup_inputs(seed: int = 0) -> dict:
    key = jax.random.key(seed)
    ks = jax.random.split(key, 32)
    nrm = lambda k, shape, s: jax.random.normal(k, shape, jnp.float32) * s
    L = DEPTH
    return {
        'x_prompt': nrm(ks[0], (BATCH, SEQ, D_MODEL), 1.0),
        'x_sample': nrm(ks[1], (DEC_BATCH, DEC_SEQ, D_MODEL), 1.0),
        'state_rwkv_shift': nrm(ks[2], (L, DEC_BATCH, RWKV_PROJ), 1.0),
        'state_rwkv_wkv': nrm(ks[3], (L, DEC_BATCH, A_HEADS, A_HD, A_HD), 0.5),
        'state_retention': nrm(ks[4], (L, DEC_BATCH, R_HEADS, R_HD, R_HD), 0.5),
        'norm_mix_g': 1.0 + nrm(ks[5], (L, D_MODEL), 0.05),
        'w_in': nrm(ks[6], (L, D_MODEL, PROJ), D_MODEL ** -0.5),
        'mu_shift': jax.random.uniform(ks[7], (L, RWKV_PROJ), jnp.float32),
        'decay_w0': nrm(ks[8], (L, A_W), 0.5),
        'decay_up': nrm(ks[9], (L, W_LORA, A_W), W_LORA ** -0.5),
        'iclr_a0': nrm(ks[10], (L, A_W), 0.3),
        'iclr_up': nrm(ks[11], (L, A_LORA, A_W), A_LORA ** -0.5),
        'gate_up': nrm(ks[12], (L, G_LORA, A_W), G_LORA ** -0.5),
        'k_k': 0.85 + nrm(ks[13], (L, A_W), 0.05),
        'k_a': 1.0 + nrm(ks[14], (L, A_W), 0.05),
        'r_k': nrm(ks[15], (L, A_HEADS, A_HD), 0.1),
        'lnx_g': 1.0 + nrm(ks[16], (L, A_W), 0.05),
        'lnx_b': nrm(ks[17], (L, A_W), 0.02),
        'w_out': nrm(ks[18], (L, MIX_W, D_MODEL), MIX_W ** -0.5),
        'norm_ffn_g': 1.0 + nrm(ks[19], (L, D_MODEL), 0.05),
        'router_coarse': nrm(ks[20], (L, D_MODEL, N_GROUPS), D_MODEL ** -0.5),
        'router_fine': nrm(ks[21], (L, N_GROUPS, D_MODEL, E_PER_GROUP), D_MODEL ** -0.5),
        'exp_w_gate': nrm(ks[22], (L, N_EXPERTS, D_MODEL, D_EXPERT), D_MODEL ** -0.5),
        'exp_w_up': nrm(ks[23], (L, N_EXPERTS, D_MODEL, D_EXPERT), D_MODEL ** -0.5),
        'exp_w_down': nrm(ks[24], (L, N_EXPERTS, D_EXPERT, D_MODEL), D_EXPERT ** -0.5),
        'norm_final_g': 1.0 + nrm(ks[25], (D_MODEL,), 0.05),
    }


def reference(x_prompt, x_sample, state_rwkv_shift, state_rwkv_wkv, state_retention, norm_mix_g,
              w_in, mu_shift, decay_w0, decay_up, iclr_a0, iclr_up, gate_up, k_k, k_a, r_k,
              lnx_g, lnx_b, w_out, norm_ffn_g, router_coarse, router_fine, exp_w_gate, exp_w_up,
              exp_w_down, norm_final_g):
    bp = x_prompt.shape[0]
    hp, hs = x_prompt, x_sample
    p_shift_l, p_wkv_l, p_ret_l, s_shift_l, s_wkv_l, s_ret_l = [], [], [], [], [], []
    for l in range(DEPTH):
        wts = (norm_mix_g[l], w_in[l], mu_shift[l], decay_w0[l], decay_up[l], iclr_a0[l],
               iclr_up[l], gate_up[l], k_k[l], k_a[l], r_k[l], lnx_g[l], lnx_b[l], w_out[l],
               norm_ffn_g[l], router_coarse[l], router_fine[l], exp_w_gate[l], exp_w_up[l],
               exp_w_down[l])
        zs = jnp.zeros((bp, RWKV_PROJ), state_rwkv_shift.dtype)
        zw = jnp.zeros((bp, A_HEADS, A_HD, A_HD), state_rwkv_wkv.dtype)
        zr = jnp.zeros((bp, R_HEADS, R_HD, R_HD), state_retention.dtype)
        hp, ps, pw, pr = _layer(hp, 0, zs, zw, zr, *wts)
        hs, ss, sw, sr = _layer(hs, PAST_LEN, state_rwkv_shift[l], state_rwkv_wkv[l],
                                state_retention[l], *wts)
        p_shift_l.append(ps); p_wkv_l.append(pw); p_ret_l.append(pr)
        s_shift_l.append(ss); s_wkv_l.append(sw); s_ret_l.append(sr)
    y_prompt = _rmsnorm(hp, norm_final_g)
    y_sample = _rmsnorm(hs, norm_final_g)
    p_shift = jnp.stack(p_shift_l)
    p_wkv = jnp.stack(p_wkv_l)
    p_ret = jnp.stack(p_ret_l)
    s_shift = jnp.stack(s_shift_l)
    s_wkv = jnp.stack(s_wkv_l)
    s_ret = jnp.stack(s_ret_l)
    return (y_prompt, y_sample, p_shift, p_wkv, p_ret, s_shift, s_wkv, s_ret)
```

```python
import functools
import math

import jax
import jax.numpy as jnp
from jax import lax
from jax.experimental import pallas as pl
from jax.experimental.pallas import tpu as pltpu

F32 = jnp.float32
BF16 = jnp.bfloat16

D_MODEL = 2048
A_W = 1024
A_HD = 64
A_HEADS = 16
W_LORA = 64
A_LORA = 64
G_LORA = 160
RWKV_PROJ = 3 * A_W + W_LORA + A_LORA + G_LORA
R_W = 1024
R_HEADS = 8
R_HD = 128
RET_CHUNK = 128
N_GROUPS = 4
E_PER_GROUP = 8
N_EXPERTS = 32
D_EXPERT = 512
NORM_EPS = 1e-6
LNX_EPS = 64e-5
DECAY_SCALE = math.exp(-0.5)
PAST_LEN = 16384

LANES = 128
OFF_WD = 3 * A_W
OFF_AD = OFF_WD + LANES
OFF_GD = OFF_AD + LANES
RWKV_PAD = OFF_GD + 2 * LANES
PROJ_PAD = RWKV_PAD + 4 * R_W
RET_BLK = 512
WKV_CHUNK = 64
MOE_BLK = 256
VMEM_LIMIT = 56 * 1024 * 1024


def _cparams(sem):
    return pltpu.CompilerParams(dimension_semantics=sem, vmem_limit_bytes=VMEM_LIMIT)


def _bdot(a, b):
    return jnp.dot(a.astype(BF16), b.astype(BF16), preferred_element_type=F32)


def _bdot_nt(a, b):
    return lax.dot_general(a.astype(BF16), b.astype(BF16), (((1,), (1,)), ((), ())),
                           preferred_element_type=F32)


def _bdot_tn(a, b):
    return lax.dot_general(a.astype(BF16), b.astype(BF16), (((0,), (0,)), ((), ())),
                           preferred_element_type=F32)


def _split3(x):
    hi = x.astype(BF16)
    r1 = x - hi.astype(F32)
    mid = r1.astype(BF16)
    lo = (r1 - mid.astype(F32)).astype(BF16)
    return hi, mid, lo


def _dot3(x, w_bf16):
    hi, mid, lo = _split3(x)
    d = lambda p: jnp.dot(p, w_bf16, preferred_element_type=F32)
    return d(hi) + d(mid) + d(lo)


def _sigmoid(x):
    return 1.0 / (1.0 + jnp.exp(-x))


def _inproj_kernel(x_ref, g_ref, w_ref, o_ref, h_ref):
    @pl.when(pl.program_id(1) == 0)
    def _():
        x = x_ref[...]
        ms = jnp.mean(x * x, axis=-1, keepdims=True)
        h_ref[...] = (x * lax.rsqrt(ms + NORM_EPS) * g_ref[...]).astype(BF16)

    o_ref[...] = jnp.dot(h_ref[...], w_ref[...], preferred_element_type=F32)


def _inproj(x, g, w_bf16, tm=512, tn=1280):
    n, d = x.shape
    pw = w_bf16.shape[1]
    return pl.pallas_call(
        _inproj_kernel,
        grid=(n // tm, pw // tn),
        in_specs=[pl.BlockSpec((tm, d), lambda i, j: (i, 0)),
                  pl.BlockSpec((1, d), lambda i, j: (0, 0)),
                  pl.BlockSpec((d, tn), lambda i, j: (0, j))],
        out_specs=pl.BlockSpec((tm, tn), lambda i, j: (i, j)),
        out_shape=jax.ShapeDtypeStruct((n, pw), F32),
        scratch_shapes=[pltpu.VMEM((tm, d), BF16)],
        compiler_params=_cparams(("parallel", "arbitrary")),
        name="inproj",
    )(x, g, w_bf16)


def _seg_sum(x, ones_bd):
    nslab = x.shape[1] // LANES
    stacked = jnp.concatenate([x[:, s * LANES:(s + 1) * LANES] for s in range(nslab)], axis=0)
    red = _dot3(stacked, ones_bd)
    c = x.shape[0]
    return jnp.concatenate([red[s * c:(s + 1) * c] for s in range(nslab)], axis=1)


def _rwkv_kernel(p_ref, shift_ref, s0_ref, mu_ref, w0_ref, wup_ref, a0_ref, aup_ref, gup_ref,
                 kk_ref, ka_ref, rk_ref, lng_ref, lnb_ref, y_ref, sout_ref, sbd_ref, carry_ref,
                 *, chunk):
    C = chunk
    c_idx = pl.program_id(1)
    n_pairs = A_HEADS // 2
    lane = lax.broadcasted_iota(jnp.int32, (1, LANES), 1)
    in_a = lane < A_HD

    @pl.when(c_idx == 0)
    def _():
        carry_ref[...] = shift_ref[0]
        for pr in range(n_pairs):
            sa = s0_ref[0, 2 * pr]
            sb = s0_ref[0, 2 * pr + 1]
            z = jnp.zeros((A_HD, A_HD), F32)
            sbd_ref[pr] = jnp.concatenate(
                [jnp.concatenate([sa, z], axis=1), jnp.concatenate([z, sb], axis=1)], axis=0)

    p = p_ref[...]
    row = lax.broadcasted_iota(jnp.int32, (C, 1), 0)
    prev = jnp.where(row == 0, carry_ref[...], pltpu.roll(p, 1, axis=0))
    carry_ref[...] = p[C - 1:C, :]
    xm = p + (prev - p) * mu_ref[...]

    r = xm[:, 0:A_W]
    k = xm[:, A_W:2 * A_W]
    v = xm[:, 2 * A_W:3 * A_W]
    wd = xm[:, OFF_WD:OFF_AD]
    ad = xm[:, OFF_AD:OFF_GD]
    gd = xm[:, OFF_GD:RWKV_PAD]

    logw = -DECAY_SCALE * _sigmoid(w0_ref[...] + _bdot(jnp.tanh(wd), wup_ref[...]))
    a = _sigmoid(a0_ref[...] + _bdot(ad, aup_ref[...]))
    g = _bdot(_sigmoid(gd), gup_ref[...])

    ones_bd = (lax.broadcasted_iota(jnp.int32, (LANES, LANES), 0) // A_HD ==
               lax.broadcasted_iota(jnp.int32, (LANES, LANES), 1) // A_HD).astype(BF16)

    kk = k * kk_ref[...]
    k2 = k * (1.0 + (a - 1.0) * ka_ref[...])
    sums = _seg_sum(jnp.concatenate([kk * kk, r * k2 * rk_ref[...]], axis=0), ones_bd)
    kk = kk / jnp.maximum(jnp.sqrt(sums[0:C]), 1e-12)
    bonus = sums[C:2 * C]
    b = kk * a

    cum = logw
    d = 1
    while d < C:
        cum = cum + jnp.where(row >= d, pltpu.roll(cum, d, axis=0), 0.0)
        d *= 2
    cum_end = cum[C - 1:C, :]
    w_incl = jnp.exp(cum)
    w_excl = jnp.exp(cum - logw)
    w_inv = jnp.exp(-cum)
    w_end = jnp.exp(cum_end - cum)
    w_all = jnp.exp(cum_end)

    kk_d = kk * w_excl
    r_d = r * w_incl
    b_h = b * w_inv
    k_h = k2 * w_inv
    b_e = b * w_end
    k_e = k2 * w_end

    C2 = 2 * C
    ri = lax.broadcasted_iota(jnp.int32, (C2, C2), 0)
    ci = lax.broadcasted_iota(jnp.int32, (C2, C2), 1)
    same = (ri // C) == (ci // C)
    strict = same & (ci < ri)
    incl = same & (ci <= ri)
    eye = (ri == ci).astype(F32)

    def expand(x):
        return jnp.concatenate([jnp.where(in_a, x, 0.0), jnp.where(in_a, 0.0, x)], axis=0)

    y_parts = []
    for pr in range(n_pairs):
        sl = slice(pr * LANES, (pr + 1) * LANES)
        e_kk, e_r = expand(kk_d[:, sl]), expand(r_d[:, sl])
        e_bh, e_kh = expand(b_h[:, sl]), expand(k_h[:, sl])
        e_be, e_ke = expand(b_e[:, sl]), expand(k_e[:, sl])
        v_e = expand(v[:, sl])
        lhs = jnp.concatenate([e_kk, e_r], axis=0).astype(BF16)
        gram = _bdot_nt(lhs, jnp.concatenate([e_bh, e_kh], axis=0))
        l_b = jnp.where(strict, gram[0:C2, 0:C2], 0.0)
        l_k = jnp.where(strict, gram[0:C2, C2:2 * C2], 0.0)
        a_b = jnp.where(incl, gram[C2:2 * C2, 0:C2], 0.0)
        a_k = jnp.where(incl, gram[C2:2 * C2, C2:2 * C2], 0.0)
        t_inv = eye - l_b
        pw = l_b
        e = 2
        while e < C:
            pw = _bdot(pw, pw)
            t_inv = t_inv + _bdot(t_inv, pw)
            e *= 2
        s_bd = sbd_ref[pr]
        xs = _bdot_nt(lhs, s_bd)
        u_e = -_bdot(t_inv, xs[0:C2] + _bdot(l_k, v_e))
        uv = jnp.concatenate([u_e, v_e], axis=0).astype(BF16)
        y_e = xs[C2:2 * C2] + _bdot(jnp.concatenate([a_b, a_k], axis=1), uv)
        y_parts.append(y_e[0:C] + y_e[C:C2])
        sbd_ref[pr] = s_bd * w_all[:, sl] + _bdot_tn(uv, jnp.concatenate([e_be, e_ke], axis=0))
    y = jnp.concatenate(y_parts, axis=1)

    inv_hd = 1.0 / A_HD
    mean = _seg_sum(y, ones_bd) * inv_hd
    yc = y - mean
    var = _seg_sum(yc * yc, ones_bd) * inv_hd
    yn = yc * lax.rsqrt(var + LNX_EPS) * lng_ref[...] + lnb_ref[...]
    yn = yn + bonus * v
    y_ref[...] = (yn * g).astype(y_ref.dtype)

    @pl.when(c_idx == pl.num_programs(1) - 1)
    def _():
        for pr in range(n_pairs):
            s_bd = sbd_ref[pr]
            sout_ref[0, 2 * pr] = s_bd[0:A_HD, 0:A_HD]
            sout_ref[0, 2 * pr + 1] = s_bd[A_HD:2 * A_HD, A_HD:2 * A_HD]


def _rwkv(proj, shift0, s0, wts, batch, seq, chunk):
    n_chunks = seq // chunk
    vec = lambda w: pl.BlockSpec(w.shape, lambda b, c: (0,) * w.ndim)
    (mu, w0, wup, a0, aup, gup, kk, ka, rk, lng, lnb) = wts
    return pl.pallas_call(
        functools.partial(_rwkv_kernel, chunk=chunk),
        grid=(batch, n_chunks),
        in_specs=[pl.BlockSpec((chunk, RWKV_PAD), lambda b, c: (b * n_chunks + c, 0)),
                  pl.BlockSpec((1, 1, RWKV_PAD), lambda b, c: (b, 0, 0)),
                  pl.BlockSpec((1, A_HEADS, A_HD, A_HD), lambda b, c: (b, 0, 0, 0))]
                 + [vec(w) for w in wts],
        out_specs=[pl.BlockSpec((chunk, A_W), lambda b, c: (b * n_chunks + c, 0)),
                   pl.BlockSpec((1, A_HEADS, A_HD, A_HD), lambda b, c: (b, 0, 0, 0))],
        out_shape=[jax.ShapeDtypeStruct((batch * seq, A_W), BF16),
                   jax.ShapeDtypeStruct((batch, A_HEADS, A_HD, A_HD), F32)],
        scratch_shapes=[pltpu.VMEM((A_HEADS // 2, LANES, LANES), F32),
                        pltpu.VMEM((1, RWKV_PAD), F32)],
        compiler_params=_cparams(("parallel", "arbitrary")),
        name="rwkv7",
    )(proj, shift0.reshape(batch, 1, RWKV_PAD), s0, *wts)


def _ret_kernel(q_ref, k_ref, v_ref, g_ref, cos_ref, sin_ref, din_ref, dq_ref, dk_ref, dc_ref,
                r0_ref, y_ref, rout_ref, r_scr, *, chunk, nb):
    C = chunk
    heads = RET_BLK // R_HD
    c_idx = pl.program_id(2)

    @pl.when(c_idx == 0)
    def _():
        r_scr[...] = r0_ref[...]

    even = lax.broadcasted_iota(jnp.int32, (1, R_HD), 1) % 2 == 0
    cos = cos_ref[...]
    sin = sin_ref[...]

    def rot(t):
        partner = jnp.where(even, pltpu.roll(t, R_HD - 1, axis=1), pltpu.roll(t, 1, axis=1))
        return t * cos + partner * sin

    for bi in range(nb):
        rows = slice(bi * C, (bi + 1) * C)
        for j in range(heads):
            cols = slice(j * R_HD, (j + 1) * R_HD)
            q = rot(q_ref[rows, cols])
            k = rot(k_ref[rows, cols]) * (R_HD ** -0.5)
            v = v_ref[rows, cols].astype(BF16)
            state = r_scr[bi, j]
            scores = _bdot_nt(q, k) * din_ref[j]
            o = _bdot(jnp.concatenate([scores, q * dq_ref[j]], axis=1),
                      jnp.concatenate([v, state.astype(BF16)], axis=0))
            r_scr[bi, j] = state * dc_ref[j, 0:1, :] + _bdot_tn(k * dk_ref[j], v)
            on = o * lax.rsqrt(jnp.mean(o * o, axis=-1, keepdims=True) + NORM_EPS)
            gate = g_ref[rows, cols]
            y_ref[rows, cols] = (gate * _sigmoid(gate) * on).astype(y_ref.dtype)

    @pl.when(c_idx == pl.num_programs(2) - 1)
    def _():
        rout_ref[...] = r_scr[...]


def _ret_tables(chunk, pos0, seq):
    pos = pos0 + jnp.arange(seq, dtype=F32)
    theta = 1.0 / (10000.0 ** jnp.linspace(0.0, 1.0, R_HD // 2, dtype=F32))
    ang = pos[:, None] * theta[None, :]
    cos2 = jnp.repeat(jnp.cos(ang), 2, axis=1)
    sin2 = jnp.stack([-jnp.sin(ang), jnp.sin(ang)], axis=-1).reshape(seq, R_HD)
    lg = jnp.log1p(-(2.0 ** (-5.0 - jnp.arange(R_HEADS, dtype=F32))))
    idx = jnp.arange(chunk, dtype=F32)
    diff = idx[:, None] - idx[None, :]
    causal = diff >= 0
    din = jnp.where(causal, jnp.exp(jnp.where(causal, diff, 0.0)[None] * lg[:, None, None]), 0.0)
    bc = lambda t: jnp.broadcast_to(t[:, :, None], (R_HEADS, t.shape[1], R_HD))
    dq = bc(jnp.exp((idx + 1.0)[None, :] * lg[:, None]))
    dk = bc(jnp.exp((chunk - 1.0 - idx)[None, :] * lg[:, None]))
    dc = jnp.broadcast_to(jnp.exp(chunk * lg)[:, None, None], (R_HEADS, 8, R_HD))
    return cos2, sin2, din, dq, dk, dc


def _retention(proj, r0, batch, seq, chunk, nb, pos0):
    n_chunks = seq // chunk
    heads = RET_BLK // R_HD
    n_hg = R_HEADS // heads
    cos2, sin2, din, dq, dk, dc = _ret_tables(chunk, pos0, seq)
    col0 = RWKV_PAD // RET_BLK
    rows = nb * chunk

    def sec(s):
        return pl.BlockSpec((rows, RET_BLK),
                            lambda b, h, c: (b * n_chunks + c, col0 + s * n_hg + h))

    tab = lambda t: pl.BlockSpec((heads,) + t.shape[1:], lambda b, h, c: (h, 0, 0))
    st = pl.BlockSpec((nb, heads, R_HD, R_HD), lambda b, h, c: (b, h, 0, 0))
    return pl.pallas_call(
        functools.partial(_ret_kernel, chunk=chunk, nb=nb),
        grid=(batch // nb, n_hg, n_chunks),
        in_specs=[sec(0), sec(1), sec(2), sec(3),
                  pl.BlockSpec((chunk, R_HD), lambda b, h, c: (c, 0)),
                  pl.BlockSpec((chunk, R_HD), lambda b, h, c: (c, 0)),
                  tab(din), tab(dq), tab(dk), tab(dc), st],
        out_specs=[pl.BlockSpec((rows, RET_BLK), lambda b, h, c: (b * n_chunks + c, h)), st],
        out_shape=[jax.ShapeDtypeStruct((batch * seq, R_W), BF16),
                   jax.ShapeDtypeStruct((batch, R_HEADS, R_HD, R_HD), F32)],
        scratch_shapes=[pltpu.VMEM((nb, heads, R_HD, R_HD), F32)],
        compiler_params=_cparams(("parallel", "parallel", "arbitrary")),
        name="retention",
    )(proj, proj, proj, proj, cos2, sin2, din, dq, dk, dc, r0)


def _outproj_kernel(x_ref, ya_ref, yb_ref, wa_ref, wb_ref, g_ref, wr_ref,
                    x1_ref, h2_ref, ri_ref, rg_ref):
    x1 = (x_ref[...] + jnp.dot(ya_ref[...], wa_ref[...], preferred_element_type=F32)
          + jnp.dot(yb_ref[...], wb_ref[...], preferred_element_type=F32))
    x1_ref[...] = x1
    ms = jnp.mean(x1 * x1, axis=-1, keepdims=True)
    h2 = x1 * lax.rsqrt(ms + NORM_EPS) * g_ref[...]
    h2_ref[...] = h2
    logits = _bdot(h2, wr_ref[...])
    lane = lax.broadcasted_iota(jnp.int32, (1, LANES), 1)
    neg = -1e30
    big = LANES
    rmax = lambda t: jnp.max(t, axis=-1, keepdims=True)
    rmin = lambda t: jnp.min(t, axis=-1, keepdims=True)
    is_c = lane < N_GROUPS
    lc = jnp.where(is_c, logits, neg)
    mc = rmax(lc)
    grp = rmin(jnp.where(lc == mc, lane, big))
    p_grp = 1.0 / jnp.sum(jnp.where(is_c, jnp.exp(lc - mc), 0.0), axis=-1, keepdims=True)
    fine = lane - N_GROUPS
    in_g = (fine >= 0) & (fine < N_EXPERTS) & ((fine // E_PER_GROUP) == grp)
    lf = jnp.where(in_g, logits, neg)
    m1 = rmax(lf)
    i1 = rmin(jnp.where(lf == m1, lane, big))
    lf2 = jnp.where(lane == i1, neg, lf)
    m2 = rmax(lf2)
    i2 = rmin(jnp.where(lf2 == m2, lane, big))
    e2 = jnp.exp(m2 - m1)
    g1 = p_grp / (1.0 + e2)
    g2 = p_grp * e2 / (1.0 + e2)
    ri_ref[...] = jnp.where(lane == 0, i1 - N_GROUPS, jnp.where(lane == 1, i2 - N_GROUPS, 0))
    rg_ref[...] = jnp.where(lane == 0, g1, jnp.where(lane == 1, g2, 0.0))


def _outproj(x, ya, yb, wa, wb, g, wr, tm=512):
    n, d = x.shape
    rowblk = lambda w: pl.BlockSpec((tm, w), lambda i: (i, 0))
    full = lambda w: pl.BlockSpec(w.shape, lambda i: (0, 0))
    return pl.pallas_call(
        _outproj_kernel,
        grid=(n // tm,),
        in_specs=[rowblk(d), rowblk(A_W), rowblk(R_W), full(wa), full(wb), full(g), full(wr)],
        out_specs=[rowblk(d), rowblk(d), rowblk(LANES), rowblk(LANES)],
        out_shape=[jax.ShapeDtypeStruct((n, d), F32), jax.ShapeDtypeStruct((n, d), F32),
                   jax.ShapeDtypeStruct((n, LANES), jnp.int32),
                   jax.ShapeDtypeStruct((n, LANES), F32)],
        compiler_params=_cparams(("parallel",)),
        name="outproj_router",
    )(x, ya, yb, wa, wb, g, wr)


def _sort_kernel(ri_ref, tri_ref, dest_ref, blk_ref, cnt_ref, base_ref, *, n_blk_pad):
    ph = pl.program_id(0)
    i = pl.program_id(1)
    lane = lax.broadcasted_iota(jnp.int32, (1, LANES), 1)
    ri = ri_ref[...]
    oh1 = (lane == ri[:, 0:1]).astype(F32)
    oh2 = (lane == ri[:, 1:2]).astype(F32)
    tot1 = jnp.sum(oh1, axis=0, keepdims=True)
    tot2 = jnp.sum(oh2, axis=0, keepdims=True)

    @pl.when(ph == 0)
    def _():
        @pl.when(i == 0)
        def _():
            cnt_ref[...] = jnp.zeros_like(cnt_ref)
        cnt_ref[...] += tot1 + tot2

    @pl.when(ph == 1)
    def _():
        @pl.when(i == 0)
        def _():
            nblk = jnp.floor((cnt_ref[...] + (MOE_BLK - 1)) * (1.0 / MOE_BLK))
            upper = (lax.broadcasted_iota(jnp.int32, (LANES, LANES), 0) <=
                     lax.broadcasted_iota(jnp.int32, (LANES, LANES), 1))
            pend = _bdot(jnp.broadcast_to(nblk, (8, LANES)), upper.astype(F32))[0:1]
            base_ref[...] = (pend - nblk) * MOE_BLK
            bidx = lax.broadcasted_iota(jnp.int32, (n_blk_pad, LANES), 0).astype(F32)
            done = jnp.where((lane < N_EXPERTS) & (pend <= bidx), 1.0, 0.0)
            blk_e = jnp.minimum(jnp.sum(done, axis=1, keepdims=True), N_EXPERTS - 1.0)
            n_used = jnp.sum(jnp.where(lane == N_EXPERTS - 1, pend, 0.0), axis=1, keepdims=True)
            blk_ref[...] = jnp.where(lane == 0, blk_e, n_used).astype(jnp.int32)

        base = base_ref[...]
        tri = tri_ref[...]
        pre1 = jnp.dot(tri, oh1.astype(BF16), preferred_element_type=F32)
        pre2 = jnp.dot(tri, oh2.astype(BF16), preferred_element_type=F32) + tot1
        d1 = jnp.sum(oh1 * (base + pre1), axis=1, keepdims=True)
        d2 = jnp.sum(oh2 * (base + pre2), axis=1, keepdims=True)
        dest_ref[...] = jnp.where(lane == 0, d1, jnp.where(lane == 1, d2, 0.0)).astype(jnp.int32)
        base_ref[...] = base + tot1 + tot2


def _sort(ri, n_blk, tt=512):
    n = ri.shape[0]
    n_blk_pad = -(-n_blk // 8) * 8
    tri = (jnp.arange(tt)[:, None] > jnp.arange(tt)[None, :]).astype(BF16)
    return pl.pallas_call(
        functools.partial(_sort_kernel, n_blk_pad=n_blk_pad),
        grid=(2, n // tt),
        in_specs=[pl.BlockSpec((tt, LANES), lambda ph, i: (i, 0)),
                  pl.BlockSpec((tt, tt), lambda ph, i: (0, 0))],
        out_specs=[pl.BlockSpec((tt, LANES), lambda ph, i: (i * ph, 0)),
                   pl.BlockSpec((n_blk_pad, LANES), lambda ph, i: (0, 0))],
        out_shape=[jax.ShapeDtypeStruct((n, LANES), jnp.int32),
                   jax.ShapeDtypeStruct((n_blk_pad, LANES), jnp.int32)],
        scratch_shapes=[pltpu.VMEM((1, LANES), F32), pltpu.VMEM((1, LANES), F32)],
        compiler_params=_cparams(("arbitrary", "arbitrary")),
        name="moe_sort",
    )(ri, tri)


def _row_copy(src_ref, src_row, dst_ref, dst_row, sem):
    return pltpu.make_async_copy(src_ref.at[pl.ds(src_row, 1)], dst_ref.at[pl.ds(dst_row, 1)], sem)


def _dispatch_kernel(d1_ref, d2_ref, h_ref, xs_in_ref, xs_ref, sem, *, tt):
    del xs_in_ref
    base = pl.program_id(0) * tt

    def issue(t, carry):
        _row_copy(h_ref, base + t, xs_ref, d1_ref[t], sem).start()
        _row_copy(h_ref, base + t, xs_ref, d2_ref[t], sem).start()
        return carry

    def drain(t, carry):
        _row_copy(h_ref, base + t, xs_ref, d1_ref[t], sem).wait()
        _row_copy(h_ref, base + t, xs_ref, d2_ref[t], sem).wait()
        return carry

    lax.fori_loop(0, tt, issue, 0)
    lax.fori_loop(0, tt, drain, 0)


def _dispatch(d1, d2, h2, xs, tt=512):
    n = h2.shape[0]
    smem = pl.BlockSpec((tt,), lambda i: (i,), memory_space=pltpu.SMEM)
    anyspec = pl.BlockSpec(memory_space=pl.ANY)
    return pl.pallas_call(
        functools.partial(_dispatch_kernel, tt=tt),
        grid=(n // tt,),
        in_specs=[smem, smem, anyspec, anyspec],
        out_specs=anyspec,
        out_shape=jax.ShapeDtypeStruct(xs.shape, xs.dtype),
        scratch_shapes=[pltpu.SemaphoreType.DMA(())],
        input_output_aliases={3: 0},
        compiler_params=_cparams(("arbitrary",)),
        name="moe_dispatch",
    )(d1, d2, h2, xs)


def _expert_kernel(be_ref, nu_ref, xs_ref, wg_ref, wu_ref, wd_ref, y_ref, wg_s, wu_s, wd_s):
    i = pl.program_id(0)
    e = be_ref[i]
    e_prev = be_ref[jnp.maximum(i - 1, 0)]

    @pl.when((i == 0) | (e != e_prev))
    def _():
        wg_s[...] = wg_ref[0].astype(BF16)
        wu_s[...] = wu_ref[0].astype(BF16)
        wd_s[...] = wd_ref[0].astype(BF16)

    @pl.when(i < nu_ref[0])
    def _():
        x = xs_ref[...].astype(BF16)
        hg = jnp.dot(x, wg_s[...], preferred_element_type=F32)
        hu = jnp.dot(x, wu_s[...], preferred_element_type=F32)
        act = (hg * _sigmoid(hg) * hu).astype(BF16)
        y_ref[...] = jnp.dot(act, wd_s[...], preferred_element_type=F32)

    @pl.when(i >= nu_ref[0])
    def _():
        y_ref[...] = jnp.zeros_like(y_ref)


def _experts(blk_e, n_used, xs, wg, wu, wd):
    p, d = xs.shape
    n_blk = p // MOE_BLK
    row = lambda i, be, nu: (jnp.minimum(i, nu[0] - 1), 0)
    wsel = lambda i, be, nu: (be[i], 0, 0)
    grid_spec = pltpu.PrefetchScalarGridSpec(
        num_scalar_prefetch=2,
        grid=(n_blk,),
        in_specs=[pl.BlockSpec((MOE_BLK, d), row),
                  pl.BlockSpec((1, d, D_EXPERT), wsel),
                  pl.BlockSpec((1, d, D_EXPERT), wsel),
                  pl.BlockSpec((1, D_EXPERT, d), wsel)],
        out_specs=pl.BlockSpec((MOE_BLK, d), lambda i, be, nu: (i, 0)),
        scratch_shapes=[pltpu.VMEM((d, D_EXPERT), BF16), pltpu.VMEM((d, D_EXPERT), BF16),
                        pltpu.VMEM((D_EXPERT, d), BF16)])
    return pl.pallas_call(
        _expert_kernel,
        grid_spec=grid_spec,
        out_shape=jax.ShapeDtypeStruct((p, d), F32),
        compiler_params=_cparams(("arbitrary",)),
        name="moe_experts",
    )(blk_e, n_used, xs, wg, wu, wd)


def _combine_kernel(d1_ref, d2_ref, x1_ref, rg_ref, g_ref, yb_ref, o_ref, buf, sem, *, tt):
    def issue(t, carry):
        _row_copy(yb_ref, d1_ref[t], buf.at[0], t, sem).start()
        _row_copy(yb_ref, d2_ref[t], buf.at[1], t, sem).start()
        return carry

    def drain(t, carry):
        _row_copy(yb_ref, d1_ref[t], buf.at[0], t, sem).wait()
        _row_copy(yb_ref, d2_ref[t], buf.at[1], t, sem).wait()
        return carry

    lax.fori_loop(0, tt, issue, 0)
    lax.fori_loop(0, tt, drain, 0)
    rg = rg_ref[...]
    x = x1_ref[...] + (rg[:, 0:1] * buf[0] + rg[:, 1:2] * buf[1])
    ms = jnp.mean(x * x, axis=-1, keepdims=True)
    o_ref[...] = x * lax.rsqrt(ms + NORM_EPS) * g_ref[...]


def _combine(d1, d2, x1, rg, g, yb, tt=256):
    n, d = x1.shape
    smem = pl.BlockSpec((tt,), lambda i: (i,), memory_space=pltpu.SMEM)
    rowblk = lambda w: pl.BlockSpec((tt, w), lambda i: (i, 0))
    return pl.pallas_call(
        functools.partial(_combine_kernel, tt=tt),
        grid=(n // tt,),
        in_specs=[smem, smem, rowblk(d), rowblk(LANES), pl.BlockSpec((1, d), lambda i: (0, 0)),
                  pl.BlockSpec(memory_space=pl.ANY)],
        out_specs=rowblk(d),
        out_shape=jax.ShapeDtypeStruct((n, d), F32),
        scratch_shapes=[pltpu.VMEM((2, tt, d), F32), pltpu.SemaphoreType.DMA(())],
        compiler_params=_cparams(("arbitrary",)),
        name="moe_combine",
    )(d1, d2, x1, rg, g, yb)


def _pad_cols(x, width):
    return jnp.pad(x, ((0, 0), (0, width - x.shape[1])))


def _rwkv_layout(x):
    return jnp.concatenate([
        x[:, :OFF_WD],
        _pad_cols(x[:, 3 * A_W:3 * A_W + W_LORA], LANES),
        _pad_cols(x[:, 3 * A_W + W_LORA:3 * A_W + W_LORA + A_LORA], LANES),
        _pad_cols(x[:, 3 * A_W + W_LORA + A_LORA:RWKV_PROJ], 2 * LANES)], axis=1)


def _rwkv_unlayout(x):
    return jnp.concatenate([x[:, :OFF_WD + W_LORA], x[:, OFF_AD:OFF_AD + A_LORA],
                            x[:, OFF_GD:OFF_GD + G_LORA]], axis=1)


def _pad_rows(w, rows):
    return jnp.pad(w, ((0, rows - w.shape[0]), (0, 0)))


def kernel(x_prompt, x_sample, state_rwkv_shift, state_rwkv_wkv, state_retention, norm_mix_g, w_in, mu_shift, decay_w0, decay_up, iclr_a0, iclr_up, gate_up, k_k, k_a, r_k, lnx_g, lnx_b, w_out, norm_ffn_g, router_coarse, router_fine, exp_w_gate, exp_w_up, exp_w_down, norm_final_g):
    bp, tp, d = x_prompt.shape
    bs, ts, _ = x_sample.shape
    row = lambda t: t.reshape(1, -1)
    l = 0

    w_in_p = jnp.concatenate([_rwkv_layout(w_in[l][:, :RWKV_PROJ]), w_in[l][:, RWKV_PROJ:]],
                             axis=1).astype(BF16)
    rwkv_wts = (_rwkv_layout(row(mu_shift[l])), row(decay_w0[l]),
                _pad_rows(decay_up[l], LANES).astype(BF16), row(iclr_a0[l]),
                _pad_rows(iclr_up[l], LANES).astype(BF16),
                _pad_rows(gate_up[l], 2 * LANES).astype(BF16),
                row(k_k[l]), row(k_a[l]), row(r_k[l]), row(lnx_g[l]), row(lnx_b[l]))
    w_out_a = w_out[l][:A_W].astype(BF16)
    w_out_b = w_out[l][A_W:].astype(BF16)
    w_router = _pad_cols(jnp.concatenate(
        [router_coarse[l], jnp.transpose(router_fine[l], (1, 0, 2)).reshape(d, N_EXPERTS)], axis=1),
        LANES).astype(BF16)

    groups = (
        (x_prompt.reshape(bp * tp, d), bp, tp, WKV_CHUNK, RET_CHUNK, 1, 0.0,
         jnp.zeros((bp, RWKV_PAD), F32), jnp.zeros((bp, A_HEADS, A_HD, A_HD), F32),
         jnp.zeros((bp, R_HEADS, R_HD, R_HD), F32)),
        (x_sample.reshape(bs * ts, d), bs, ts, ts, math.gcd(ts, RET_CHUNK), 8, float(PAST_LEN),
         _rwkv_layout(state_rwkv_shift[l]), state_rwkv_wkv[l], state_retention[l]),
    )
    mixed = []
    for x, b, t, wkv_chunk, ret_chunk, ret_nb, pos0, shift0, wkv0, ret0 in groups:
        proj = _inproj(x, row(norm_mix_g[l]), w_in_p)
        new_shift = _rwkv_unlayout(proj.reshape(b, t, PROJ_PAD)[:, -1, :RWKV_PAD])
        y_a, new_wkv = _rwkv(proj, shift0, wkv0, rwkv_wts, b, t, wkv_chunk)
        y_b, new_ret = _retention(proj, ret0, b, t, ret_chunk, ret_nb, pos0)
        x1, h2, ri, rg = _outproj(x, y_a, y_b, w_out_a, w_out_b, row(norm_ffn_g[l]), w_router)
        mixed.append((x1, h2, ri, rg, new_shift, new_wkv, new_ret))

    n_tok = [m[0].shape[0] for m in mixed]
    n_all = sum(n_tok)
    n_blk = -(-2 * n_all // MOE_BLK) + N_EXPERTS
    dest, blk = _sort(jnp.concatenate([m[2] for m in mixed], axis=0), n_blk)
    blk_e, n_used = blk[:n_blk, 0], blk[0:1, 1]
    xs = jnp.zeros((n_blk * MOE_BLK, d), F32)
    off = 0
    dests = []
    for m, n in zip(mixed, n_tok):
        d1, d2 = dest[off:off + n, 0], dest[off:off + n, 1]
        dests.append((d1, d2))
        xs = _dispatch(d1, d2, m[1], xs)
        off += n
    yb = _experts(blk_e, n_used, xs, exp_w_gate[l], exp_w_up[l], exp_w_down[l])
    outs = [_combine(d1, d2, m[0], m[3], row(norm_final_g), yb)
            for m, (d1, d2) in zip(mixed, dests)]

    (_, _, _, _, p_shift, p_wkv, p_ret), (_, _, _, _, s_shift, s_wkv, s_ret) = mixed
    return (outs[0].reshape(bp, tp, d), outs[1].reshape(bs, ts, d),
            p_shift[None], p_wkv[None], p_ret[None], s_shift[None], s_wkv[None], s_ret[None])
```

```python
import functools
import math

import jax
import jax.numpy as jnp
from jax import lax
from jax.experimental import pallas as pl
from jax.experimental.pallas import tpu as pltpu

F32 = jnp.float32
BF16 = jnp.bfloat16

D_MODEL = 2048
A_W = 1024
A_HD = 64
A_HEADS = 16
W_LORA = 64
A_LORA = 64
G_LORA = 160
RWKV_PROJ = 3 * A_W + W_LORA + A_LORA + G_LORA
R_W = 1024
R_HEADS = 8
R_HD = 128
RET_CHUNK = 128
N_GROUPS = 4
E_PER_GROUP = 8
N_EXPERTS = 32
D_EXPERT = 512
NORM_EPS = 1e-6
LNX_EPS = 64e-5
DECAY_SCALE = math.exp(-0.5)
PAST_LEN = 16384

LANES = 128
OFF_WD = 3 * A_W
OFF_AD = OFF_WD + LANES
OFF_GD = OFF_AD + LANES
RWKV_PAD = OFF_GD + 2 * LANES
PROJ_PAD = RWKV_PAD + 4 * R_W
RET_BLK = 512
WKV_CHUNK = 64
MOE_BLK = 256
VMEM_LIMIT = 56 * 1024 * 1024


def _cparams(sem):
    return pltpu.CompilerParams(dimension_semantics=sem, vmem_limit_bytes=VMEM_LIMIT)


def _bdot(a, b):
    return jnp.dot(a.astype(BF16), b.astype(BF16), preferred_element_type=F32)


def _bdot_nt(a, b):
    return lax.dot_general(a.astype(BF16), b.astype(BF16), (((1,), (1,)), ((), ())),
                           preferred_element_type=F32)


def _bdot_tn(a, b):
    return lax.dot_general(a.astype(BF16), b.astype(BF16), (((0,), (0,)), ((), ())),
                           preferred_element_type=F32)


def _split3(x):
    hi = x.astype(BF16)
    r1 = x - hi.astype(F32)
    mid = r1.astype(BF16)
    lo = (r1 - mid.astype(F32)).astype(BF16)
    return hi, mid, lo


def _dot3(x, w_bf16):
    hi, mid, lo = _split3(x)
    d = lambda p: jnp.dot(p, w_bf16, preferred_element_type=F32)
    return d(hi) + d(mid) + d(lo)


def _sigmoid(x):
    return 1.0 / (1.0 + jnp.exp(-x))


def _inproj_kernel(x_ref, g_ref, w_ref, o_ref, h_ref):
    @pl.when(pl.program_id(1) == 0)
    def _():
        x = x_ref[...]
        ms = jnp.mean(x * x, axis=-1, keepdims=True)
        h_ref[...] = (x * lax.rsqrt(ms + NORM_EPS) * g_ref[...]).astype(BF16)

    o_ref[...] = jnp.dot(h_ref[...], w_ref[...], preferred_element_type=F32)


def _inproj(x, g, w_bf16, tm=512, tn=1280):
    n, d = x.shape
    pw = w_bf16.shape[1]
    return pl.pallas_call(
        _inproj_kernel,
        grid=(n // tm, pw // tn),
        in_specs=[pl.BlockSpec((tm, d), lambda i, j: (i, 0)),
                  pl.BlockSpec((1, d), lambda i, j: (0, 0)),
                  pl.BlockSpec((d, tn), lambda i, j: (0, j))],
        out_specs=pl.BlockSpec((tm, tn), lambda i, j: (i, j)),
        out_shape=jax.ShapeDtypeStruct((n, pw), F32),
        scratch_shapes=[pltpu.VMEM((tm, d), BF16)],
        compiler_params=_cparams(("parallel", "arbitrary")),
        name="inproj",
    )(x, g, w_bf16)


def _seg_sum(x, ones_bd):
    nslab = x.shape[1] // LANES
    stacked = jnp.concatenate([x[:, s * LANES:(s + 1) * LANES] for s in range(nslab)], axis=0)
    red = _dot3(stacked, ones_bd)
    c = x.shape[0]
    return jnp.concatenate([red[s * c:(s + 1) * c] for s in range(nslab)], axis=1)


def _rwkv_kernel(p_ref, shift_ref, s0_ref, mu_ref, w0_ref, wup_ref, a0_ref, aup_ref, gup_ref,
                 kk_ref, ka_ref, rk_ref, lng_ref, lnb_ref, y_ref, sout_ref, sbd_ref, carry_ref,
                 *, chunk, nb, n_chunks):
    C = chunk
    R = nb * C
    c_idx = pl.program_id(1)
    n_pairs = A_HEADS // 2
    pairs = range(n_pairs)
    seqs = range(nb)
    lane = lax.broadcasted_iota(jnp.int32, (1, LANES), 1)
    in_a = lane < A_HD

    @pl.when(c_idx == 0)
    def _():
        carry_ref[...] = shift_ref[...]
        z = jnp.zeros((A_HD, A_HD), F32)
        for s in seqs:
            for pr in pairs:
                sa = s0_ref[s, 2 * pr]
                sb = s0_ref[s, 2 * pr + 1]
                sbd_ref[s, pr] = jnp.concatenate(
                    [jnp.concatenate([sa, z], axis=1), jnp.concatenate([z, sb], axis=1)], axis=0)

    p = p_ref[...]
    row = lax.broadcasted_iota(jnp.int32, (R, 1), 0)
    pos = row % C
    first = jnp.broadcast_to(carry_ref[...], (nb, C, RWKV_PAD)).reshape(R, RWKV_PAD)
    prev = jnp.where(pos == 0, first, pltpu.roll(p, 1, axis=0))
    if n_chunks > 1:
        carry_ref[0] = p[C - 1:C, :]
    xm = p + (prev - p) * mu_ref[...]

    r = xm[:, 0:A_W]
    k = xm[:, A_W:2 * A_W]
    v = xm[:, 2 * A_W:3 * A_W]
    wd = xm[:, OFF_WD:OFF_AD]
    ad = xm[:, OFF_AD:OFF_GD]
    gd = xm[:, OFF_GD:RWKV_PAD]

    logw = -DECAY_SCALE * _sigmoid(w0_ref[...] + _bdot(jnp.tanh(wd), wup_ref[...]))
    a = _sigmoid(a0_ref[...] + _bdot(ad, aup_ref[...]))
    g = _bdot(_sigmoid(gd), gup_ref[...])

    ones_bd = (lax.broadcasted_iota(jnp.int32, (LANES, LANES), 0) // A_HD ==
               lax.broadcasted_iota(jnp.int32, (LANES, LANES), 1) // A_HD).astype(BF16)

    kk = k * kk_ref[...]
    k2 = k * (1.0 + (a - 1.0) * ka_ref[...])
    sums = _seg_sum(jnp.concatenate([kk * kk, r * k2 * rk_ref[...]], axis=0), ones_bd)
    kk = kk / jnp.maximum(jnp.sqrt(sums[0:R]), 1e-12)
    bonus = sums[R:2 * R]
    b = kk * a

    cum = logw
    suf = logw
    d = 1
    while d < C:
        cum = cum + jnp.where(pos >= d, pltpu.roll(cum, d, axis=0), 0.0)
        suf = suf + jnp.where(pos < C - d, pltpu.roll(suf, R - d, axis=0), 0.0)
        d *= 2
    after = suf - logw
    w_incl = jnp.exp(cum)
    w_excl = jnp.exp(cum - logw)
    w_inv = jnp.exp(-cum)
    w_end = jnp.exp(after)
    w_all = jnp.exp(cum + after)

    kk_d = kk * w_excl
    r_d = r * w_incl
    b_h = b * w_inv
    k_h = k2 * w_inv
    b_e = b * w_end
    k_e = k2 * w_end

    R2 = 2 * R
    ri = lax.broadcasted_iota(jnp.int32, (R2, R2), 0)
    ci = lax.broadcasted_iota(jnp.int32, (R2, R2), 1)
    same = (ri // C) == (ci // C)
    strict = same & (ci < ri)
    incl = same & (ci <= ri)
    eye = (ri == ci).astype(F32)

    def expand(x):
        return jnp.concatenate([jnp.where(in_a, x, 0.0), jnp.where(in_a, 0.0, x)], axis=0)

    def seq_rows(x, s):
        if nb == 1:
            return x
        return jnp.concatenate([x[s * C:(s + 1) * C], x[R + s * C:R + (s + 1) * C]], axis=0)

    def from_seqs(parts):
        if nb == 1:
            return parts[0]
        return jnp.concatenate([q[0:C] for q in parts] + [q[C:2 * C] for q in parts], axis=0)

    sl = [slice(pr * LANES, (pr + 1) * LANES) for pr in pairs]
    ex = lambda x: [expand(x[:, sl[pr]]) for pr in pairs]
    e_kk, e_r, e_bh, e_kh, e_be, e_ke = ex(kk_d), ex(r_d), ex(b_h), ex(k_h), ex(b_e), ex(k_e)
    v_e = [t.astype(BF16) for t in ex(v)]
    kk_bf = [t.astype(BF16) for t in e_kk]
    r_bf = [t.astype(BF16) for t in e_r]
    gram = [_bdot_nt(jnp.concatenate([kk_bf[pr], r_bf[pr]], axis=0),
                     jnp.concatenate([e_bh[pr], e_kh[pr]], axis=0)) for pr in pairs]
    l_b = [jnp.where(strict, gram[pr][0:R2, 0:R2], 0.0) for pr in pairs]
    l_k = [jnp.where(strict, gram[pr][0:R2, R2:2 * R2], 0.0) for pr in pairs]
    a_bk = [jnp.concatenate([jnp.where(incl, gram[pr][R2:2 * R2, 0:R2], 0.0),
                             jnp.where(incl, gram[pr][R2:2 * R2, R2:2 * R2], 0.0)],
                            axis=1).astype(BF16) for pr in pairs]
    t_inv = [eye - l_b[pr] for pr in pairs]
    pw = [t.astype(BF16) for t in l_b]
    e = 2
    while e < C:
        pw = [_bdot(pw[pr], pw[pr]).astype(BF16) for pr in pairs]
        t_inv = [t_inv[pr] + _bdot(t_inv[pr], pw[pr]) for pr in pairs]
        e *= 2
    lkv = [_bdot(l_k[pr], v_e[pr]) for pr in pairs]
    xs_kk, xs_r = [], []
    for pr in pairs:
        parts = [_bdot_nt(jnp.concatenate([seq_rows(kk_bf[pr], s), seq_rows(r_bf[pr], s)], axis=0),
                          sbd_ref[s, pr]) for s in seqs]
        xs_kk.append(from_seqs([q[0:2 * C] for q in parts]))
        xs_r.append(from_seqs([q[2 * C:4 * C] for q in parts]))
    u_e = [-_bdot(t_inv[pr], xs_kk[pr] + lkv[pr]).astype(BF16) for pr in pairs]
    uv = [jnp.concatenate([u_e[pr], v_e[pr]], axis=0) for pr in pairs]
    y_e = [xs_r[pr] + _bdot(a_bk[pr], uv[pr]) for pr in pairs]
    y = jnp.concatenate([y_e[pr][0:R] + y_e[pr][R:R2] for pr in pairs], axis=1)
    for pr in pairs:
        be_bf, ke_bf = e_be[pr].astype(BF16), e_ke[pr].astype(BF16)
        for s in seqs:
            upd = _bdot_tn(jnp.concatenate([seq_rows(u_e[pr], s), seq_rows(v_e[pr], s)], axis=0),
                           jnp.concatenate([seq_rows(be_bf, s), seq_rows(ke_bf, s)], axis=0))
            sbd_ref[s, pr] = sbd_ref[s, pr] * w_all[s * C:s * C + 1, sl[pr]] + upd

    inv_hd = 1.0 / A_HD
    mean = _seg_sum(y, ones_bd) * inv_hd
    yc = y - mean
    var = _seg_sum(yc * yc, ones_bd) * inv_hd
    yn = yc * lax.rsqrt(var + LNX_EPS) * lng_ref[...] + lnb_ref[...]
    yn = yn + bonus * v
    y_ref[...] = (yn * g).astype(y_ref.dtype)

    @pl.when(c_idx == pl.num_programs(1) - 1)
    def _():
        for s in seqs:
            for pr in pairs:
                s_bd = sbd_ref[s, pr]
                sout_ref[s, 2 * pr] = s_bd[0:A_HD, 0:A_HD]
                sout_ref[s, 2 * pr + 1] = s_bd[A_HD:2 * A_HD, A_HD:2 * A_HD]


def _rwkv(proj, shift0, s0, wts, batch, seq, chunk, nb):
    n_chunks = seq // chunk
    assert nb == 1 or n_chunks == 1
    rows = nb * chunk
    vec = lambda w: pl.BlockSpec(w.shape, lambda b, c: (0,) * w.ndim)
    st = pl.BlockSpec((nb, A_HEADS, A_HD, A_HD), lambda b, c: (b, 0, 0, 0))
    return pl.pallas_call(
        functools.partial(_rwkv_kernel, chunk=chunk, nb=nb, n_chunks=n_chunks),
        grid=(batch // nb, n_chunks),
        in_specs=[pl.BlockSpec((rows, RWKV_PAD), lambda b, c: (b * n_chunks + c, 0)),
                  pl.BlockSpec((nb, 1, RWKV_PAD), lambda b, c: (b, 0, 0)), st]
                 + [vec(w) for w in wts],
        out_specs=[pl.BlockSpec((rows, A_W), lambda b, c: (b * n_chunks + c, 0)), st],
        out_shape=[jax.ShapeDtypeStruct((batch * seq, A_W), BF16),
                   jax.ShapeDtypeStruct((batch, A_HEADS, A_HD, A_HD), F32)],
        scratch_shapes=[pltpu.VMEM((nb, A_HEADS // 2, LANES, LANES), F32),
                        pltpu.VMEM((nb, 1, RWKV_PAD), F32)],
        compiler_params=_cparams(("parallel", "arbitrary")),
        name="rwkv7",
    )(proj, shift0.reshape(batch, 1, RWKV_PAD), s0, *wts)


def _ret_kernel(q_ref, k_ref, v_ref, g_ref, cos_ref, sin_ref, din_ref, dq_ref, dk_ref, dc_ref,
                r0_ref, y_ref, rout_ref, r_scr, *, chunk, nb):
    C = chunk
    heads = RET_BLK // R_HD
    c_idx = pl.program_id(2)

    @pl.when(c_idx == 0)
    def _():
        r_scr[...] = r0_ref[...]

    even = lax.broadcasted_iota(jnp.int32, (1, R_HD), 1) % 2 == 0
    cos = cos_ref[...]
    sin = sin_ref[...]

    def rot(t):
        partner = jnp.where(even, pltpu.roll(t, R_HD - 1, axis=1), pltpu.roll(t, 1, axis=1))
        return t * cos + partner * sin

    for bi in range(nb):
        rows = slice(bi * C, (bi + 1) * C)
        for j in range(heads):
            cols = slice(j * R_HD, (j + 1) * R_HD)
            q = rot(q_ref[rows, cols])
            k = rot(k_ref[rows, cols]) * (R_HD ** -0.5)
            v = v_ref[rows, cols].astype(BF16)
            state = r_scr[bi, j]
            scores = _bdot_nt(q, k) * din_ref[j]
            o = _bdot(jnp.concatenate([scores, q * dq_ref[j]], axis=1),
                      jnp.concatenate([v, state.astype(BF16)], axis=0))
            r_scr[bi, j] = state * dc_ref[j, 0:1, :] + _bdot_tn(k * dk_ref[j], v)
            on = o * lax.rsqrt(jnp.mean(o * o, axis=-1, keepdims=True) + NORM_EPS)
            gate = g_ref[rows, cols]
            y_ref[rows, cols] = (gate * _sigmoid(gate) * on).astype(y_ref.dtype)

    @pl.when(c_idx == pl.num_programs(2) - 1)
    def _():
        rout_ref[...] = r_scr[...]


def _ret_tables(chunk, pos0, seq):
    pos = pos0 + jnp.arange(seq, dtype=F32)
    theta = 1.0 / (10000.0 ** jnp.linspace(0.0, 1.0, R_HD // 2, dtype=F32))
    ang = pos[:, None] * theta[None, :]
    cos2 = jnp.repeat(jnp.cos(ang), 2, axis=1)
    sin2 = jnp.stack([-jnp.sin(ang), jnp.sin(ang)], axis=-1).reshape(seq, R_HD)
    lg = jnp.log1p(-(2.0 ** (-5.0 - jnp.arange(R_HEADS, dtype=F32))))
    idx = jnp.arange(chunk, dtype=F32)
    diff = idx[:, None] - idx[None, :]
    causal = diff >= 0
    din = jnp.where(causal, jnp.exp(jnp.where(causal, diff, 0.0)[None] * lg[:, None, None]), 0.0)
    bc = lambda t: jnp.broadcast_to(t[:, :, None], (R_HEADS, t.shape[1], R_HD))
    dq = bc(jnp.exp((idx + 1.0)[None, :] * lg[:, None]))
    dk = bc(jnp.exp((chunk - 1.0 - idx)[None, :] * lg[:, None]))
    dc = jnp.broadcast_to(jnp.exp(chunk * lg)[:, None, None], (R_HEADS, 8, R_HD))
    return cos2, sin2, din, dq, dk, dc


def _retention(proj, r0, batch, seq, chunk, nb, pos0):
    n_chunks = seq // chunk
    heads = RET_BLK // R_HD
    n_hg = R_HEADS // heads
    cos2, sin2, din, dq, dk, dc = _ret_tables(chunk, pos0, seq)
    col0 = RWKV_PAD // RET_BLK
    rows = nb * chunk

    def sec(s):
        return pl.BlockSpec((rows, RET_BLK),
                            lambda b, h, c: (b * n_chunks + c, col0 + s * n_hg + h))

    tab = lambda t: pl.BlockSpec((heads,) + t.shape[1:], lambda b, h, c: (h, 0, 0))
    st = pl.BlockSpec((nb, heads, R_HD, R_HD), lambda b, h, c: (b, h, 0, 0))
    return pl.pallas_call(
        functools.partial(_ret_kernel, chunk=chunk, nb=nb),
        grid=(batch // nb, n_hg, n_chunks),
        in_specs=[sec(0), sec(1), sec(2), sec(3),
                  pl.BlockSpec((chunk, R_HD), lambda b, h, c: (c, 0)),
                  pl.BlockSpec((chunk, R_HD), lambda b, h, c: (c, 0)),
                  tab(din), tab(dq), tab(dk), tab(dc), st],
        out_specs=[pl.BlockSpec((rows, RET_BLK), lambda b, h, c: (b * n_chunks + c, h)), st],
        out_shape=[jax.ShapeDtypeStruct((batch * seq, R_W), BF16),
                   jax.ShapeDtypeStruct((batch, R_HEADS, R_HD, R_HD), F32)],
        scratch_shapes=[pltpu.VMEM((nb, heads, R_HD, R_HD), F32)],
        compiler_params=_cparams(("parallel", "parallel", "arbitrary")),
        name="retention",
    )(proj, proj, proj, proj, cos2, sin2, din, dq, dk, dc, r0)


def _outproj_kernel(x_ref, ya_ref, yb_ref, wa_ref, wb_ref, g_ref, wr_ref,
                    x1_ref, h2_ref, ri_ref, rg_ref):
    x1 = (x_ref[...] + jnp.dot(ya_ref[...], wa_ref[...], preferred_element_type=F32)
          + jnp.dot(yb_ref[...], wb_ref[...], preferred_element_type=F32))
    x1_ref[...] = x1
    ms = jnp.mean(x1 * x1, axis=-1, keepdims=True)
    h2 = x1 * lax.rsqrt(ms + NORM_EPS) * g_ref[...]
    h2_ref[...] = h2
    logits = _bdot(h2, wr_ref[...])
    lane = lax.broadcasted_iota(jnp.int32, (1, LANES), 1)
    neg = -1e30
    big = LANES
    rmax = lambda t: jnp.max(t, axis=-1, keepdims=True)
    rmin = lambda t: jnp.min(t, axis=-1, keepdims=True)
    is_c = lane < N_GROUPS
    lc = jnp.where(is_c, logits, neg)
    mc = rmax(lc)
    grp = rmin(jnp.where(lc == mc, lane, big))
    p_grp = 1.0 / jnp.sum(jnp.where(is_c, jnp.exp(lc - mc), 0.0), axis=-1, keepdims=True)
    fine = lane - N_GROUPS
    in_g = (fine >= 0) & (fine < N_EXPERTS) & ((fine // E_PER_GROUP) == grp)
    lf = jnp.where(in_g, logits, neg)
    m1 = rmax(lf)
    i1 = rmin(jnp.where(lf == m1, lane, big))
    lf2 = jnp.where(lane == i1, neg, lf)
    m2 = rmax(lf2)
    i2 = rmin(jnp.where(lf2 == m2, lane, big))
    e2 = jnp.exp(m2 - m1)
    g1 = p_grp / (1.0 + e2)
    g2 = p_grp * e2 / (1.0 + e2)
    ri_ref[...] = jnp.where(lane == 0, i1 - N_GROUPS, jnp.where(lane == 1, i2 - N_GROUPS, 0))
    rg_ref[...] = jnp.where(lane == 0, g1, jnp.where(lane == 1, g2, 0.0))


def _outproj(x, ya, yb, wa, wb, g, wr, tm=512):
    n, d = x.shape
    rowblk = lambda w: pl.BlockSpec((tm, w), lambda i: (i, 0))
    full = lambda w: pl.BlockSpec(w.shape, lambda i: (0, 0))
    return pl.pallas_call(
        _outproj_kernel,
        grid=(n // tm,),
        in_specs=[rowblk(d), rowblk(A_W), rowblk(R_W), full(wa), full(wb), full(g), full(wr)],
        out_specs=[rowblk(d), rowblk(d), rowblk(LANES), rowblk(LANES)],
        out_shape=[jax.ShapeDtypeStruct((n, d), F32), jax.ShapeDtypeStruct((n, d), F32),
                   jax.ShapeDtypeStruct((n, LANES), jnp.int32),
                   jax.ShapeDtypeStruct((n, LANES), F32)],
        compiler_params=_cparams(("parallel",)),
        name="outproj_router",
    )(x, ya, yb, wa, wb, g, wr)


def _sort_kernel(ri_ref, tri_ref, dest_ref, blk_ref, cnt_ref, base_ref, *, n_blk_pad):
    ph = pl.program_id(0)
    i = pl.program_id(1)
    lane = lax.broadcasted_iota(jnp.int32, (1, LANES), 1)
    ri = ri_ref[...]
    oh1 = (lane == ri[:, 0:1]).astype(F32)
    oh2 = (lane == ri[:, 1:2]).astype(F32)
    tot1 = jnp.sum(oh1, axis=0, keepdims=True)
    tot2 = jnp.sum(oh2, axis=0, keepdims=True)

    @pl.when(ph == 0)
    def _():
        @pl.when(i == 0)
        def _():
            cnt_ref[...] = jnp.zeros_like(cnt_ref)
        cnt_ref[...] += tot1 + tot2

    @pl.when(ph == 1)
    def _():
        @pl.when(i == 0)
        def _():
            nblk = jnp.floor((cnt_ref[...] + (MOE_BLK - 1)) * (1.0 / MOE_BLK))
            upper = (lax.broadcasted_iota(jnp.int32, (LANES, LANES), 0) <=
                     lax.broadcasted_iota(jnp.int32, (LANES, LANES), 1))
            pend = _bdot(jnp.broadcast_to(nblk, (8, LANES)), upper.astype(F32))[0:1]
            base_ref[...] = (pend - nblk) * MOE_BLK
            bidx = lax.broadcasted_iota(jnp.int32, (n_blk_pad, LANES), 0).astype(F32)
            done = jnp.where((lane < N_EXPERTS) & (pend <= bidx), 1.0, 0.0)
            blk_e = jnp.minimum(jnp.sum(done, axis=1, keepdims=True), N_EXPERTS - 1.0)
            n_used = jnp.sum(jnp.where(lane == N_EXPERTS - 1, pend, 0.0), axis=1, keepdims=True)
            blk_ref[...] = jnp.where(lane == 0, blk_e, n_used).astype(jnp.int32)

        base = base_ref[...]
        tri = tri_ref[...]
        pre1 = jnp.dot(tri, oh1.astype(BF16), preferred_element_type=F32)
        pre2 = jnp.dot(tri, oh2.astype(BF16), preferred_element_type=F32) + tot1
        d1 = jnp.sum(oh1 * (base + pre1), axis=1, keepdims=True)
        d2 = jnp.sum(oh2 * (base + pre2), axis=1, keepdims=True)
        dest_ref[...] = jnp.where(lane == 0, d1, jnp.where(lane == 1, d2, 0.0)).astype(jnp.int32)
        base_ref[...] = base + tot1 + tot2


def _sort(ri, n_blk, tt=512):
    n = ri.shape[0]
    n_blk_pad = -(-n_blk // 8) * 8
    tri = (jnp.arange(tt)[:, None] > jnp.arange(tt)[None, :]).astype(BF16)
    return pl.pallas_call(
        functools.partial(_sort_kernel, n_blk_pad=n_blk_pad),
        grid=(2, n // tt),
        in_specs=[pl.BlockSpec((tt, LANES), lambda ph, i: (i, 0)),
                  pl.BlockSpec((tt, tt), lambda ph, i: (0, 0))],
        out_specs=[pl.BlockSpec((tt, LANES), lambda ph, i: (i * ph, 0)),
                   pl.BlockSpec((n_blk_pad, LANES), lambda ph, i: (0, 0))],
        out_shape=[jax.ShapeDtypeStruct((n, LANES), jnp.int32),
                   jax.ShapeDtypeStruct((n_blk_pad, LANES), jnp.int32)],
        scratch_shapes=[pltpu.VMEM((1, LANES), F32), pltpu.VMEM((1, LANES), F32)],
        compiler_params=_cparams(("arbitrary", "arbitrary")),
        name="moe_sort",
    )(ri, tri)


def _row_copy(src_ref, src_row, dst_ref, dst_row, sem):
    return pltpu.make_async_copy(src_ref.at[pl.ds(src_row, 1)], dst_ref.at[pl.ds(dst_row, 1)], sem)


def _dispatch_kernel(d1_ref, d2_ref, h_ref, xs_in_ref, xs_ref, sem, *, tt):
    del xs_in_ref

    def issue(t, carry):
        _row_copy(h_ref, t, xs_ref, d1_ref[t], sem).start()
        _row_copy(h_ref, t, xs_ref, d2_ref[t], sem).start()
        return carry

    def drain(t, carry):
        _row_copy(h_ref, t, xs_ref, d1_ref[t], sem).wait()
        _row_copy(h_ref, t, xs_ref, d2_ref[t], sem).wait()
        return carry

    lax.fori_loop(0, tt, issue, 0, unroll=8)
    lax.fori_loop(0, tt, drain, 0, unroll=8)


def _dispatch(d1, d2, h2, xs, tt=512):
    n, d = h2.shape
    smem = pl.BlockSpec((tt,), lambda i: (i,), memory_space=pltpu.SMEM)
    anyspec = pl.BlockSpec(memory_space=pl.ANY)
    return pl.pallas_call(
        functools.partial(_dispatch_kernel, tt=tt),
        grid=(n // tt,),
        in_specs=[smem, smem, pl.BlockSpec((tt, d), lambda i: (i, 0)), anyspec],
        out_specs=anyspec,
        out_shape=jax.ShapeDtypeStruct(xs.shape, xs.dtype),
        scratch_shapes=[pltpu.SemaphoreType.DMA(())],
        input_output_aliases={3: 0},
        compiler_params=_cparams(("arbitrary",)),
        name="moe_dispatch",
    )(d1, d2, h2, xs)


def _expert_kernel(be_ref, nu_ref, xs_ref, wg_ref, wu_ref, wd_ref, y_ref, wg_s, wu_s, wd_s):
    i = pl.program_id(0)
    e = be_ref[i]
    e_prev = be_ref[jnp.maximum(i - 1, 0)]

    @pl.when((i == 0) | (e != e_prev))
    def _():
        wg_s[...] = wg_ref[0].astype(BF16)
        wu_s[...] = wu_ref[0].astype(BF16)
        wd_s[...] = wd_ref[0].astype(BF16)

    @pl.when(i < nu_ref[0])
    def _():
        x = xs_ref[...].astype(BF16)
        hg = jnp.dot(x, wg_s[...], preferred_element_type=F32)
        hu = jnp.dot(x, wu_s[...], preferred_element_type=F32)
        act = (hg * _sigmoid(hg) * hu).astype(BF16)
        y_ref[...] = jnp.dot(act, wd_s[...], preferred_element_type=F32)

    @pl.when(i >= nu_ref[0])
    def _():
        y_ref[...] = jnp.zeros_like(y_ref)


def _experts(blk_e, n_used, xs, wg, wu, wd):
    p, d = xs.shape
    n_blk = p // MOE_BLK
    row = lambda i, be, nu: (jnp.minimum(i, nu[0] - 1), 0)
    wsel = lambda i, be, nu: (be[i], 0, 0)
    grid_spec = pltpu.PrefetchScalarGridSpec(
        num_scalar_prefetch=2,
        grid=(n_blk,),
        in_specs=[pl.BlockSpec((MOE_BLK, d), row),
                  pl.BlockSpec((1, d, D_EXPERT), wsel),
                  pl.BlockSpec((1, d, D_EXPERT), wsel),
                  pl.BlockSpec((1, D_EXPERT, d), wsel)],
        out_specs=pl.BlockSpec((MOE_BLK, d), lambda i, be, nu: (i, 0)),
        scratch_shapes=[pltpu.VMEM((d, D_EXPERT), BF16), pltpu.VMEM((d, D_EXPERT), BF16),
                        pltpu.VMEM((D_EXPERT, d), BF16)])
    return pl.pallas_call(
        _expert_kernel,
        grid_spec=grid_spec,
        out_shape=jax.ShapeDtypeStruct((p, d), F32),
        compiler_params=_cparams(("arbitrary",)),
        name="moe_experts",
    )(blk_e, n_used, xs, wg, wu, wd)


def _combine_kernel(d1_ref, d2_ref, x1_ref, rg_ref, g_ref, yb_ref, o_ref, buf, sem, *, tt):
    def issue(t, carry):
        _row_copy(yb_ref, d1_ref[t], buf.at[0], t, sem).start()
        _row_copy(yb_ref, d2_ref[t], buf.at[1], t, sem).start()
        return carry

    def drain(t, carry):
        _row_copy(yb_ref, d1_ref[t], buf.at[0], t, sem).wait()
        _row_copy(yb_ref, d2_ref[t], buf.at[1], t, sem).wait()
        return carry

    lax.fori_loop(0, tt, issue, 0, unroll=8)
    lax.fori_loop(0, tt, drain, 0, unroll=8)
    rg = rg_ref[...]
    x = x1_ref[...] + (rg[:, 0:1] * buf[0] + rg[:, 1:2] * buf[1])
    ms = jnp.mean(x * x, axis=-1, keepdims=True)
    o_ref[...] = x * lax.rsqrt(ms + NORM_EPS) * g_ref[...]


def _combine(d1, d2, x1, rg, g, yb, tt=256):
    n, d = x1.shape
    smem = pl.BlockSpec((tt,), lambda i: (i,), memory_space=pltpu.SMEM)
    rowblk = lambda w: pl.BlockSpec((tt, w), lambda i: (i, 0))
    return pl.pallas_call(
        functools.partial(_combine_kernel, tt=tt),
        grid=(n // tt,),
        in_specs=[smem, smem, rowblk(d), rowblk(LANES), pl.BlockSpec((1, d), lambda i: (0, 0)),
                  pl.BlockSpec(memory_space=pl.ANY)],
        out_specs=rowblk(d),
        out_shape=jax.ShapeDtypeStruct((n, d), F32),
        scratch_shapes=[pltpu.VMEM((2, tt, d), F32), pltpu.SemaphoreType.DMA(())],
        compiler_params=_cparams(("arbitrary",)),
        name="moe_combine",
    )(d1, d2, x1, rg, g, yb)


def _pad_cols(x, width):
    return jnp.pad(x, ((0, 0), (0, width - x.shape[1])))


def _rwkv_layout(x):
    return jnp.concatenate([
        x[:, :OFF_WD],
        _pad_cols(x[:, 3 * A_W:3 * A_W + W_LORA], LANES),
        _pad_cols(x[:, 3 * A_W + W_LORA:3 * A_W + W_LORA + A_LORA], LANES),
        _pad_cols(x[:, 3 * A_W + W_LORA + A_LORA:RWKV_PROJ], 2 * LANES)], axis=1)


def _rwkv_unlayout(x):
    return jnp.concatenate([x[:, :OFF_WD + W_LORA], x[:, OFF_AD:OFF_AD + A_LORA],
                            x[:, OFF_GD:OFF_GD + G_LORA]], axis=1)


def _pad_rows(w, rows):
    return jnp.pad(w, ((0, rows - w.shape[0]), (0, 0)))


def kernel(x_prompt, x_sample, state_rwkv_shift, state_rwkv_wkv, state_retention, norm_mix_g, w_in, mu_shift, decay_w0, decay_up, iclr_a0, iclr_up, gate_up, k_k, k_a, r_k, lnx_g, lnx_b, w_out, norm_ffn_g, router_coarse, router_fine, exp_w_gate, exp_w_up, exp_w_down, norm_final_g):
    bp, tp, d = x_prompt.shape
    bs, ts, _ = x_sample.shape
    row = lambda t: t.reshape(1, -1)
    l = 0

    w_in_p = jnp.concatenate([_rwkv_layout(w_in[l][:, :RWKV_PROJ]), w_in[l][:, RWKV_PROJ:]],
                             axis=1).astype(BF16)
    rwkv_wts = (_rwkv_layout(row(mu_shift[l])), row(decay_w0[l]),
                _pad_rows(decay_up[l], LANES).astype(BF16), row(iclr_a0[l]),
                _pad_rows(iclr_up[l], LANES).astype(BF16),
                _pad_rows(gate_up[l], 2 * LANES).astype(BF16),
                row(k_k[l]), row(k_a[l]), row(r_k[l]), row(lnx_g[l]), row(lnx_b[l]))
    w_out_a = w_out[l][:A_W].astype(BF16)
    w_out_b = w_out[l][A_W:].astype(BF16)
    w_router = _pad_cols(jnp.concatenate(
        [router_coarse[l], jnp.transpose(router_fine[l], (1, 0, 2)).reshape(d, N_EXPERTS)], axis=1),
        LANES).astype(BF16)

    groups = (
        (x_prompt.reshape(bp * tp, d), bp, tp, WKV_CHUNK, RET_CHUNK, 1, 0.0,
         jnp.zeros((bp, RWKV_PAD), F32), jnp.zeros((bp, A_HEADS, A_HD, A_HD), F32),
         jnp.zeros((bp, R_HEADS, R_HD, R_HD), F32)),
        (x_sample.reshape(bs * ts, d), bs, ts, ts, math.gcd(ts, RET_CHUNK), 8, float(PAST_LEN),
         _rwkv_layout(state_rwkv_shift[l]), state_rwkv_wkv[l], state_retention[l]),
    )
    mixed = []
    for x, b, t, wkv_chunk, ret_chunk, nb, pos0, shift0, wkv0, ret0 in groups:
        proj = _inproj(x, row(norm_mix_g[l]), w_in_p)
        new_shift = _rwkv_unlayout(proj.reshape(b, t, PROJ_PAD)[:, -1, :RWKV_PAD])
        y_a, new_wkv = _rwkv(proj, shift0, wkv0, rwkv_wts, b, t, wkv_chunk, nb)
        y_b, new_ret = _retention(proj, ret0, b, t, ret_chunk, nb, pos0)
        x1, h2, ri, rg = _outproj(x, y_a, y_b, w_out_a, w_out_b, row(norm_ffn_g[l]), w_router)
        mixed.append((x1, h2, ri, rg, new_shift, new_wkv, new_ret))

    n_tok = [m[0].shape[0] for m in mixed]
    n_all = sum(n_tok)
    n_blk = -(-2 * n_all // MOE_BLK) + N_EXPERTS
    dest, blk = _sort(jnp.concatenate([m[2] for m in mixed], axis=0), n_blk)
    blk_e, n_used = blk[:n_blk, 0], blk[0:1, 1]
    xs = jnp.zeros((n_blk * MOE_BLK, d), F32)
    off = 0
    dests = []
    for m, n in zip(mixed, n_tok):
        d1, d2 = dest[off:off + n, 0], dest[off:off + n, 1]
        dests.append((d1, d2))
        xs = _dispatch(d1, d2, m[1], xs)
        off += n
    yb = _experts(blk_e, n_used, xs, exp_w_gate[l], exp_w_up[l], exp_w_down[l])
    outs = [_combine(d1, d2, m[0], m[3], row(norm_final_g), yb)
            for m, (d1, d2) in zip(mixed, dests)]

    (_, _, _, _, p_shift, p_wkv, p_ret), (_, _, _, _, s_shift, s_wkv, s_ret) = mixed
    return (outs[0].reshape(bp, tp, d), outs[1].reshape(bs, ts, d),
            p_shift[None], p_wkv[None], p_ret[None], s_shift[None], s_wkv[None], s_ret[None])
```

```python
import functools
import math

import jax
import jax.numpy as jnp
from jax import lax
from jax.experimental import pallas as pl
from jax.experimental.pallas import tpu as pltpu

F32 = jnp.float32
BF16 = jnp.bfloat16

D_MODEL = 2048
A_W = 1024
A_HD = 64
A_HEADS = 16
W_LORA = 64
A_LORA = 64
G_LORA = 160
RWKV_PROJ = 3 * A_W + W_LORA + A_LORA + G_LORA
R_W = 1024
R_HEADS = 8
R_HD = 128
RET_CHUNK = 128
N_GROUPS = 4
E_PER_GROUP = 8
N_EXPERTS = 32
D_EXPERT = 512
NORM_EPS = 1e-6
LNX_EPS = 64e-5
DECAY_SCALE = math.exp(-0.5)
PAST_LEN = 16384

LANES = 128
OFF_WD = 3 * A_W
OFF_AD = OFF_WD + LANES
OFF_GD = OFF_AD + LANES
RWKV_PAD = OFF_GD + 2 * LANES
PROJ_PAD = RWKV_PAD + 4 * R_W
RET_BLK = 512
WKV_CHUNK = 64
MOE_BLK = 256
VMEM_LIMIT = 56 * 1024 * 1024


def _cparams(sem):
    return pltpu.CompilerParams(dimension_semantics=sem, vmem_limit_bytes=VMEM_LIMIT)


def _bdot(a, b):
    return jnp.dot(a.astype(BF16), b.astype(BF16), preferred_element_type=F32)


def _bdot_nt(a, b):
    return lax.dot_general(a.astype(BF16), b.astype(BF16), (((1,), (1,)), ((), ())),
                           preferred_element_type=F32)


def _bdot_tn(a, b):
    return lax.dot_general(a.astype(BF16), b.astype(BF16), (((0,), (0,)), ((), ())),
                           preferred_element_type=F32)


def _split3(x):
    hi = x.astype(BF16)
    r1 = x - hi.astype(F32)
    mid = r1.astype(BF16)
    lo = (r1 - mid.astype(F32)).astype(BF16)
    return hi, mid, lo


def _dot3(x, w_bf16):
    hi, mid, lo = _split3(x)
    d = lambda p: jnp.dot(p, w_bf16, preferred_element_type=F32)
    return d(hi) + d(mid) + d(lo)


def _sigmoid(x):
    return 1.0 / (1.0 + jnp.exp(-x))


def _pack_bf16_pairs(x):
    w = x.shape[1] // 2
    lo = lax.bitcast_convert_type(x[:, :w].astype(BF16).astype(F32), jnp.uint32)
    hi = lax.bitcast_convert_type(x[:, w:].astype(BF16).astype(F32), jnp.uint32)
    return hi | (lo >> 16)


def _unpack_bf16_pairs(p):
    lo = lax.bitcast_convert_type(p << 16, F32)
    hi = lax.bitcast_convert_type(p & jnp.uint32(0xFFFF0000), F32)
    return lo, hi


def _inproj_kernel(x_ref, g_ref, w_ref, o_ref, h_ref):
    @pl.when(pl.program_id(1) == 0)
    def _():
        x = x_ref[...]
        ms = jnp.mean(x * x, axis=-1, keepdims=True)
        h_ref[...] = (x * lax.rsqrt(ms + NORM_EPS) * g_ref[...]).astype(BF16)

    o_ref[...] = jnp.dot(h_ref[...], w_ref[...], preferred_element_type=F32).astype(o_ref.dtype)


def _inproj(x, g, w_bf16, out_dtype, n_cols, tm, tn):
    n, d = x.shape
    return pl.pallas_call(
        _inproj_kernel,
        grid=(n // tm, n_cols // tn),
        in_specs=[pl.BlockSpec((tm, d), lambda i, j: (i, 0)),
                  pl.BlockSpec((1, d), lambda i, j: (0, 0)),
                  pl.BlockSpec((d, tn), lambda i, j: (0, j))],
        out_specs=pl.BlockSpec((tm, tn), lambda i, j: (i, j)),
        out_shape=jax.ShapeDtypeStruct((n, n_cols), out_dtype),
        scratch_shapes=[pltpu.VMEM((tm, d), BF16)],
        compiler_params=_cparams(("parallel", "arbitrary")),
        name="inproj",
    )(x, g, w_bf16)


def _seg_sum(x, ones_bd):
    nslab = x.shape[1] // LANES
    stacked = jnp.concatenate([x[:, s * LANES:(s + 1) * LANES] for s in range(nslab)], axis=0)
    red = _dot3(stacked, ones_bd)
    c = x.shape[0]
    return jnp.concatenate([red[s * c:(s + 1) * c] for s in range(nslab)], axis=1)


def _rwkv_kernel(p_ref, shift_ref, s0_ref, mu_ref, w0_ref, wup_ref, a0_ref, aup_ref, gup_ref,
                 kk_ref, ka_ref, rk_ref, lng_ref, lnb_ref, y_ref, sout_ref, sbd_ref, carry_ref,
                 *, chunk, nb, n_chunks):
    C = chunk
    R = nb * C
    c_idx = pl.program_id(1)
    n_pairs = A_HEADS // 2
    pairs = range(n_pairs)
    seqs = range(nb)
    lane = lax.broadcasted_iota(jnp.int32, (1, LANES), 1)
    in_a = lane < A_HD

    @pl.when(c_idx == 0)
    def _():
        carry_ref[...] = shift_ref[...]
        z = jnp.zeros((A_HD, A_HD), F32)
        for s in seqs:
            for pr in pairs:
                sa = s0_ref[s, 2 * pr]
                sb = s0_ref[s, 2 * pr + 1]
                sbd_ref[s, pr] = jnp.concatenate(
                    [jnp.concatenate([sa, z], axis=1), jnp.concatenate([z, sb], axis=1)], axis=0)

    p = p_ref[...].astype(F32)
    row = lax.broadcasted_iota(jnp.int32, (R, 1), 0)
    pos = row % C
    first = jnp.broadcast_to(carry_ref[...], (nb, C, RWKV_PAD)).reshape(R, RWKV_PAD)
    prev = jnp.where(pos == 0, first, pltpu.roll(p, 1, axis=0))
    if n_chunks > 1:
        carry_ref[0] = p[C - 1:C, :]
    xm = p + (prev - p) * mu_ref[...]

    r = xm[:, 0:A_W]
    k = xm[:, A_W:2 * A_W]
    v = xm[:, 2 * A_W:3 * A_W]
    wd = xm[:, OFF_WD:OFF_AD]
    ad = xm[:, OFF_AD:OFF_GD]
    gd = xm[:, OFF_GD:RWKV_PAD]

    logw = -DECAY_SCALE * _sigmoid(w0_ref[...] + _bdot(jnp.tanh(wd), wup_ref[...]))
    a = _sigmoid(a0_ref[...] + _bdot(ad, aup_ref[...]))
    g = _bdot(_sigmoid(gd), gup_ref[...])

    ones_bd = (lax.broadcasted_iota(jnp.int32, (LANES, LANES), 0) // A_HD ==
               lax.broadcasted_iota(jnp.int32, (LANES, LANES), 1) // A_HD).astype(BF16)

    kk = k * kk_ref[...]
    k2 = k * (1.0 + (a - 1.0) * ka_ref[...])
    sums = _seg_sum(jnp.concatenate([kk * kk, r * k2 * rk_ref[...]], axis=0), ones_bd)
    kk = kk / jnp.maximum(jnp.sqrt(sums[0:R]), 1e-12)
    bonus = sums[R:2 * R]
    b = kk * a

    cum = logw
    suf = logw
    d = 1
    while d < C:
        cum = cum + jnp.where(pos >= d, pltpu.roll(cum, d, axis=0), 0.0)
        suf = suf + jnp.where(pos < C - d, pltpu.roll(suf, R - d, axis=0), 0.0)
        d *= 2
    after = suf - logw
    w_incl = jnp.exp(cum)
    w_excl = jnp.exp(cum - logw)
    w_inv = jnp.exp(-cum)
    w_end = jnp.exp(after)
    w_all = jnp.exp(cum + after)

    kk_d = kk * w_excl
    r_d = r * w_incl
    b_h = b * w_inv
    k_h = k2 * w_inv
    b_e = b * w_end
    k_e = k2 * w_end

    R2 = 2 * R
    ri = lax.broadcasted_iota(jnp.int32, (R2, R2), 0)
    ci = lax.broadcasted_iota(jnp.int32, (R2, R2), 1)
    same = (ri // C) == (ci // C)
    strict = same & (ci < ri)
    incl = same & (ci <= ri)
    eye = (ri == ci).astype(F32)

    def expand(x):
        return jnp.concatenate([jnp.where(in_a, x, 0.0), jnp.where(in_a, 0.0, x)], axis=0)

    def seq_rows(x, s):
        if nb == 1:
            return x
        return jnp.concatenate([x[s * C:(s + 1) * C], x[R + s * C:R + (s + 1) * C]], axis=0)

    def from_seqs(parts):
        if nb == 1:
            return parts[0]
        return jnp.concatenate([q[0:C] for q in parts] + [q[C:2 * C] for q in parts], axis=0)

    sl = [slice(pr * LANES, (pr + 1) * LANES) for pr in pairs]
    ex = lambda x: [expand(x[:, sl[pr]]) for pr in pairs]
    e_kk, e_r, e_bh, e_kh, e_be, e_ke = ex(kk_d), ex(r_d), ex(b_h), ex(k_h), ex(b_e), ex(k_e)
    v_e = [t.astype(BF16) for t in ex(v)]
    kk_bf = [t.astype(BF16) for t in e_kk]
    r_bf = [t.astype(BF16) for t in e_r]
    gram = [_bdot_nt(jnp.concatenate([kk_bf[pr], r_bf[pr]], axis=0),
                     jnp.concatenate([e_bh[pr], e_kh[pr]], axis=0)) for pr in pairs]
    l_b = [jnp.where(strict, gram[pr][0:R2, 0:R2], 0.0) for pr in pairs]
    l_k = [jnp.where(strict, gram[pr][0:R2, R2:2 * R2], 0.0) for pr in pairs]
    a_bk = [jnp.concatenate([jnp.where(incl, gram[pr][R2:2 * R2, 0:R2], 0.0),
                             jnp.where(incl, gram[pr][R2:2 * R2, R2:2 * R2], 0.0)],
                            axis=1).astype(BF16) for pr in pairs]
    t_inv = [eye - l_b[pr] for pr in pairs]
    pw = [t.astype(BF16) for t in l_b]
    e = 2
    while e < C:
        pw = [_bdot(pw[pr], pw[pr]).astype(BF16) for pr in pairs]
        t_inv = [t_inv[pr] + _bdot(t_inv[pr], pw[pr]) for pr in pairs]
        e *= 2
    lkv = [_bdot(l_k[pr], v_e[pr]) for pr in pairs]
    xs_kk, xs_r = [], []
    for pr in pairs:
        parts = [_bdot_nt(jnp.concatenate([seq_rows(kk_bf[pr], s), seq_rows(r_bf[pr], s)], axis=0),
                          sbd_ref[s, pr]) for s in seqs]
        xs_kk.append(from_seqs([q[0:2 * C] for q in parts]))
        xs_r.append(from_seqs([q[2 * C:4 * C] for q in parts]))
    u_e = [-_bdot(t_inv[pr], xs_kk[pr] + lkv[pr]).astype(BF16) for pr in pairs]
    uv = [jnp.concatenate([u_e[pr], v_e[pr]], axis=0) for pr in pairs]
    y_e = [xs_r[pr] + _bdot(a_bk[pr], uv[pr]) for pr in pairs]
    y = jnp.concatenate([y_e[pr][0:R] + y_e[pr][R:R2] for pr in pairs], axis=1)
    for pr in pairs:
        be_bf, ke_bf = e_be[pr].astype(BF16), e_ke[pr].astype(BF16)
        for s in seqs:
            upd = _bdot_tn(jnp.concatenate([seq_rows(u_e[pr], s), seq_rows(v_e[pr], s)], axis=0),
                           jnp.concatenate([seq_rows(be_bf, s), seq_rows(ke_bf, s)], axis=0))
            sbd_ref[s, pr] = sbd_ref[s, pr] * w_all[s * C:s * C + 1, sl[pr]] + upd

    inv_hd = 1.0 / A_HD
    mean = _seg_sum(y, ones_bd) * inv_hd
    yc = y - mean
    var = _seg_sum(yc * yc, ones_bd) * inv_hd
    yn = yc * lax.rsqrt(var + LNX_EPS) * lng_ref[...] + lnb_ref[...]
    yn = yn + bonus * v
    y_ref[...] = (yn * g).astype(y_ref.dtype)

    @pl.when(c_idx == pl.num_programs(1) - 1)
    def _():
        for s in seqs:
            for pr in pairs:
                s_bd = sbd_ref[s, pr]
                sout_ref[s, 2 * pr] = s_bd[0:A_HD, 0:A_HD]
                sout_ref[s, 2 * pr + 1] = s_bd[A_HD:2 * A_HD, A_HD:2 * A_HD]


def _rwkv(proj, shift0, s0, wts, batch, seq, chunk, nb):
    n_chunks = seq // chunk
    assert nb == 1 or n_chunks == 1
    rows = nb * chunk
    vec = lambda w: pl.BlockSpec(w.shape, lambda b, c: (0,) * w.ndim)
    st = pl.BlockSpec((nb, A_HEADS, A_HD, A_HD), lambda b, c: (b, 0, 0, 0))
    return pl.pallas_call(
        functools.partial(_rwkv_kernel, chunk=chunk, nb=nb, n_chunks=n_chunks),
        grid=(batch // nb, n_chunks),
        in_specs=[pl.BlockSpec((rows, RWKV_PAD), lambda b, c: (b * n_chunks + c, 0)),
                  pl.BlockSpec((nb, 1, RWKV_PAD), lambda b, c: (b, 0, 0)), st]
                 + [vec(w) for w in wts],
        out_specs=[pl.BlockSpec((rows, A_W), lambda b, c: (b * n_chunks + c, 0)), st],
        out_shape=[jax.ShapeDtypeStruct((batch * seq, A_W), BF16),
                   jax.ShapeDtypeStruct((batch, A_HEADS, A_HD, A_HD), F32)],
        scratch_shapes=[pltpu.VMEM((nb, A_HEADS // 2, LANES, LANES), F32),
                        pltpu.VMEM((nb, 1, RWKV_PAD), F32)],
        compiler_params=_cparams(("parallel", "arbitrary")),
        name="rwkv7",
    )(proj, shift0.reshape(batch, 1, RWKV_PAD), s0, *wts)


def _ret_kernel(q0_ref, q1_ref, k0_ref, k1_ref, v0_ref, v1_ref, g0_ref, g1_ref, cos_ref, sin_ref,
                din_ref, dq_ref, dk_ref, dc_ref, r0_ref, y_ref, rout_ref, r_scr, *, chunk, nb):
    C = chunk
    half = RET_BLK // R_HD
    c_idx = pl.program_id(1)

    @pl.when(c_idx == 0)
    def _():
        r_scr[...] = r0_ref[...]

    even = lax.broadcasted_iota(jnp.int32, (1, R_HD), 1) % 2 == 0
    cos = cos_ref[...]
    sin = sin_ref[...]

    def rot(t):
        partner = jnp.where(even, pltpu.roll(t, R_HD - 1, axis=1), pltpu.roll(t, 1, axis=1))
        return t * cos + partner * sin

    def load(halves, s, h):
        return halves[h // half][s * C:(s + 1) * C, (h % half) * R_HD:(h % half + 1) * R_HD]

    f32 = lambda *refs: tuple(r[...].astype(F32) for r in refs)
    items = [(s, h) for s in range(nb) for h in range(R_HEADS)]
    q_in, k_in, v_in, g_in = f32(q0_ref, q1_ref), f32(k0_ref, k1_ref), f32(v0_ref, v1_ref), f32(g0_ref, g1_ref)
    q = [rot(load(q_in, s, h)) for s, h in items]
    k = [rot(load(k_in, s, h)) * (R_HD ** -0.5) for s, h in items]
    v = [load(v_in, s, h).astype(BF16) for s, h in items]
    state = [r_scr[s, h] for s, h in items]
    scores = [_bdot_nt(q[i], k[i]) * din_ref[h] for i, (s, h) in enumerate(items)]
    o = [_bdot(jnp.concatenate([scores[i], q[i] * dq_ref[h]], axis=1),
               jnp.concatenate([v[i], state[i].astype(BF16)], axis=0))
         for i, (s, h) in enumerate(items)]
    for i, (s, h) in enumerate(items):
        r_scr[s, h] = state[i] * dc_ref[h, 0:1, :] + _bdot_tn(k[i] * dk_ref[h], v[i])
    for i, (s, h) in enumerate(items):
        on = o[i] * lax.rsqrt(jnp.mean(o[i] * o[i], axis=-1, keepdims=True) + NORM_EPS)
        gate = load(g_in, s, h)
        y_ref[s * C:(s + 1) * C, h * R_HD:(h + 1) * R_HD] = (
            gate * _sigmoid(gate) * on).astype(y_ref.dtype)

    @pl.when(c_idx == pl.num_programs(1) - 1)
    def _():
        rout_ref[...] = r_scr[...]


def _ret_tables(chunk, pos0, seq):
    pos = pos0 + jnp.arange(seq, dtype=F32)
    theta = 1.0 / (10000.0 ** jnp.linspace(0.0, 1.0, R_HD // 2, dtype=F32))
    ang = pos[:, None] * theta[None, :]
    cos2 = jnp.repeat(jnp.cos(ang), 2, axis=1)
    sin2 = jnp.stack([-jnp.sin(ang), jnp.sin(ang)], axis=-1).reshape(seq, R_HD)
    lg = jnp.log1p(-(2.0 ** (-5.0 - jnp.arange(R_HEADS, dtype=F32))))
    idx = jnp.arange(chunk, dtype=F32)
    diff = idx[:, None] - idx[None, :]
    causal = diff >= 0
    din = jnp.where(causal, jnp.exp(jnp.where(causal, diff, 0.0)[None] * lg[:, None, None]), 0.0)
    bc = lambda t: jnp.broadcast_to(t[:, :, None], (R_HEADS, t.shape[1], R_HD))
    dq = bc(jnp.exp((idx + 1.0)[None, :] * lg[:, None]))
    dk = bc(jnp.exp((chunk - 1.0 - idx)[None, :] * lg[:, None]))
    dc = jnp.broadcast_to(jnp.exp(chunk * lg)[:, None, None], (R_HEADS, 8, R_HD))
    return cos2, sin2, din, dq, dk, dc


def _retention(proj, r0, batch, seq, chunk, nb, pos0):
    n_chunks = seq // chunk
    cos2, sin2, din, dq, dk, dc = _ret_tables(chunk, pos0, seq)
    col0 = RWKV_PAD // RET_BLK
    rows = nb * chunk

    def sec(i):
        return pl.BlockSpec((rows, RET_BLK), lambda b, c: (b * n_chunks + c, col0 + i))

    tab = lambda t: pl.BlockSpec(t.shape, lambda b, c: (0, 0, 0))
    st = pl.BlockSpec((nb, R_HEADS, R_HD, R_HD), lambda b, c: (b, 0, 0, 0))
    return pl.pallas_call(
        functools.partial(_ret_kernel, chunk=chunk, nb=nb),
        grid=(batch // nb, n_chunks),
        in_specs=[sec(i) for i in range(8)]
                 + [pl.BlockSpec((chunk, R_HD), lambda b, c: (c, 0)),
                    pl.BlockSpec((chunk, R_HD), lambda b, c: (c, 0)),
                    tab(din), tab(dq), tab(dk), tab(dc), st],
        out_specs=[pl.BlockSpec((rows, R_W), lambda b, c: (b * n_chunks + c, 0)), st],
        out_shape=[jax.ShapeDtypeStruct((batch * seq, R_W), BF16),
                   jax.ShapeDtypeStruct((batch, R_HEADS, R_HD, R_HD), F32)],
        scratch_shapes=[pltpu.VMEM((nb, R_HEADS, R_HD, R_HD), F32)],
        compiler_params=_cparams(("parallel", "arbitrary")),
        name="retention",
    )(*([proj] * 8), cos2, sin2, din, dq, dk, dc, r0)


def _outproj_kernel(x_ref, ya_ref, yb_ref, wa_ref, wb_ref, g_ref, wr_ref,
                    x1_ref, h2_ref, ri_ref, rg_ref):
    x1 = (x_ref[...] + jnp.dot(ya_ref[...], wa_ref[...], preferred_element_type=F32)
          + jnp.dot(yb_ref[...], wb_ref[...], preferred_element_type=F32))
    x1_ref[...] = x1
    ms = jnp.mean(x1 * x1, axis=-1, keepdims=True)
    h2 = x1 * lax.rsqrt(ms + NORM_EPS) * g_ref[...]
    h2_ref[...] = _pack_bf16_pairs(h2)
    logits = _bdot(h2, wr_ref[...])
    lane = lax.broadcasted_iota(jnp.int32, (1, LANES), 1)
    neg = -1e30
    big = LANES
    rmax = lambda t: jnp.max(t, axis=-1, keepdims=True)
    rmin = lambda t: jnp.min(t, axis=-1, keepdims=True)
    is_c = lane < N_GROUPS
    lc = jnp.where(is_c, logits, neg)
    mc = rmax(lc)
    grp = rmin(jnp.where(lc == mc, lane, big))
    p_grp = 1.0 / jnp.sum(jnp.where(is_c, jnp.exp(lc - mc), 0.0), axis=-1, keepdims=True)
    fine = lane - N_GROUPS
    in_g = (fine >= 0) & (fine < N_EXPERTS) & ((fine // E_PER_GROUP) == grp)
    lf = jnp.where(in_g, logits, neg)
    m1 = rmax(lf)
    i1 = rmin(jnp.where(lf == m1, lane, big))
    lf2 = jnp.where(lane == i1, neg, lf)
    m2 = rmax(lf2)
    i2 = rmin(jnp.where(lf2 == m2, lane, big))
    e2 = jnp.exp(m2 - m1)
    g1 = p_grp / (1.0 + e2)
    g2 = p_grp * e2 / (1.0 + e2)
    ri_ref[...] = jnp.where(lane == 0, i1 - N_GROUPS, jnp.where(lane == 1, i2 - N_GROUPS, 0))
    rg_ref[...] = jnp.where(lane == 0, g1, jnp.where(lane == 1, g2, 0.0))


def _outproj(x, ya, yb, wa, wb, g, wr, tm=512):
    n, d = x.shape
    rowblk = lambda w: pl.BlockSpec((tm, w), lambda i: (i, 0))
    full = lambda w: pl.BlockSpec(w.shape, lambda i: (0, 0))
    return pl.pallas_call(
        _outproj_kernel,
        grid=(n // tm,),
        in_specs=[rowblk(d), rowblk(A_W), rowblk(R_W), full(wa), full(wb), full(g), full(wr)],
        out_specs=[rowblk(d), rowblk(d // 2), rowblk(LANES), rowblk(LANES)],
        out_shape=[jax.ShapeDtypeStruct((n, d), F32), jax.ShapeDtypeStruct((n, d // 2), jnp.uint32),
                   jax.ShapeDtypeStruct((n, LANES), jnp.int32),
                   jax.ShapeDtypeStruct((n, LANES), F32)],
        compiler_params=_cparams(("parallel",)),
        name="outproj_router",
    )(x, ya, yb, wa, wb, g, wr)


def _sort_kernel(ri_ref, tri_ref, dest_ref, info_ref, cnt_ref, base_ref):
    ph = pl.program_id(0)
    i = pl.program_id(1)
    lane = lax.broadcasted_iota(jnp.int32, (1, LANES), 1)
    ri = ri_ref[...]
    oh1 = (lane == ri[:, 0:1]).astype(F32)
    oh2 = (lane == ri[:, 1:2]).astype(F32)
    tot1 = jnp.sum(oh1, axis=0, keepdims=True)
    tot2 = jnp.sum(oh2, axis=0, keepdims=True)

    @pl.when(ph == 0)
    def _():
        @pl.when(i == 0)
        def _():
            cnt_ref[...] = jnp.zeros_like(cnt_ref)
        cnt_ref[...] += tot1 + tot2

    @pl.when(ph == 1)
    def _():
        @pl.when(i == 0)
        def _():
            nblk = jnp.floor((cnt_ref[...] + (MOE_BLK - 1)) * (1.0 / MOE_BLK))
            upper = (lax.broadcasted_iota(jnp.int32, (LANES, LANES), 0) <=
                     lax.broadcasted_iota(jnp.int32, (LANES, LANES), 1))
            pend = _bdot(jnp.broadcast_to(nblk, (8, LANES)), upper.astype(F32))[0:1]
            first = pend - nblk
            base_ref[...] = first * MOE_BLK
            srow = lax.broadcasted_iota(jnp.int32, (8, LANES), 0)
            info_ref[...] = jnp.where(srow == 0, nblk, jnp.where(srow == 1, first, 0.0)).astype(jnp.int32)

        base = base_ref[...]
        tri = tri_ref[...]
        pre1 = jnp.dot(tri, oh1.astype(BF16), preferred_element_type=F32)
        pre2 = jnp.dot(tri, oh2.astype(BF16), preferred_element_type=F32) + tot1
        d1 = jnp.sum(oh1 * (base + pre1), axis=1, keepdims=True)
        d2 = jnp.sum(oh2 * (base + pre2), axis=1, keepdims=True)
        dest_ref[...] = jnp.where(lane == 0, d1, jnp.where(lane == 1, d2, 0.0)).astype(jnp.int32)
        base_ref[...] = base + tot1 + tot2


def _sort(ri, tt=512):
    n = ri.shape[0]
    tri = (jnp.arange(tt)[:, None] > jnp.arange(tt)[None, :]).astype(BF16)
    return pl.pallas_call(
        _sort_kernel,
        grid=(2, n // tt),
        in_specs=[pl.BlockSpec((tt, LANES), lambda ph, i: (i, 0)),
                  pl.BlockSpec((tt, tt), lambda ph, i: (0, 0))],
        out_specs=[pl.BlockSpec((tt, LANES), lambda ph, i: (i * ph, 0)),
                   pl.BlockSpec((8, LANES), lambda ph, i: (0, 0))],
        out_shape=[jax.ShapeDtypeStruct((n, LANES), jnp.int32),
                   jax.ShapeDtypeStruct((8, LANES), jnp.int32)],
        scratch_shapes=[pltpu.VMEM((1, LANES), F32), pltpu.VMEM((1, LANES), F32)],
        compiler_params=_cparams(("arbitrary", "arbitrary")),
        name="moe_sort",
    )(ri, tri)


def _row_copy(src_ref, src_row, dst_ref, dst_row, sem):
    return pltpu.make_async_copy(src_ref.at[pl.ds(src_row, 1)], dst_ref.at[pl.ds(dst_row, 1)], sem)


def _dispatch_kernel(d1_ref, d2_ref, h_ref, xs_in_ref, xs_ref, sem, *, tt):
    del xs_in_ref

    def issue(t, carry):
        _row_copy(h_ref, t, xs_ref, d1_ref[t], sem).start()
        _row_copy(h_ref, t, xs_ref, d2_ref[t], sem).start()
        return carry

    def drain(t, carry):
        _row_copy(h_ref, t, xs_ref, d1_ref[t], sem).wait()
        _row_copy(h_ref, t, xs_ref, d2_ref[t], sem).wait()
        return carry

    lax.fori_loop(0, tt, issue, 0, unroll=8)
    lax.fori_loop(0, tt, drain, 0, unroll=8)


def _dispatch(d1, d2, h2, xs, tt=512):
    n, d = h2.shape
    smem = pl.BlockSpec((tt,), lambda i: (i,), memory_space=pltpu.SMEM)
    anyspec = pl.BlockSpec(memory_space=pl.ANY)
    return pl.pallas_call(
        functools.partial(_dispatch_kernel, tt=tt),
        grid=(n // tt,),
        in_specs=[smem, smem, pl.BlockSpec((tt, d), lambda i: (i, 0)), anyspec],
        out_specs=anyspec,
        out_shape=jax.ShapeDtypeStruct(xs.shape, xs.dtype),
        scratch_shapes=[pltpu.SemaphoreType.DMA(())],
        input_output_aliases={3: 0},
        compiler_params=_cparams(("arbitrary",)),
        name="moe_dispatch",
    )(d1, d2, h2, xs)


def _expert_kernel(nb_ref, fb_ref, xs_hbm, wg_ref, wu_ref, wd_ref, y_hbm,
                   wg_s, wu_s, wd_s, xbuf, ybuf, sem_in, sem_out, *, n_blk):
    e = pl.program_id(0)
    n = nb_ref[e]
    b0 = fb_ref[e]
    half = xbuf.shape[2]

    def load(j, slot):
        return pltpu.make_async_copy(xs_hbm.at[pl.ds((b0 + j) * MOE_BLK, MOE_BLK)],
                                     xbuf.at[slot], sem_in.at[slot])

    def store(blk, slot):
        return pltpu.make_async_copy(ybuf.at[slot], y_hbm.at[pl.ds(blk * MOE_BLK, MOE_BLK)],
                                     sem_out.at[slot])

    @pl.when(n > 0)
    def _():
        load(0, 0).start()
        wg_s[...] = wg_ref[0].astype(BF16)
        wu_s[...] = wu_ref[0].astype(BF16)
        wd_s[...] = wd_ref[0].astype(BF16)

        def body(j, carry):
            slot = j % 2
            load(j, slot).wait()

            @pl.when(j + 1 < n)
            def _():
                load(j + 1, 1 - slot).start()

            @pl.when(j >= 2)
            def _():
                store(b0 + j - 2, slot).wait()

            x_lo, x_hi = _unpack_bf16_pairs(xbuf[slot])
            x_lo, x_hi = x_lo.astype(BF16), x_hi.astype(BF16)
            mm = lambda w_s: (jnp.dot(x_lo, w_s[0:half], preferred_element_type=F32)
                              + jnp.dot(x_hi, w_s[half:2 * half], preferred_element_type=F32))
            hg = mm(wg_s)
            hu = mm(wu_s)
            act = (hg * _sigmoid(hg) * hu).astype(BF16)
            ybuf[slot] = _pack_bf16_pairs(jnp.dot(act, wd_s[...], preferred_element_type=F32))
            store(b0 + j, slot).start()
            return carry

        lax.fori_loop(0, n, body, 0)

        @pl.when(n >= 2)
        def _():
            store(b0 + n - 2, n % 2).wait()
        store(b0 + n - 1, (n - 1) % 2).wait()

    @pl.when(e == pl.num_programs(0) - 1)
    def _():
        ybuf[0] = jnp.zeros(ybuf.shape[1:], ybuf.dtype)

        def fill(blk, carry):
            cp = store(blk, 0)
            cp.start()
            cp.wait()
            return carry

        lax.fori_loop(b0 + n, n_blk, fill, 0)


def _experts(n_blocks, first_block, xs, wg, wu, wd):
    p, half = xs.shape
    d = 2 * half
    wsel = lambda e, nb, fb: (e, 0, 0)
    anyspec = pl.BlockSpec(memory_space=pl.ANY)
    grid_spec = pltpu.PrefetchScalarGridSpec(
        num_scalar_prefetch=2,
        grid=(N_EXPERTS,),
        in_specs=[anyspec,
                  pl.BlockSpec((1, d, D_EXPERT), wsel),
                  pl.BlockSpec((1, d, D_EXPERT), wsel),
                  pl.BlockSpec((1, D_EXPERT, d), wsel)],
        out_specs=anyspec,
        scratch_shapes=[pltpu.VMEM((d, D_EXPERT), BF16), pltpu.VMEM((d, D_EXPERT), BF16),
                        pltpu.VMEM((D_EXPERT, d), BF16),
                        pltpu.VMEM((2, MOE_BLK, half), jnp.uint32),
                        pltpu.VMEM((2, MOE_BLK, half), jnp.uint32),
                        pltpu.SemaphoreType.DMA((2,)), pltpu.SemaphoreType.DMA((2,))])
    return pl.pallas_call(
        functools.partial(_expert_kernel, n_blk=p // MOE_BLK),
        grid_spec=grid_spec,
        out_shape=jax.ShapeDtypeStruct((p, half), jnp.uint32),
        compiler_params=_cparams(("arbitrary",)),
        name="moe_experts",
    )(n_blocks, first_block, xs, wg, wu, wd)


def _combine_kernel(d1_ref, d2_ref, x1_ref, rg_ref, g_ref, yb_ref, o_ref, buf, sem, *, tt):
    def issue(t, carry):
        _row_copy(yb_ref, d1_ref[t], buf.at[0], t, sem).start()
        _row_copy(yb_ref, d2_ref[t], buf.at[1], t, sem).start()
        return carry

    def drain(t, carry):
        _row_copy(yb_ref, d1_ref[t], buf.at[0], t, sem).wait()
        _row_copy(yb_ref, d2_ref[t], buf.at[1], t, sem).wait()
        return carry

    lax.fori_loop(0, tt, issue, 0, unroll=8)
    lax.fori_loop(0, tt, drain, 0, unroll=8)
    rg = rg_ref[...]
    lo1, hi1 = _unpack_bf16_pairs(buf[0])
    lo2, hi2 = _unpack_bf16_pairs(buf[1])
    g1, g2 = rg[:, 0:1], rg[:, 1:2]
    x = x1_ref[...] + jnp.concatenate([g1 * lo1 + g2 * lo2, g1 * hi1 + g2 * hi2], axis=1)
    ms = jnp.mean(x * x, axis=-1, keepdims=True)
    o_ref[...] = x * lax.rsqrt(ms + NORM_EPS) * g_ref[...]


def _combine(d1, d2, x1, rg, g, yb, tt=256):
    n, d = x1.shape
    smem = pl.BlockSpec((tt,), lambda i: (i,), memory_space=pltpu.SMEM)
    rowblk = lambda w: pl.BlockSpec((tt, w), lambda i: (i, 0))
    return pl.pallas_call(
        functools.partial(_combine_kernel, tt=tt),
        grid=(n // tt,),
        in_specs=[smem, smem, rowblk(d), rowblk(LANES), pl.BlockSpec((1, d), lambda i: (0, 0)),
                  pl.BlockSpec(memory_space=pl.ANY)],
        out_specs=rowblk(d),
        out_shape=jax.ShapeDtypeStruct((n, d), F32),
        scratch_shapes=[pltpu.VMEM((2, tt, d // 2), jnp.uint32), pltpu.SemaphoreType.DMA(())],
        compiler_params=_cparams(("arbitrary",)),
        name="moe_combine",
    )(d1, d2, x1, rg, g, yb)


def _pad_cols(x, width):
    return jnp.pad(x, ((0, 0), (0, width - x.shape[1])))


def _rwkv_layout(x):
    return jnp.concatenate([
        x[:, :OFF_WD],
        _pad_cols(x[:, 3 * A_W:3 * A_W + W_LORA], LANES),
        _pad_cols(x[:, 3 * A_W + W_LORA:3 * A_W + W_LORA + A_LORA], LANES),
        _pad_cols(x[:, 3 * A_W + W_LORA + A_LORA:RWKV_PROJ], 2 * LANES)], axis=1)


def _rwkv_unlayout(x):
    return jnp.concatenate([x[:, :OFF_WD + W_LORA], x[:, OFF_AD:OFF_AD + A_LORA],
                            x[:, OFF_GD:OFF_GD + G_LORA]], axis=1)


def _pad_rows(w, rows):
    return jnp.pad(w, ((0, rows - w.shape[0]), (0, 0)))


def kernel(x_prompt, x_sample, state_rwkv_shift, state_rwkv_wkv, state_retention, norm_mix_g, w_in, mu_shift, decay_w0, decay_up, iclr_a0, iclr_up, gate_up, k_k, k_a, r_k, lnx_g, lnx_b, w_out, norm_ffn_g, router_coarse, router_fine, exp_w_gate, exp_w_up, exp_w_down, norm_final_g):
    bp, tp, d = x_prompt.shape
    bs, ts, _ = x_sample.shape
    row = lambda t: t.reshape(1, -1)
    l = 0

    w_in_p = jnp.concatenate([_rwkv_layout(w_in[l][:, :RWKV_PROJ]), w_in[l][:, RWKV_PROJ:]],
                             axis=1).astype(BF16)
    rwkv_wts = (_rwkv_layout(row(mu_shift[l])), row(decay_w0[l]),
                _pad_rows(decay_up[l], LANES).astype(BF16), row(iclr_a0[l]),
                _pad_rows(iclr_up[l], LANES).astype(BF16),
                _pad_rows(gate_up[l], 2 * LANES).astype(BF16),
                row(k_k[l]), row(k_a[l]), row(r_k[l]), row(lnx_g[l]), row(lnx_b[l]))
    w_out_a = w_out[l][:A_W].astype(BF16)
    w_out_b = w_out[l][A_W:].astype(BF16)
    w_router = _pad_cols(jnp.concatenate(
        [router_coarse[l], jnp.transpose(router_fine[l], (1, 0, 2)).reshape(d, N_EXPERTS)], axis=1),
        LANES).astype(BF16)

    groups = (
        (x_prompt.reshape(bp * tp, d), bp, tp, WKV_CHUNK, RET_CHUNK, 1, 0.0,
         jnp.zeros((bp, RWKV_PAD), F32), jnp.zeros((bp, A_HEADS, A_HD, A_HD), F32),
         jnp.zeros((bp, R_HEADS, R_HD, R_HD), F32)),
        (x_sample.reshape(bs * ts, d), bs, ts, ts, math.gcd(ts, RET_CHUNK), 8, float(PAST_LEN),
         _rwkv_layout(state_rwkv_shift[l]), state_rwkv_wkv[l], state_retention[l]),
    )
    x_last = jnp.concatenate([x_prompt[:, -1, :], x_sample[:, -1, :]], axis=0)
    n_last = -(-(bp + bs) // 8) * 8
    shifts = _rwkv_unlayout(_inproj(_pad_rows(x_last, n_last), row(norm_mix_g[l]), w_in_p, F32,
                                    RWKV_PAD, n_last, RWKV_PAD // 2))
    new_shifts = (shifts[:bp], shifts[bp:bp + bs])

    mixed = []
    for (x, b, t, wkv_chunk, ret_chunk, nb, pos0, shift0, wkv0, ret0), new_shift in zip(groups, new_shifts):
        proj = _inproj(x, row(norm_mix_g[l]), w_in_p, BF16, PROJ_PAD, 1024, 1536)
        y_a, new_wkv = _rwkv(proj, shift0, wkv0, rwkv_wts, b, t, wkv_chunk, nb)
        y_b, new_ret = _retention(proj, ret0, b, t, ret_chunk, nb, pos0)
        x1, h2, ri, rg = _outproj(x, y_a, y_b, w_out_a, w_out_b, row(norm_ffn_g[l]), w_router)
        mixed.append((x1, h2, ri, rg, new_shift, new_wkv, new_ret))

    n_tok = [m[0].shape[0] for m in mixed]
    n_all = sum(n_tok)
    n_blk = -(-2 * n_all // MOE_BLK) + N_EXPERTS
    dest, info = _sort(jnp.concatenate([m[2] for m in mixed], axis=0))
    n_blocks, first_block = info[0, :N_EXPERTS], info[1, :N_EXPERTS]
    xs = jnp.zeros((n_blk * MOE_BLK, d // 2), jnp.uint32)
    off = 0
    dests = []
    for m, n in zip(mixed, n_tok):
        d1, d2 = dest[off:off + n, 0], dest[off:off + n, 1]
        dests.append((d1, d2))
        xs = _dispatch(d1, d2, m[1], xs)
        off += n
    yb = _experts(n_blocks, first_block, xs, exp_w_gate[l], exp_w_up[l], exp_w_down[l])
    outs = [_combine(d1, d2, m[0], m[3], row(norm_final_g), yb)
            for m, (d1, d2) in zip(mixed, dests)]

    (_, _, _, _, p_shift, p_wkv, p_ret), (_, _, _, _, s_shift, s_wkv, s_ret) = mixed
    return (outs[0].reshape(bp, tp, d), outs[1].reshape(bs, ts, d),
            p_shift[None], p_wkv[None], p_ret[None], s_shift[None], s_wkv[None], s_ret[None])
```

```python
import functools
import math

import jax
import jax.numpy as jnp
from jax import lax
from jax.experimental import pallas as pl
from jax.experimental.pallas import tpu as pltpu

F32 = jnp.float32
BF16 = jnp.bfloat16

D_MODEL = 2048
A_W = 1024
A_HD = 64
A_HEADS = 16
W_LORA = 64
A_LORA = 64
G_LORA = 160
RWKV_PROJ = 3 * A_W + W_LORA + A_LORA + G_LORA
R_W = 1024
R_HEADS = 8
R_HD = 128
RET_CHUNK = 128
N_GROUPS = 4
E_PER_GROUP = 8
N_EXPERTS = 32
D_EXPERT = 512
NORM_EPS = 1e-6
LNX_EPS = 64e-5
DECAY_SCALE = math.exp(-0.5)
PAST_LEN = 16384

LANES = 128
OFF_WD = 3 * A_W
OFF_AD = OFF_WD + LANES
OFF_GD = OFF_AD + LANES
RWKV_PAD = OFF_GD + 2 * LANES
PROJ_PAD = RWKV_PAD + 4 * R_W
RET_BLK = 512
WKV_CHUNK = 64
MOE_BLK = 256
VMEM_LIMIT = 56 * 1024 * 1024


def _cparams(sem):
    return pltpu.CompilerParams(dimension_semantics=sem, vmem_limit_bytes=VMEM_LIMIT)


def _bdot(a, b):
    return jnp.dot(a.astype(BF16), b.astype(BF16), preferred_element_type=F32)


def _bdot_nt(a, b):
    return lax.dot_general(a.astype(BF16), b.astype(BF16), (((1,), (1,)), ((), ())),
                           preferred_element_type=F32)


def _bdot_tn(a, b):
    return lax.dot_general(a.astype(BF16), b.astype(BF16), (((0,), (0,)), ((), ())),
                           preferred_element_type=F32)


def _split3(x):
    hi = x.astype(BF16)
    r1 = x - hi.astype(F32)
    mid = r1.astype(BF16)
    lo = (r1 - mid.astype(F32)).astype(BF16)
    return hi, mid, lo


def _dot3(x, w_bf16):
    hi, mid, lo = _split3(x)
    d = lambda p: jnp.dot(p, w_bf16, preferred_element_type=F32)
    return d(hi) + d(mid) + d(lo)


def _sigmoid(x):
    return 1.0 / (1.0 + jnp.exp(-x))


def _pack_bf16_pairs(x):
    w = x.shape[1] // 2
    lo = lax.bitcast_convert_type(x[:, :w].astype(BF16).astype(F32), jnp.uint32)
    hi = lax.bitcast_convert_type(x[:, w:].astype(BF16).astype(F32), jnp.uint32)
    return hi | (lo >> 16)


def _unpack_bf16_pairs(p):
    lo = lax.bitcast_convert_type(p << 16, F32)
    hi = lax.bitcast_convert_type(p & jnp.uint32(0xFFFF0000), F32)
    return lo, hi


def _inproj_kernel(x_ref, g_ref, w_ref, o_ref, h_ref):
    @pl.when(pl.program_id(1) == 0)
    def _():
        x = x_ref[...]
        ms = jnp.mean(x * x, axis=-1, keepdims=True)
        h_ref[...] = (x * lax.rsqrt(ms + NORM_EPS) * g_ref[...]).astype(BF16)

    o_ref[...] = jnp.dot(h_ref[...], w_ref[...], preferred_element_type=F32).astype(o_ref.dtype)


def _inproj(x, g, w_bf16, out_dtype, n_cols, tm, tn):
    n, d = x.shape
    return pl.pallas_call(
        _inproj_kernel,
        grid=(n // tm, n_cols // tn),
        in_specs=[pl.BlockSpec((tm, d), lambda i, j: (i, 0)),
                  pl.BlockSpec((1, d), lambda i, j: (0, 0)),
                  pl.BlockSpec((d, tn), lambda i, j: (0, j))],
        out_specs=pl.BlockSpec((tm, tn), lambda i, j: (i, j)),
        out_shape=jax.ShapeDtypeStruct((n, n_cols), out_dtype),
        scratch_shapes=[pltpu.VMEM((tm, d), BF16)],
        compiler_params=_cparams(("parallel", "arbitrary")),
        name="inproj",
    )(x, g, w_bf16)


def _seg_sum(x, ones_bd):
    nslab = x.shape[1] // LANES
    stacked = jnp.concatenate([x[:, s * LANES:(s + 1) * LANES] for s in range(nslab)], axis=0)
    red = _dot3(stacked, ones_bd)
    c = x.shape[0]
    return jnp.concatenate([red[s * c:(s + 1) * c] for s in range(nslab)], axis=1)


def _rwkv_kernel(p_ref, shift_ref, s0_ref, mu_ref, w0_ref, wup_ref, a0_ref, aup_ref, gup_ref,
                 kk_ref, ka_ref, rk_ref, lng_ref, lnb_ref, y_ref, sout_ref, sbd_ref, carry_ref,
                 *, chunk, nb, n_chunks):
    C = chunk
    R = nb * C
    c_idx = pl.program_id(1)
    n_pairs = A_HEADS // 2
    pairs = range(n_pairs)
    seqs = range(nb)
    lane = lax.broadcasted_iota(jnp.int32, (1, LANES), 1)
    in_a = lane < A_HD

    @pl.when(c_idx == 0)
    def _():
        carry_ref[...] = shift_ref[...]
        z = jnp.zeros((A_HD, A_HD), F32)
        for s in seqs:
            for pr in pairs:
                sa = s0_ref[s, 2 * pr]
                sb = s0_ref[s, 2 * pr + 1]
                sbd_ref[s, pr] = jnp.concatenate(
                    [jnp.concatenate([sa, z], axis=1), jnp.concatenate([z, sb], axis=1)], axis=0)

    p = p_ref[...].astype(F32)
    row = lax.broadcasted_iota(jnp.int32, (R, 1), 0)
    pos = row % C
    first = jnp.broadcast_to(carry_ref[...], (nb, C, RWKV_PAD)).reshape(R, RWKV_PAD)
    prev = jnp.where(pos == 0, first, pltpu.roll(p, 1, axis=0))
    if n_chunks > 1:
        carry_ref[0] = p[C - 1:C, :]
    xm = p + (prev - p) * mu_ref[...]

    r = xm[:, 0:A_W]
    k = xm[:, A_W:2 * A_W]
    v = xm[:, 2 * A_W:3 * A_W]
    wd = xm[:, OFF_WD:OFF_AD]
    ad = xm[:, OFF_AD:OFF_GD]
    gd = xm[:, OFF_GD:RWKV_PAD]

    logw = -DECAY_SCALE * _sigmoid(w0_ref[...] + _bdot(jnp.tanh(wd), wup_ref[...]))
    a = _sigmoid(a0_ref[...] + _bdot(ad, aup_ref[...]))
    g = _bdot(_sigmoid(gd), gup_ref[...])

    ones_bd = (lax.broadcasted_iota(jnp.int32, (LANES, LANES), 0) // A_HD ==
               lax.broadcasted_iota(jnp.int32, (LANES, LANES), 1) // A_HD).astype(BF16)

    kk = k * kk_ref[...]
    k2 = k * (1.0 + (a - 1.0) * ka_ref[...])
    sums = _seg_sum(jnp.concatenate([kk * kk, r * k2 * rk_ref[...]], axis=0), ones_bd)
    kk = kk / jnp.maximum(jnp.sqrt(sums[0:R]), 1e-12)
    bonus = sums[R:2 * R]
    b = kk * a

    cum = logw
    suf = logw
    d = 1
    while d < C:
        cum = cum + jnp.where(pos >= d, pltpu.roll(cum, d, axis=0), 0.0)
        suf = suf + jnp.where(pos < C - d, pltpu.roll(suf, R - d, axis=0), 0.0)
        d *= 2
    after = suf - logw
    w_incl = jnp.exp(cum)
    w_excl = jnp.exp(cum - logw)
    w_inv = jnp.exp(-cum)
    w_end = jnp.exp(after)
    w_all = jnp.exp(cum + after)

    kk_d = kk * w_excl
    r_d = r * w_incl
    b_h = b * w_inv
    k_h = k2 * w_inv
    b_e = b * w_end
    k_e = k2 * w_end

    R2 = 2 * R
    ri = lax.broadcasted_iota(jnp.int32, (R2, R2), 0)
    ci = lax.broadcasted_iota(jnp.int32, (R2, R2), 1)
    same = (ri // C) == (ci // C)
    strict = same & (ci < ri)
    incl = same & (ci <= ri)
    eye = (ri == ci).astype(F32)

    def expand(x):
        return jnp.concatenate([jnp.where(in_a, x, 0.0), jnp.where(in_a, 0.0, x)], axis=0)

    def seq_rows(x, s):
        if nb == 1:
            return x
        return jnp.concatenate([x[s * C:(s + 1) * C], x[R + s * C:R + (s + 1) * C]], axis=0)

    def from_seqs(parts):
        if nb == 1:
            return parts[0]
        return jnp.concatenate([q[0:C] for q in parts] + [q[C:2 * C] for q in parts], axis=0)

    sl = [slice(pr * LANES, (pr + 1) * LANES) for pr in pairs]
    ex = lambda x: [expand(x[:, sl[pr]]) for pr in pairs]
    e_kk, e_r, e_bh, e_kh, e_be, e_ke = ex(kk_d), ex(r_d), ex(b_h), ex(k_h), ex(b_e), ex(k_e)
    v_e = [t.astype(BF16) for t in ex(v)]
    kk_bf = [t.astype(BF16) for t in e_kk]
    r_bf = [t.astype(BF16) for t in e_r]
    gram = [_bdot_nt(jnp.concatenate([kk_bf[pr], r_bf[pr]], axis=0),
                     jnp.concatenate([e_bh[pr], e_kh[pr]], axis=0)) for pr in pairs]
    l_b = [jnp.where(strict, gram[pr][0:R2, 0:R2], 0.0) for pr in pairs]
    l_k = [jnp.where(strict, gram[pr][0:R2, R2:2 * R2], 0.0) for pr in pairs]
    a_bk = [jnp.concatenate([jnp.where(incl, gram[pr][R2:2 * R2, 0:R2], 0.0),
                             jnp.where(incl, gram[pr][R2:2 * R2, R2:2 * R2], 0.0)],
                            axis=1).astype(BF16) for pr in pairs]
    t_inv = [eye - l_b[pr] for pr in pairs]
    pw = [t.astype(BF16) for t in l_b]
    e = 2
    while e < C:
        pw = [_bdot(pw[pr], pw[pr]).astype(BF16) for pr in pairs]
        t_inv = [t_inv[pr] + _bdot(t_inv[pr], pw[pr]) for pr in pairs]
        e *= 2
    lkv = [_bdot(l_k[pr], v_e[pr]) for pr in pairs]
    xs_kk, xs_r = [], []
    for pr in pairs:
        parts = [_bdot_nt(jnp.concatenate([seq_rows(kk_bf[pr], s), seq_rows(r_bf[pr], s)], axis=0),
                          sbd_ref[s, pr]) for s in seqs]
        xs_kk.append(from_seqs([q[0:2 * C] for q in parts]))
        xs_r.append(from_seqs([q[2 * C:4 * C] for q in parts]))
    u_e = [-_bdot(t_inv[pr], xs_kk[pr] + lkv[pr]).astype(BF16) for pr in pairs]
    uv = [jnp.concatenate([u_e[pr], v_e[pr]], axis=0) for pr in pairs]
    y_e = [xs_r[pr] + _bdot(a_bk[pr], uv[pr]) for pr in pairs]
    y = jnp.concatenate([y_e[pr][0:R] + y_e[pr][R:R2] for pr in pairs], axis=1)
    for pr in pairs:
        be_bf, ke_bf = e_be[pr].astype(BF16), e_ke[pr].astype(BF16)
        for s in seqs:
            upd = _bdot_tn(jnp.concatenate([seq_rows(u_e[pr], s), seq_rows(v_e[pr], s)], axis=0),
                           jnp.concatenate([seq_rows(be_bf, s), seq_rows(ke_bf, s)], axis=0))
            sbd_ref[s, pr] = sbd_ref[s, pr] * w_all[s * C:s * C + 1, sl[pr]] + upd

    inv_hd = 1.0 / A_HD
    mean = _seg_sum(y, ones_bd) * inv_hd
    yc = y - mean
    var = _seg_sum(yc * yc, ones_bd) * inv_hd
    yn = yc * lax.rsqrt(var + LNX_EPS) * lng_ref[...] + lnb_ref[...]
    yn = yn + bonus * v
    y_ref[...] = (yn * g).astype(y_ref.dtype)

    @pl.when(c_idx == pl.num_programs(1) - 1)
    def _():
        for s in seqs:
            for pr in pairs:
                s_bd = sbd_ref[s, pr]
                sout_ref[s, 2 * pr] = s_bd[0:A_HD, 0:A_HD]
                sout_ref[s, 2 * pr + 1] = s_bd[A_HD:2 * A_HD, A_HD:2 * A_HD]


def _rwkv(proj, shift0, s0, wts, batch, seq, chunk, nb):
    n_chunks = seq // chunk
    assert nb == 1 or n_chunks == 1
    rows = nb * chunk
    vec = lambda w: pl.BlockSpec(w.shape, lambda b, c: (0,) * w.ndim)
    st = pl.BlockSpec((nb, A_HEADS, A_HD, A_HD), lambda b, c: (b, 0, 0, 0))
    return pl.pallas_call(
        functools.partial(_rwkv_kernel, chunk=chunk, nb=nb, n_chunks=n_chunks),
        grid=(batch // nb, n_chunks),
        in_specs=[pl.BlockSpec((rows, RWKV_PAD), lambda b, c: (b * n_chunks + c, 0)),
                  pl.BlockSpec((nb, 1, RWKV_PAD), lambda b, c: (b, 0, 0)), st]
                 + [vec(w) for w in wts],
        out_specs=[pl.BlockSpec((rows, A_W), lambda b, c: (b * n_chunks + c, 0)), st],
        out_shape=[jax.ShapeDtypeStruct((batch * seq, A_W), BF16),
                   jax.ShapeDtypeStruct((batch, A_HEADS, A_HD, A_HD), F32)],
        scratch_shapes=[pltpu.VMEM((nb, A_HEADS // 2, LANES, LANES), F32),
                        pltpu.VMEM((nb, 1, RWKV_PAD), F32)],
        compiler_params=_cparams(("parallel", "arbitrary")),
        name="rwkv7",
    )(proj, shift0.reshape(batch, 1, RWKV_PAD), s0, *wts)


def _ret_kernel(q0_ref, q1_ref, k0_ref, k1_ref, v0_ref, v1_ref, g0_ref, g1_ref, cos_ref, sin_ref,
                din_ref, dq_ref, dk_ref, dc_ref, r0_ref, y_ref, rout_ref, r_scr, *, chunk, nb):
    C = chunk
    half = RET_BLK // R_HD
    c_idx = pl.program_id(1)

    @pl.when(c_idx == 0)
    def _():
        r_scr[...] = r0_ref[...]

    even = lax.broadcasted_iota(jnp.int32, (1, R_HD), 1) % 2 == 0
    cos = cos_ref[...]
    sin = sin_ref[...]

    def rot(t):
        partner = jnp.where(even, pltpu.roll(t, R_HD - 1, axis=1), pltpu.roll(t, 1, axis=1))
        return t * cos + partner * sin

    def load(halves, s, h):
        return halves[h // half][s * C:(s + 1) * C, (h % half) * R_HD:(h % half + 1) * R_HD]

    f32 = lambda *refs: tuple(r[...].astype(F32) for r in refs)
    items = [(s, h) for s in range(nb) for h in range(R_HEADS)]
    q_in, k_in, v_in, g_in = f32(q0_ref, q1_ref), f32(k0_ref, k1_ref), f32(v0_ref, v1_ref), f32(g0_ref, g1_ref)
    q = [rot(load(q_in, s, h)) for s, h in items]
    k = [rot(load(k_in, s, h)) * (R_HD ** -0.5) for s, h in items]
    v = [load(v_in, s, h).astype(BF16) for s, h in items]
    state = [r_scr[s, h] for s, h in items]
    scores = [_bdot_nt(q[i], k[i]) * din_ref[h] for i, (s, h) in enumerate(items)]
    o = [_bdot(jnp.concatenate([scores[i], q[i] * dq_ref[h]], axis=1),
               jnp.concatenate([v[i], state[i].astype(BF16)], axis=0))
         for i, (s, h) in enumerate(items)]
    for i, (s, h) in enumerate(items):
        r_scr[s, h] = state[i] * dc_ref[h, 0:1, :] + _bdot_tn(k[i] * dk_ref[h], v[i])
    for i, (s, h) in enumerate(items):
        on = o[i] * lax.rsqrt(jnp.mean(o[i] * o[i], axis=-1, keepdims=True) + NORM_EPS)
        gate = load(g_in, s, h)
        y_ref[s * C:(s + 1) * C, h * R_HD:(h + 1) * R_HD] = (
            gate * _sigmoid(gate) * on).astype(y_ref.dtype)

    @pl.when(c_idx == pl.num_programs(1) - 1)
    def _():
        rout_ref[...] = r_scr[...]


def _ret_tables(chunk, pos0, seq):
    pos = pos0 + jnp.arange(seq, dtype=F32)
    theta = 1.0 / (10000.0 ** jnp.linspace(0.0, 1.0, R_HD // 2, dtype=F32))
    ang = pos[:, None] * theta[None, :]
    cos2 = jnp.repeat(jnp.cos(ang), 2, axis=1)
    sin2 = jnp.stack([-jnp.sin(ang), jnp.sin(ang)], axis=-1).reshape(seq, R_HD)
    lg = jnp.log1p(-(2.0 ** (-5.0 - jnp.arange(R_HEADS, dtype=F32))))
    idx = jnp.arange(chunk, dtype=F32)
    diff = idx[:, None] - idx[None, :]
    causal = diff >= 0
    din = jnp.where(causal, jnp.exp(jnp.where(causal, diff, 0.0)[None] * lg[:, None, None]), 0.0)
    bc = lambda t: jnp.broadcast_to(t[:, :, None], (R_HEADS, t.shape[1], R_HD))
    dq = bc(jnp.exp((idx + 1.0)[None, :] * lg[:, None]))
    dk = bc(jnp.exp((chunk - 1.0 - idx)[None, :] * lg[:, None]))
    dc = jnp.broadcast_to(jnp.exp(chunk * lg)[:, None, None], (R_HEADS, 8, R_HD))
    return cos2, sin2, din, dq, dk, dc


def _retention(proj, r0, batch, seq, chunk, nb, pos0):
    n_chunks = seq // chunk
    cos2, sin2, din, dq, dk, dc = _ret_tables(chunk, pos0, seq)
    col0 = RWKV_PAD // RET_BLK
    rows = nb * chunk

    def sec(i):
        return pl.BlockSpec((rows, RET_BLK), lambda b, c: (b * n_chunks + c, col0 + i))

    tab = lambda t: pl.BlockSpec(t.shape, lambda b, c: (0, 0, 0))
    st = pl.BlockSpec((nb, R_HEADS, R_HD, R_HD), lambda b, c: (b, 0, 0, 0))
    return pl.pallas_call(
        functools.partial(_ret_kernel, chunk=chunk, nb=nb),
        grid=(batch // nb, n_chunks),
        in_specs=[sec(i) for i in range(8)]
                 + [pl.BlockSpec((chunk, R_HD), lambda b, c: (c, 0)),
                    pl.BlockSpec((chunk, R_HD), lambda b, c: (c, 0)),
                    tab(din), tab(dq), tab(dk), tab(dc), st],
        out_specs=[pl.BlockSpec((rows, R_W), lambda b, c: (b * n_chunks + c, 0)), st],
        out_shape=[jax.ShapeDtypeStruct((batch * seq, R_W), BF16),
                   jax.ShapeDtypeStruct((batch, R_HEADS, R_HD, R_HD), F32)],
        scratch_shapes=[pltpu.VMEM((nb, R_HEADS, R_HD, R_HD), F32)],
        compiler_params=_cparams(("parallel", "arbitrary")),
        name="retention",
    )(*([proj] * 8), cos2, sin2, din, dq, dk, dc, r0)


def _outproj_kernel(x_ref, ya_ref, yb_ref, wa_ref, wb_ref, g_ref, wr_ref,
                    x1_ref, h2_ref, ri_ref, rg_ref):
    x1 = (x_ref[...] + jnp.dot(ya_ref[...], wa_ref[...], preferred_element_type=F32)
          + jnp.dot(yb_ref[...], wb_ref[...], preferred_element_type=F32))
    x1_ref[...] = x1
    ms = jnp.mean(x1 * x1, axis=-1, keepdims=True)
    h2 = x1 * lax.rsqrt(ms + NORM_EPS) * g_ref[...]
    h2_ref[...] = _pack_bf16_pairs(h2)
    logits = _bdot(h2, wr_ref[...])
    lane = lax.broadcasted_iota(jnp.int32, (1, LANES), 1)
    neg = -1e30
    big = LANES
    rmax = lambda t: jnp.max(t, axis=-1, keepdims=True)
    rmin = lambda t: jnp.min(t, axis=-1, keepdims=True)
    is_c = lane < N_GROUPS
    lc = jnp.where(is_c, logits, neg)
    mc = rmax(lc)
    grp = rmin(jnp.where(lc == mc, lane, big))
    p_grp = 1.0 / jnp.sum(jnp.where(is_c, jnp.exp(lc - mc), 0.0), axis=-1, keepdims=True)
    fine = lane - N_GROUPS
    in_g = (fine >= 0) & (fine < N_EXPERTS) & ((fine // E_PER_GROUP) == grp)
    lf = jnp.where(in_g, logits, neg)
    m1 = rmax(lf)
    i1 = rmin(jnp.where(lf == m1, lane, big))
    lf2 = jnp.where(lane == i1, neg, lf)
    m2 = rmax(lf2)
    i2 = rmin(jnp.where(lf2 == m2, lane, big))
    e2 = jnp.exp(m2 - m1)
    g1 = p_grp / (1.0 + e2)
    g2 = p_grp * e2 / (1.0 + e2)
    ri_ref[...] = jnp.where(lane == 0, i1 - N_GROUPS, jnp.where(lane == 1, i2 - N_GROUPS, 0))
    rg_ref[...] = jnp.where(lane == 0, g1, jnp.where(lane == 1, g2, 0.0))


def _outproj(x, ya, yb, wa, wb, g, wr, tm=512):
    n, d = x.shape
    rowblk = lambda w: pl.BlockSpec((tm, w), lambda i: (i, 0))
    full = lambda w: pl.BlockSpec(w.shape, lambda i: (0, 0))
    return pl.pallas_call(
        _outproj_kernel,
        grid=(n // tm,),
        in_specs=[rowblk(d), rowblk(A_W), rowblk(R_W), full(wa), full(wb), full(g), full(wr)],
        out_specs=[rowblk(d), rowblk(d // 2), rowblk(LANES), rowblk(LANES)],
        out_shape=[jax.ShapeDtypeStruct((n, d), F32), jax.ShapeDtypeStruct((n, d // 2), jnp.uint32),
                   jax.ShapeDtypeStruct((n, LANES), jnp.int32),
                   jax.ShapeDtypeStruct((n, LANES), F32)],
        compiler_params=_cparams(("parallel",)),
        name="outproj_router",
    )(x, ya, yb, wa, wb, g, wr)


def _sort_kernel(ri_ref, tri_ref, dest_ref, info_ref, cnt_ref, base_ref):
    ph = pl.program_id(0)
    i = pl.program_id(1)
    lane = lax.broadcasted_iota(jnp.int32, (1, LANES), 1)
    ri = ri_ref[...]
    oh1 = (lane == ri[:, 0:1]).astype(F32)
    oh2 = (lane == ri[:, 1:2]).astype(F32)
    tot1 = jnp.sum(oh1, axis=0, keepdims=True)
    tot2 = jnp.sum(oh2, axis=0, keepdims=True)

    @pl.when(ph == 0)
    def _():
        @pl.when(i == 0)
        def _():
            cnt_ref[...] = jnp.zeros_like(cnt_ref)
        cnt_ref[...] += tot1 + tot2

    @pl.when(ph == 1)
    def _():
        @pl.when(i == 0)
        def _():
            nblk = jnp.floor((cnt_ref[...] + (MOE_BLK - 1)) * (1.0 / MOE_BLK))
            upper = (lax.broadcasted_iota(jnp.int32, (LANES, LANES), 0) <=
                     lax.broadcasted_iota(jnp.int32, (LANES, LANES), 1))
            pend = _bdot(jnp.broadcast_to(nblk, (8, LANES)), upper.astype(F32))[0:1]
            first = pend - nblk
            base_ref[...] = first * MOE_BLK
            srow = lax.broadcasted_iota(jnp.int32, (8, LANES), 0)
            info_ref[...] = jnp.where(srow == 0, nblk, jnp.where(srow == 1, first, 0.0)).astype(jnp.int32)

        base = base_ref[...]
        tri = tri_ref[...]
        pre1 = jnp.dot(tri, oh1.astype(BF16), preferred_element_type=F32)
        pre2 = jnp.dot(tri, oh2.astype(BF16), preferred_element_type=F32) + tot1
        d1 = jnp.sum(oh1 * (base + pre1), axis=1, keepdims=True)
        d2 = jnp.sum(oh2 * (base + pre2), axis=1, keepdims=True)
        dest_ref[...] = jnp.where(lane == 0, d1, jnp.where(lane == 1, d2, 0.0)).astype(jnp.int32)
        base_ref[...] = base + tot1 + tot2


def _sort(ri, tt=512):
    n = ri.shape[0]
    tri = (jnp.arange(tt)[:, None] > jnp.arange(tt)[None, :]).astype(BF16)
    return pl.pallas_call(
        _sort_kernel,
        grid=(2, n // tt),
        in_specs=[pl.BlockSpec((tt, LANES), lambda ph, i: (i, 0)),
                  pl.BlockSpec((tt, tt), lambda ph, i: (0, 0))],
        out_specs=[pl.BlockSpec((tt, LANES), lambda ph, i: (i * ph, 0)),
                   pl.BlockSpec((8, LANES), lambda ph, i: (0, 0))],
        out_shape=[jax.ShapeDtypeStruct((n, LANES), jnp.int32),
                   jax.ShapeDtypeStruct((8, LANES), jnp.int32)],
        scratch_shapes=[pltpu.VMEM((1, LANES), F32), pltpu.VMEM((1, LANES), F32)],
        compiler_params=_cparams(("arbitrary", "arbitrary")),
        name="moe_sort",
    )(ri, tri)


def _row_copy(src_ref, src_row, dst_ref, dst_row, sem):
    return pltpu.make_async_copy(src_ref.at[pl.ds(src_row, 1)], dst_ref.at[pl.ds(dst_row, 1)], sem)


def _dispatch_kernel(d1_ref, d2_ref, h_ref, xs_in_ref, xs_ref, sem, *, tt):
    del xs_in_ref

    def issue(t, carry):
        _row_copy(h_ref, t, xs_ref, d1_ref[t], sem).start()
        _row_copy(h_ref, t, xs_ref, d2_ref[t], sem).start()
        return carry

    def drain(t, carry):
        _row_copy(h_ref, t, xs_ref, d1_ref[t], sem).wait()
        _row_copy(h_ref, t, xs_ref, d2_ref[t], sem).wait()
        return carry

    lax.fori_loop(0, tt, issue, 0, unroll=8)
    lax.fori_loop(0, tt, drain, 0, unroll=8)


def _dispatch(d1, d2, h2, xs, tt=512):
    n, d = h2.shape
    smem = pl.BlockSpec((tt,), lambda i: (i,), memory_space=pltpu.SMEM)
    anyspec = pl.BlockSpec(memory_space=pl.ANY)
    return pl.pallas_call(
        functools.partial(_dispatch_kernel, tt=tt),
        grid=(n // tt,),
        in_specs=[smem, smem, pl.BlockSpec((tt, d), lambda i: (i, 0)), anyspec],
        out_specs=anyspec,
        out_shape=jax.ShapeDtypeStruct(xs.shape, xs.dtype),
        scratch_shapes=[pltpu.SemaphoreType.DMA(())],
        input_output_aliases={3: 0},
        compiler_params=_cparams(("arbitrary",)),
        name="moe_dispatch",
    )(d1, d2, h2, xs)


def _expert_kernel(nb_ref, fb_ref, xs_hbm, wg_ref, wu_ref, wd_ref, y_hbm,
                   wg_s, wu_s, wd_s, xbuf, ybuf, sem_in, sem_out, *, n_blk):
    e = pl.program_id(0)
    last = pl.num_programs(0) - 1
    n = nb_ref[e]
    b0 = fb_ref[e]
    n_used = fb_ref[last] + nb_ref[last]
    half = xbuf.shape[2]

    def load(blk, slot):
        return pltpu.make_async_copy(xs_hbm.at[pl.ds(blk * MOE_BLK, MOE_BLK)],
                                     xbuf.at[slot], sem_in.at[slot])

    def store(blk, slot):
        return pltpu.make_async_copy(ybuf.at[slot], y_hbm.at[pl.ds(blk * MOE_BLK, MOE_BLK)],
                                     sem_out.at[slot])

    @pl.when(n > 0)
    def _():
        @pl.when(b0 == 0)
        def _():
            load(0, 0).start()

        wg_s[...] = wg_ref[0].astype(BF16)
        wu_s[...] = wu_ref[0].astype(BF16)
        wd_s[...] = wd_ref[0].astype(BF16)

        def body(blk, carry):
            slot = blk % 2
            load(blk, slot).wait()

            @pl.when(blk + 1 < n_used)
            def _():
                load(blk + 1, 1 - slot).start()

            @pl.when(blk >= 2)
            def _():
                store(blk - 2, slot).wait()

            x_lo, x_hi = _unpack_bf16_pairs(xbuf[slot])
            x_lo, x_hi = x_lo.astype(BF16), x_hi.astype(BF16)
            mm = lambda w_s: (jnp.dot(x_lo, w_s[0:half], preferred_element_type=F32)
                              + jnp.dot(x_hi, w_s[half:2 * half], preferred_element_type=F32))
            hg = mm(wg_s)
            hu = mm(wu_s)
            act = (hg * _sigmoid(hg) * hu).astype(BF16)
            ybuf[slot] = _pack_bf16_pairs(jnp.dot(act, wd_s[...], preferred_element_type=F32))
            store(blk, slot).start()
            return carry

        lax.fori_loop(b0, b0 + n, body, 0)

    @pl.when(e == last)
    def _():
        @pl.when(n_used >= 2)
        def _():
            store(n_used - 2, n_used % 2).wait()

        @pl.when(n_used >= 1)
        def _():
            store(n_used - 1, (n_used - 1) % 2).wait()

        ybuf[0] = jnp.zeros(ybuf.shape[1:], ybuf.dtype)

        def fill(blk, carry):
            cp = store(blk, 0)
            cp.start()
            cp.wait()
            return carry

        lax.fori_loop(n_used, n_blk, fill, 0)


def _experts(n_blocks, first_block, xs, wg, wu, wd):
    p, half = xs.shape
    d = 2 * half
    wsel = lambda e, nb, fb: (e, 0, 0)
    anyspec = pl.BlockSpec(memory_space=pl.ANY)
    grid_spec = pltpu.PrefetchScalarGridSpec(
        num_scalar_prefetch=2,
        grid=(N_EXPERTS,),
        in_specs=[anyspec,
                  pl.BlockSpec((1, d, D_EXPERT), wsel),
                  pl.BlockSpec((1, d, D_EXPERT), wsel),
                  pl.BlockSpec((1, D_EXPERT, d), wsel)],
        out_specs=anyspec,
        scratch_shapes=[pltpu.VMEM((d, D_EXPERT), BF16), pltpu.VMEM((d, D_EXPERT), BF16),
                        pltpu.VMEM((D_EXPERT, d), BF16),
                        pltpu.VMEM((2, MOE_BLK, half), jnp.uint32),
                        pltpu.VMEM((2, MOE_BLK, half), jnp.uint32),
                        pltpu.SemaphoreType.DMA((2,)), pltpu.SemaphoreType.DMA((2,))])
    return pl.pallas_call(
        functools.partial(_expert_kernel, n_blk=p // MOE_BLK),
        grid_spec=grid_spec,
        out_shape=jax.ShapeDtypeStruct((p, half), jnp.uint32),
        compiler_params=_cparams(("arbitrary",)),
        name="moe_experts",
    )(n_blocks, first_block, xs, wg, wu, wd)


def _combine_kernel(d1_ref, d2_ref, d1n_ref, d2n_ref, x1_ref, rg_ref, g_ref, yb_ref, o_ref, buf, sem,
                    *, tt):
    i = pl.program_id(0)
    slot = i % 2

    def gather(a_ref, b_ref, s, wait):
        def body(t, carry):
            for j, idx in enumerate((a_ref, b_ref)):
                cp = _row_copy(yb_ref, idx[t], buf.at[s, j], t, sem.at[s])
                cp.wait() if wait else cp.start()
            return carry
        lax.fori_loop(0, tt, body, 0, unroll=8)

    @pl.when(i == 0)
    def _():
        gather(d1_ref, d2_ref, 0, False)

    @pl.when(i + 1 < pl.num_programs(0))
    def _():
        gather(d1n_ref, d2n_ref, 1 - slot, False)

    gather(d1_ref, d2_ref, slot, True)
    rg = rg_ref[...]
    lo1, hi1 = _unpack_bf16_pairs(buf[slot, 0])
    lo2, hi2 = _unpack_bf16_pairs(buf[slot, 1])
    g1, g2 = rg[:, 0:1], rg[:, 1:2]
    x = x1_ref[...] + jnp.concatenate([g1 * lo1 + g2 * lo2, g1 * hi1 + g2 * hi2], axis=1)
    ms = jnp.mean(x * x, axis=-1, keepdims=True)
    o_ref[...] = x * lax.rsqrt(ms + NORM_EPS) * g_ref[...]


def _combine(d1, d2, x1, rg, g, yb, tt=256):
    n, d = x1.shape
    steps = n // tt
    smem = pl.BlockSpec((tt,), lambda i: (i,), memory_space=pltpu.SMEM)
    smem_next = pl.BlockSpec((tt,), lambda i: (jnp.minimum(i + 1, steps - 1),),
                             memory_space=pltpu.SMEM)
    rowblk = lambda w: pl.BlockSpec((tt, w), lambda i: (i, 0))
    return pl.pallas_call(
        functools.partial(_combine_kernel, tt=tt),
        grid=(steps,),
        in_specs=[smem, smem, smem_next, smem_next, rowblk(d), rowblk(LANES),
                  pl.BlockSpec((1, d), lambda i: (0, 0)), pl.BlockSpec(memory_space=pl.ANY)],
        out_specs=rowblk(d),
        out_shape=jax.ShapeDtypeStruct((n, d), F32),
        scratch_shapes=[pltpu.VMEM((2, 2, tt, d // 2), jnp.uint32), pltpu.SemaphoreType.DMA((2,))],
        compiler_params=_cparams(("arbitrary",)),
        name="moe_combine",
    )(d1, d2, d1, d2, x1, rg, g, yb)


def _pad_cols(x, width):
    return jnp.pad(x, ((0, 0), (0, width - x.shape[1])))


def _rwkv_layout(x):
    return jnp.concatenate([
        x[:, :OFF_WD],
        _pad_cols(x[:, 3 * A_W:3 * A_W + W_LORA], LANES),
        _pad_cols(x[:, 3 * A_W + W_LORA:3 * A_W + W_LORA + A_LORA], LANES),
        _pad_cols(x[:, 3 * A_W + W_LORA + A_LORA:RWKV_PROJ], 2 * LANES)], axis=1)


def _rwkv_unlayout(x):
    return jnp.concatenate([x[:, :OFF_WD + W_LORA], x[:, OFF_AD:OFF_AD + A_LORA],
                            x[:, OFF_GD:OFF_GD + G_LORA]], axis=1)


def _pad_rows(w, rows):
    return jnp.pad(w, ((0, rows - w.shape[0]), (0, 0)))


def kernel(x_prompt, x_sample, state_rwkv_shift, state_rwkv_wkv, state_retention, norm_mix_g, w_in, mu_shift, decay_w0, decay_up, iclr_a0, iclr_up, gate_up, k_k, k_a, r_k, lnx_g, lnx_b, w_out, norm_ffn_g, router_coarse, router_fine, exp_w_gate, exp_w_up, exp_w_down, norm_final_g):
    bp, tp, d = x_prompt.shape
    bs, ts, _ = x_sample.shape
    row = lambda t: t.reshape(1, -1)
    l = 0

    w_in_p = jnp.concatenate([_rwkv_layout(w_in[l][:, :RWKV_PROJ]), w_in[l][:, RWKV_PROJ:]],
                             axis=1).astype(BF16)
    rwkv_wts = (_rwkv_layout(row(mu_shift[l])), row(decay_w0[l]),
                _pad_rows(decay_up[l], LANES).astype(BF16), row(iclr_a0[l]),
                _pad_rows(iclr_up[l], LANES).astype(BF16),
                _pad_rows(gate_up[l], 2 * LANES).astype(BF16),
                row(k_k[l]), row(k_a[l]), row(r_k[l]), row(lnx_g[l]), row(lnx_b[l]))
    w_out_a = w_out[l][:A_W].astype(BF16)
    w_out_b = w_out[l][A_W:].astype(BF16)
    w_router = _pad_cols(jnp.concatenate(
        [router_coarse[l], jnp.transpose(router_fine[l], (1, 0, 2)).reshape(d, N_EXPERTS)], axis=1),
        LANES).astype(BF16)

    groups = (
        (x_prompt.reshape(bp * tp, d), bp, tp, WKV_CHUNK, RET_CHUNK, 1, 0.0,
         jnp.zeros((bp, RWKV_PAD), F32), jnp.zeros((bp, A_HEADS, A_HD, A_HD), F32),
         jnp.zeros((bp, R_HEADS, R_HD, R_HD), F32)),
        (x_sample.reshape(bs * ts, d), bs, ts, ts, math.gcd(ts, RET_CHUNK), 8, float(PAST_LEN),
         _rwkv_layout(state_rwkv_shift[l]), state_rwkv_wkv[l], state_retention[l]),
    )
    x_last = jnp.concatenate([x_prompt[:, -1, :], x_sample[:, -1, :]], axis=0)
    n_last = -(-(bp + bs) // 8) * 8
    shifts = _rwkv_unlayout(_inproj(_pad_rows(x_last, n_last), row(norm_mix_g[l]), w_in_p, F32,
                                    RWKV_PAD, n_last, RWKV_PAD // 2))
    new_shifts = (shifts[:bp], shifts[bp:bp + bs])

    mixed = []
    for (x, b, t, wkv_chunk, ret_chunk, nb, pos0, shift0, wkv0, ret0), new_shift in zip(groups, new_shifts):
        proj = _inproj(x, row(norm_mix_g[l]), w_in_p, BF16, PROJ_PAD, 1024, 1536)
        y_a, new_wkv = _rwkv(proj, shift0, wkv0, rwkv_wts, b, t, wkv_chunk, nb)
        y_b, new_ret = _retention(proj, ret0, b, t, ret_chunk, nb, pos0)
        x1, h2, ri, rg = _outproj(x, y_a, y_b, w_out_a, w_out_b, row(norm_ffn_g[l]), w_router)
        mixed.append((x1, h2, ri, rg, new_shift, new_wkv, new_ret))

    n_tok = [m[0].shape[0] for m in mixed]
    n_all = sum(n_tok)
    n_blk = -(-2 * n_all // MOE_BLK) + N_EXPERTS
    dest, info = _sort(jnp.concatenate([m[2] for m in mixed], axis=0))
    n_blocks, first_block = info[0, :N_EXPERTS], info[1, :N_EXPERTS]
    xs = jnp.zeros((n_blk * MOE_BLK, d // 2), jnp.uint32)
    off = 0
    dests = []
    for m, n in zip(mixed, n_tok):
        d1, d2 = dest[off:off + n, 0], dest[off:off + n, 1]
        dests.append((d1, d2))
        xs = _dispatch(d1, d2, m[1], xs)
        off += n
    yb = _experts(n_blocks, first_block, xs, exp_w_gate[l], exp_w_up[l], exp_w_down[l])
    outs = [_combine(d1, d2, m[0], m[3], row(norm_final_g), yb)
            for m, (d1, d2) in zip(mixed, dests)]

    (_, _, _, _, p_shift, p_wkv, p_ret), (_, _, _, _, s_shift, s_wkv, s_ret) = mixed
    return (outs[0].reshape(bp, tp, d), outs[1].reshape(bs, ts, d),
            p_shift[None], p_wkv[None], p_ret[None], s_shift[None], s_wkv[None], s_ret[None])
```

```python
import functools
import math

import jax
import jax.numpy as jnp
from jax import lax
from jax.experimental import pallas as pl
from jax.experimental.pallas import tpu as pltpu

F32 = jnp.float32
BF16 = jnp.bfloat16

D_MODEL = 2048
A_W = 1024
A_HD = 64
A_HEADS = 16
W_LORA = 64
A_LORA = 64
G_LORA = 160
RWKV_PROJ = 3 * A_W + W_LORA + A_LORA + G_LORA
R_W = 1024
R_HEADS = 8
R_HD = 128
RET_CHUNK = 128
N_GROUPS = 4
E_PER_GROUP = 8
N_EXPERTS = 32
D_EXPERT = 512
NORM_EPS = 1e-6
LNX_EPS = 64e-5
DECAY_SCALE = math.exp(-0.5)
PAST_LEN = 16384

LANES = 128
OFF_WD = 3 * A_W
OFF_AD = OFF_WD + LANES
OFF_GD = OFF_AD + LANES
RWKV_PAD = OFF_GD + 2 * LANES
PROJ_PAD = RWKV_PAD + 4 * R_W
RET_BLK = 512
WKV_CHUNK = 64
WKV_SUB = 2
MOE_BLK = 256
W_AHEAD = 2
VMEM_LIMIT = 56 * 1024 * 1024


def _cparams(sem):
    return pltpu.CompilerParams(dimension_semantics=sem, vmem_limit_bytes=VMEM_LIMIT)


def _bdot(a, b):
    return jnp.dot(a.astype(BF16), b.astype(BF16), preferred_element_type=F32)


def _bdot_nt(a, b):
    return lax.dot_general(a.astype(BF16), b.astype(BF16), (((1,), (1,)), ((), ())),
                           preferred_element_type=F32)


def _bdot_tn(a, b):
    return lax.dot_general(a.astype(BF16), b.astype(BF16), (((0,), (0,)), ((), ())),
                           preferred_element_type=F32)


def _split3(x):
    hi = x.astype(BF16)
    r1 = x - hi.astype(F32)
    mid = r1.astype(BF16)
    lo = (r1 - mid.astype(F32)).astype(BF16)
    return hi, mid, lo


def _dot3(x, w_bf16):
    hi, mid, lo = _split3(x)
    d = lambda p: jnp.dot(p, w_bf16, preferred_element_type=F32)
    return d(hi) + d(mid) + d(lo)


def _sigmoid(x):
    return 1.0 / (1.0 + jnp.exp(-x))


def _pack_bf16_pairs(x):
    w = x.shape[1] // 2
    lo = lax.bitcast_convert_type(x[:, :w].astype(BF16).astype(F32), jnp.uint32)
    hi = lax.bitcast_convert_type(x[:, w:].astype(BF16).astype(F32), jnp.uint32)
    return hi | (lo >> 16)


def _unpack_bf16_pairs(p):
    lo = lax.bitcast_convert_type(p << 16, F32)
    hi = lax.bitcast_convert_type(p & jnp.uint32(0xFFFF0000), F32)
    return lo, hi


def _inproj_kernel(x_ref, g_ref, w_ref, o_ref, h_ref):
    @pl.when(pl.program_id(1) == 0)
    def _():
        x = x_ref[...]
        ms = jnp.mean(x * x, axis=-1, keepdims=True)
        h_ref[...] = (x * lax.rsqrt(ms + NORM_EPS) * g_ref[...]).astype(BF16)

    o_ref[...] = jnp.dot(h_ref[...], w_ref[...], preferred_element_type=F32).astype(o_ref.dtype)


def _inproj(x, g, w_bf16, out_dtype, n_cols, tm, tn):
    n, d = x.shape
    return pl.pallas_call(
        _inproj_kernel,
        grid=(n // tm, n_cols // tn),
        in_specs=[pl.BlockSpec((tm, d), lambda i, j: (i, 0)),
                  pl.BlockSpec((1, d), lambda i, j: (0, 0)),
                  pl.BlockSpec((d, tn), lambda i, j: (0, j))],
        out_specs=pl.BlockSpec((tm, tn), lambda i, j: (i, j)),
        out_shape=jax.ShapeDtypeStruct((n, n_cols), out_dtype),
        scratch_shapes=[pltpu.VMEM((tm, d), BF16)],
        compiler_params=_cparams(("parallel", "arbitrary")),
        name="inproj",
    )(x, g, w_bf16)


def _seg_sum(x, ones_bd):
    nslab = x.shape[1] // LANES
    stacked = jnp.concatenate([x[:, s * LANES:(s + 1) * LANES] for s in range(nslab)], axis=0)
    red = _dot3(stacked, ones_bd)
    c = x.shape[0]
    return jnp.concatenate([red[s * c:(s + 1) * c] for s in range(nslab)], axis=1)


def _rwkv_chunk(p, before, sbd_ref, mu_ref, w0_ref, wup_ref, a0_ref, aup_ref, gup_ref,
                kk_ref, ka_ref, rk_ref, lng_ref, lnb_ref, *, chunk, nb):
    C = chunk
    R = nb * C
    pairs = range(A_HEADS // 2)
    seqs = range(nb)
    lane = lax.broadcasted_iota(jnp.int32, (1, LANES), 1)
    in_a = lane < A_HD

    row = lax.broadcasted_iota(jnp.int32, (R, 1), 0)
    pos = row % C
    first = jnp.broadcast_to(before, (nb, C, RWKV_PAD)).reshape(R, RWKV_PAD)
    prev = jnp.where(pos == 0, first, pltpu.roll(p, 1, axis=0))
    xm = p + (prev - p) * mu_ref[...]

    r = xm[:, 0:A_W]
    k = xm[:, A_W:2 * A_W]
    v = xm[:, 2 * A_W:3 * A_W]
    wd = xm[:, OFF_WD:OFF_AD]
    ad = xm[:, OFF_AD:OFF_GD]
    gd = xm[:, OFF_GD:RWKV_PAD]

    logw = -DECAY_SCALE * _sigmoid(w0_ref[...] + _bdot(jnp.tanh(wd), wup_ref[...]))
    a = _sigmoid(a0_ref[...] + _bdot(ad, aup_ref[...]))
    g = _bdot(_sigmoid(gd), gup_ref[...])

    ones_bd = (lax.broadcasted_iota(jnp.int32, (LANES, LANES), 0) // A_HD ==
               lax.broadcasted_iota(jnp.int32, (LANES, LANES), 1) // A_HD).astype(BF16)

    kk = k * kk_ref[...]
    k2 = k * (1.0 + (a - 1.0) * ka_ref[...])
    sums = _seg_sum(jnp.concatenate([kk * kk, r * k2 * rk_ref[...]], axis=0), ones_bd)
    kk = kk / jnp.maximum(jnp.sqrt(sums[0:R]), 1e-12)
    bonus = sums[R:2 * R]
    b = kk * a

    cum = logw
    suf = logw
    d = 1
    while d < C:
        cum = cum + jnp.where(pos >= d, pltpu.roll(cum, d, axis=0), 0.0)
        suf = suf + jnp.where(pos < C - d, pltpu.roll(suf, R - d, axis=0), 0.0)
        d *= 2
    after = suf - logw
    w_incl = jnp.exp(cum)
    w_excl = jnp.exp(cum - logw)
    w_inv = jnp.exp(-cum)
    w_end = jnp.exp(after)
    w_all = jnp.exp(cum + after)

    kk_d = kk * w_excl
    r_d = r * w_incl
    b_h = b * w_inv
    k_h = k2 * w_inv
    b_e = b * w_end
    k_e = k2 * w_end

    R2 = 2 * R
    ri = lax.broadcasted_iota(jnp.int32, (R2, R2), 0)
    ci = lax.broadcasted_iota(jnp.int32, (R2, R2), 1)
    same = (ri // C) == (ci // C)
    strict = same & (ci < ri)
    incl = same & (ci <= ri)
    eye = (ri == ci).astype(F32)

    def expand(x):
        return jnp.concatenate([jnp.where(in_a, x, 0.0), jnp.where(in_a, 0.0, x)], axis=0)

    def seq_rows(x, s):
        if nb == 1:
            return x
        return jnp.concatenate([x[s * C:(s + 1) * C], x[R + s * C:R + (s + 1) * C]], axis=0)

    def from_seqs(parts):
        if nb == 1:
            return parts[0]
        return jnp.concatenate([q[0:C] for q in parts] + [q[C:2 * C] for q in parts], axis=0)

    sl = [slice(pr * LANES, (pr + 1) * LANES) for pr in pairs]
    ex = lambda x: [expand(x[:, sl[pr]]) for pr in pairs]
    e_kk, e_r, e_bh, e_kh, e_be, e_ke = ex(kk_d), ex(r_d), ex(b_h), ex(k_h), ex(b_e), ex(k_e)
    v_e = [t.astype(BF16) for t in ex(v)]
    kk_bf = [t.astype(BF16) for t in e_kk]
    r_bf = [t.astype(BF16) for t in e_r]
    gram = [_bdot_nt(jnp.concatenate([kk_bf[pr], r_bf[pr]], axis=0),
                     jnp.concatenate([e_bh[pr], e_kh[pr]], axis=0)) for pr in pairs]
    l_b = [jnp.where(strict, gram[pr][0:R2, 0:R2], 0.0) for pr in pairs]
    l_k = [jnp.where(strict, gram[pr][0:R2, R2:2 * R2], 0.0) for pr in pairs]
    a_bk = [jnp.concatenate([jnp.where(incl, gram[pr][R2:2 * R2, 0:R2], 0.0),
                             jnp.where(incl, gram[pr][R2:2 * R2, R2:2 * R2], 0.0)],
                            axis=1).astype(BF16) for pr in pairs]
    t_inv = [eye - l_b[pr] for pr in pairs]
    pw = [t.astype(BF16) for t in l_b]
    e = 2
    while e < C:
        pw = [_bdot(pw[pr], pw[pr]).astype(BF16) for pr in pairs]
        t_inv = [t_inv[pr] + _bdot(t_inv[pr], pw[pr]) for pr in pairs]
        e *= 2
    lkv = [_bdot(l_k[pr], v_e[pr]) for pr in pairs]
    xs_kk, xs_r = [], []
    for pr in pairs:
        parts = [_bdot_nt(jnp.concatenate([seq_rows(kk_bf[pr], s), seq_rows(r_bf[pr], s)], axis=0),
                          sbd_ref[s, pr]) for s in seqs]
        xs_kk.append(from_seqs([q[0:2 * C] for q in parts]))
        xs_r.append(from_seqs([q[2 * C:4 * C] for q in parts]))
    u_e = [-_bdot(t_inv[pr], xs_kk[pr] + lkv[pr]).astype(BF16) for pr in pairs]
    uv = [jnp.concatenate([u_e[pr], v_e[pr]], axis=0) for pr in pairs]
    y_e = [xs_r[pr] + _bdot(a_bk[pr], uv[pr]) for pr in pairs]
    y = jnp.concatenate([y_e[pr][0:R] + y_e[pr][R:R2] for pr in pairs], axis=1)
    for pr in pairs:
        be_bf, ke_bf = e_be[pr].astype(BF16), e_ke[pr].astype(BF16)
        for s in seqs:
            upd = _bdot_tn(jnp.concatenate([seq_rows(u_e[pr], s), seq_rows(v_e[pr], s)], axis=0),
                           jnp.concatenate([seq_rows(be_bf, s), seq_rows(ke_bf, s)], axis=0))
            sbd_ref[s, pr] = sbd_ref[s, pr] * w_all[s * C:s * C + 1, sl[pr]] + upd

    inv_hd = 1.0 / A_HD
    mean = _seg_sum(y, ones_bd) * inv_hd
    yc = y - mean
    var = _seg_sum(yc * yc, ones_bd) * inv_hd
    yn = yc * lax.rsqrt(var + LNX_EPS) * lng_ref[...] + lnb_ref[...]
    yn = yn + bonus * v
    return yn * g


def _rwkv_kernel(p_ref, shift_ref, s0_ref, *rest, chunk, nb, sub):
    wrefs, (y_ref, sout_ref, sbd_ref, carry_ref) = rest[:-4], rest[-4:]
    c_idx = pl.program_id(1)
    pairs = range(A_HEADS // 2)
    rows = nb * chunk

    @pl.when(c_idx == 0)
    def _():
        carry_ref[...] = shift_ref[...]
        z = jnp.zeros((A_HD, A_HD), F32)
        for s in range(nb):
            for pr in pairs:
                sa = s0_ref[s, 2 * pr]
                sb = s0_ref[s, 2 * pr + 1]
                sbd_ref[s, pr] = jnp.concatenate(
                    [jnp.concatenate([sa, z], axis=1), jnp.concatenate([z, sb], axis=1)], axis=0)

    p_all = p_ref[...].astype(F32)
    before = carry_ref[...]
    for i in range(sub):
        p = p_all[i * rows:(i + 1) * rows]
        y = _rwkv_chunk(p, before, sbd_ref, *wrefs, chunk=chunk, nb=nb)
        y_ref[i * rows:(i + 1) * rows, :] = y.astype(y_ref.dtype)
        before = p[rows - 1:rows, :].reshape(1, 1, RWKV_PAD)
    if nb == 1:
        carry_ref[...] = before

    @pl.when(c_idx == pl.num_programs(1) - 1)
    def _():
        for s in range(nb):
            for pr in pairs:
                s_bd = sbd_ref[s, pr]
                sout_ref[s, 2 * pr] = s_bd[0:A_HD, 0:A_HD]
                sout_ref[s, 2 * pr + 1] = s_bd[A_HD:2 * A_HD, A_HD:2 * A_HD]


def _rwkv(proj, shift0, s0, wts, batch, seq, chunk, nb, sub):
    n_chunks = seq // (chunk * sub)
    assert nb == 1 or (n_chunks == 1 and sub == 1)
    rows = nb * chunk * sub
    vec = lambda w: pl.BlockSpec(w.shape, lambda b, c: (0,) * w.ndim)
    st = pl.BlockSpec((nb, A_HEADS, A_HD, A_HD), lambda b, c: (b, 0, 0, 0))
    return pl.pallas_call(
        functools.partial(_rwkv_kernel, chunk=chunk, nb=nb, sub=sub),
        grid=(batch // nb, n_chunks),
        in_specs=[pl.BlockSpec((rows, RWKV_PAD), lambda b, c: (b * n_chunks + c, 0)),
                  pl.BlockSpec((nb, 1, RWKV_PAD), lambda b, c: (b, 0, 0)), st]
                 + [vec(w) for w in wts],
        out_specs=[pl.BlockSpec((rows, A_W), lambda b, c: (b * n_chunks + c, 0)), st],
        out_shape=[jax.ShapeDtypeStruct((batch * seq, A_W), BF16),
                   jax.ShapeDtypeStruct((batch, A_HEADS, A_HD, A_HD), F32)],
        scratch_shapes=[pltpu.VMEM((nb, A_HEADS // 2, LANES, LANES), F32),
                        pltpu.VMEM((nb, 1, RWKV_PAD), F32)],
        compiler_params=_cparams(("parallel", "arbitrary")),
        name="rwkv7",
    )(proj, shift0.reshape(batch, 1, RWKV_PAD), s0, *wts)


def _ret_kernel(q0_ref, q1_ref, k0_ref, k1_ref, v0_ref, v1_ref, g0_ref, g1_ref, cos_ref, sin_ref,
                din_ref, dq_ref, dk_ref, dc_ref, r0_ref, y_ref, rout_ref, r_scr, *, chunk, nb):
    C = chunk
    half = RET_BLK // R_HD
    c_idx = pl.program_id(1)

    @pl.when(c_idx == 0)
    def _():
        r_scr[...] = r0_ref[...]

    even = lax.broadcasted_iota(jnp.int32, (1, R_HD), 1) % 2 == 0
    cos = cos_ref[...]
    sin = sin_ref[...]

    def rot(t):
        partner = jnp.where(even, pltpu.roll(t, R_HD - 1, axis=1), pltpu.roll(t, 1, axis=1))
        return t * cos + partner * sin

    def load(halves, s, h):
        return halves[h // half][s * C:(s + 1) * C, (h % half) * R_HD:(h % half + 1) * R_HD]

    f32 = lambda *refs: tuple(r[...].astype(F32) for r in refs)
    items = [(s, h) for s in range(nb) for h in range(R_HEADS)]
    q_in, k_in, v_in, g_in = f32(q0_ref, q1_ref), f32(k0_ref, k1_ref), f32(v0_ref, v1_ref), f32(g0_ref, g1_ref)
    q = [rot(load(q_in, s, h)) for s, h in items]
    k = [rot(load(k_in, s, h)) * (R_HD ** -0.5) for s, h in items]
    v = [load(v_in, s, h).astype(BF16) for s, h in items]
    state = [r_scr[s, h] for s, h in items]
    scores = [_bdot_nt(q[i], k[i]) * din_ref[h] for i, (s, h) in enumerate(items)]
    o = [_bdot(jnp.concatenate([scores[i], q[i] * dq_ref[h]], axis=1),
               jnp.concatenate([v[i], state[i].astype(BF16)], axis=0))
         for i, (s, h) in enumerate(items)]
    for i, (s, h) in enumerate(items):
        r_scr[s, h] = state[i] * dc_ref[h, 0:1, :] + _bdot_tn(k[i] * dk_ref[h], v[i])
    for i, (s, h) in enumerate(items):
        on = o[i] * lax.rsqrt(jnp.mean(o[i] * o[i], axis=-1, keepdims=True) + NORM_EPS)
        gate = load(g_in, s, h)
        y_ref[s * C:(s + 1) * C, h * R_HD:(h + 1) * R_HD] = (
            gate * _sigmoid(gate) * on).astype(y_ref.dtype)

    @pl.when(c_idx == pl.num_programs(1) - 1)
    def _():
        rout_ref[...] = r_scr[...]


def _ret_tables(chunk, pos0, seq):
    pos = pos0 + jnp.arange(seq, dtype=F32)
    theta = 1.0 / (10000.0 ** jnp.linspace(0.0, 1.0, R_HD // 2, dtype=F32))
    ang = pos[:, None] * theta[None, :]
    cos2 = jnp.repeat(jnp.cos(ang), 2, axis=1)
    sin2 = jnp.stack([-jnp.sin(ang), jnp.sin(ang)], axis=-1).reshape(seq, R_HD)
    lg = jnp.log1p(-(2.0 ** (-5.0 - jnp.arange(R_HEADS, dtype=F32))))
    idx = jnp.arange(chunk, dtype=F32)
    diff = idx[:, None] - idx[None, :]
    causal = diff >= 0
    din = jnp.where(causal, jnp.exp(jnp.where(causal, diff, 0.0)[None] * lg[:, None, None]), 0.0)
    bc = lambda t: jnp.broadcast_to(t[:, :, None], (R_HEADS, t.shape[1], R_HD))
    dq = bc(jnp.exp((idx + 1.0)[None, :] * lg[:, None]))
    dk = bc(jnp.exp((chunk - 1.0 - idx)[None, :] * lg[:, None]))
    dc = jnp.broadcast_to(jnp.exp(chunk * lg)[:, None, None], (R_HEADS, 8, R_HD))
    return cos2, sin2, din, dq, dk, dc


def _retention(proj, r0, batch, seq, chunk, nb, pos0):
    n_chunks = seq // chunk
    cos2, sin2, din, dq, dk, dc = _ret_tables(chunk, pos0, seq)
    col0 = RWKV_PAD // RET_BLK
    rows = nb * chunk

    def sec(i):
        return pl.BlockSpec((rows, RET_BLK), lambda b, c: (b * n_chunks + c, col0 + i))

    tab = lambda t: pl.BlockSpec(t.shape, lambda b, c: (0, 0, 0))
    st = pl.BlockSpec((nb, R_HEADS, R_HD, R_HD), lambda b, c: (b, 0, 0, 0))
    return pl.pallas_call(
        functools.partial(_ret_kernel, chunk=chunk, nb=nb),
        grid=(batch // nb, n_chunks),
        in_specs=[sec(i) for i in range(8)]
                 + [pl.BlockSpec((chunk, R_HD), lambda b, c: (c, 0)),
                    pl.BlockSpec((chunk, R_HD), lambda b, c: (c, 0)),
                    tab(din), tab(dq), tab(dk), tab(dc), st],
        out_specs=[pl.BlockSpec((rows, R_W), lambda b, c: (b * n_chunks + c, 0)), st],
        out_shape=[jax.ShapeDtypeStruct((batch * seq, R_W), BF16),
                   jax.ShapeDtypeStruct((batch, R_HEADS, R_HD, R_HD), F32)],
        scratch_shapes=[pltpu.VMEM((nb, R_HEADS, R_HD, R_HD), F32)],
        compiler_params=_cparams(("parallel", "arbitrary")),
        name="retention",
    )(*([proj] * 8), cos2, sin2, din, dq, dk, dc, r0)


def _outproj_kernel(x_ref, ya_ref, yb_ref, wa_ref, wb_ref, g_ref, wr_ref,
                    x1_ref, h2_ref, ri_ref, rg_ref):
    x1 = (x_ref[...] + jnp.dot(ya_ref[...], wa_ref[...], preferred_element_type=F32)
          + jnp.dot(yb_ref[...], wb_ref[...], preferred_element_type=F32))
    x1_ref[...] = x1
    ms = jnp.mean(x1 * x1, axis=-1, keepdims=True)
    h2 = x1 * lax.rsqrt(ms + NORM_EPS) * g_ref[...]
    h2_ref[...] = _pack_bf16_pairs(h2)
    logits = _bdot(h2, wr_ref[...])
    lane = lax.broadcasted_iota(jnp.int32, (1, LANES), 1)
    neg = -1e30
    big = LANES
    rmax = lambda t: jnp.max(t, axis=-1, keepdims=True)
    rmin = lambda t: jnp.min(t, axis=-1, keepdims=True)
    is_c = lane < N_GROUPS
    lc = jnp.where(is_c, logits, neg)
    mc = rmax(lc)
    grp = rmin(jnp.where(lc == mc, lane, big))
    p_grp = 1.0 / jnp.sum(jnp.where(is_c, jnp.exp(lc - mc), 0.0), axis=-1, keepdims=True)
    fine = lane - N_GROUPS
    in_g = (fine >= 0) & (fine < N_EXPERTS) & ((fine // E_PER_GROUP) == grp)
    lf = jnp.where(in_g, logits, neg)
    m1 = rmax(lf)
    i1 = rmin(jnp.where(lf == m1, lane, big))
    lf2 = jnp.where(lane == i1, neg, lf)
    m2 = rmax(lf2)
    i2 = rmin(jnp.where(lf2 == m2, lane, big))
    e2 = jnp.exp(m2 - m1)
    g1 = p_grp / (1.0 + e2)
    g2 = p_grp * e2 / (1.0 + e2)
    ri_ref[...] = jnp.where(lane == 0, i1 - N_GROUPS, jnp.where(lane == 1, i2 - N_GROUPS, 0))
    rg_ref[...] = jnp.where(lane == 0, g1, jnp.where(lane == 1, g2, 0.0))


def _outproj(x, ya, yb, wa, wb, g, wr, tm=512):
    n, d = x.shape
    rowblk = lambda w: pl.BlockSpec((tm, w), lambda i: (i, 0))
    full = lambda w: pl.BlockSpec(w.shape, lambda i: (0, 0))
    return pl.pallas_call(
        _outproj_kernel,
        grid=(n // tm,),
        in_specs=[rowblk(d), rowblk(A_W), rowblk(R_W), full(wa), full(wb), full(g), full(wr)],
        out_specs=[rowblk(d), rowblk(d // 2), rowblk(LANES), rowblk(LANES)],
        out_shape=[jax.ShapeDtypeStruct((n, d), F32), jax.ShapeDtypeStruct((n, d // 2), jnp.uint32),
                   jax.ShapeDtypeStruct((n, LANES), jnp.int32),
                   jax.ShapeDtypeStruct((n, LANES), F32)],
        compiler_params=_cparams(("parallel",)),
        name="outproj_router",
    )(x, ya, yb, wa, wb, g, wr)


def _sort_kernel(ri_ref, tri_ref, dest_ref, info_ref, cnt_ref, base_ref):
    ph = pl.program_id(0)
    i = pl.program_id(1)
    lane = lax.broadcasted_iota(jnp.int32, (1, LANES), 1)
    ri = ri_ref[...]
    oh1 = (lane == ri[:, 0:1]).astype(F32)
    oh2 = (lane == ri[:, 1:2]).astype(F32)
    tot1 = jnp.sum(oh1, axis=0, keepdims=True)
    tot2 = jnp.sum(oh2, axis=0, keepdims=True)

    @pl.when(ph == 0)
    def _():
        @pl.when(i == 0)
        def _():
            cnt_ref[...] = jnp.zeros_like(cnt_ref)
        cnt_ref[...] += tot1 + tot2

    @pl.when(ph == 1)
    def _():
        @pl.when(i == 0)
        def _():
            nblk = jnp.floor((cnt_ref[...] + (MOE_BLK - 1)) * (1.0 / MOE_BLK))
            upper = (lax.broadcasted_iota(jnp.int32, (LANES, LANES), 0) <=
                     lax.broadcasted_iota(jnp.int32, (LANES, LANES), 1))
            pend = _bdot(jnp.broadcast_to(nblk, (8, LANES)), upper.astype(F32))[0:1]
            first = pend - nblk
            base_ref[...] = first * MOE_BLK
            srow = lax.broadcasted_iota(jnp.int32, (8, LANES), 0)
            info_ref[...] = jnp.where(srow == 0, nblk, jnp.where(srow == 1, first, 0.0)).astype(jnp.int32)

        base = base_ref[...]
        tri = tri_ref[...]
        pre1 = jnp.dot(tri, oh1.astype(BF16), preferred_element_type=F32)
        pre2 = jnp.dot(tri, oh2.astype(BF16), preferred_element_type=F32) + tot1
        d1 = jnp.sum(oh1 * (base + pre1), axis=1, keepdims=True)
        d2 = jnp.sum(oh2 * (base + pre2), axis=1, keepdims=True)
        dest_ref[...] = jnp.where(lane == 0, d1, jnp.where(lane == 1, d2, 0.0)).astype(jnp.int32)
        base_ref[...] = base + tot1 + tot2


def _sort(ri, tt=512):
    n = ri.shape[0]
    tri = (jnp.arange(tt)[:, None] > jnp.arange(tt)[None, :]).astype(BF16)
    return pl.pallas_call(
        _sort_kernel,
        grid=(2, n // tt),
        in_specs=[pl.BlockSpec((tt, LANES), lambda ph, i: (i, 0)),
                  pl.BlockSpec((tt, tt), lambda ph, i: (0, 0))],
        out_specs=[pl.BlockSpec((tt, LANES), lambda ph, i: (i * ph, 0)),
                   pl.BlockSpec((8, LANES), lambda ph, i: (0, 0))],
        out_shape=[jax.ShapeDtypeStruct((n, LANES), jnp.int32),
                   jax.ShapeDtypeStruct((8, LANES), jnp.int32)],
        scratch_shapes=[pltpu.VMEM((1, LANES), F32), pltpu.VMEM((1, LANES), F32)],
        compiler_params=_cparams(("arbitrary", "arbitrary")),
        name="moe_sort",
    )(ri, tri)


def _row_copy(src_ref, src_row, dst_ref, dst_row, sem):
    return pltpu.make_async_copy(src_ref.at[pl.ds(src_row, 1)], dst_ref.at[pl.ds(dst_row, 1)], sem)


def _dispatch_kernel(d1_ref, d2_ref, h_ref, xs_in_ref, xs_ref, sem, *, tt):
    del xs_in_ref

    def issue(t, carry):
        _row_copy(h_ref, t, xs_ref, d1_ref[t], sem).start()
        _row_copy(h_ref, t, xs_ref, d2_ref[t], sem).start()
        return carry

    def drain(t, carry):
        _row_copy(h_ref, t, xs_ref, d1_ref[t], sem).wait()
        _row_copy(h_ref, t, xs_ref, d2_ref[t], sem).wait()
        return carry

    lax.fori_loop(0, tt, issue, 0, unroll=8)
    lax.fori_loop(0, tt, drain, 0, unroll=8)


def _dispatch(d1, d2, h2, xs, tt=512):
    n, d = h2.shape
    smem = pl.BlockSpec((tt,), lambda i: (i,), memory_space=pltpu.SMEM)
    anyspec = pl.BlockSpec(memory_space=pl.ANY)
    return pl.pallas_call(
        functools.partial(_dispatch_kernel, tt=tt),
        grid=(n // tt,),
        in_specs=[smem, smem, pl.BlockSpec((tt, d), lambda i: (i, 0)), anyspec],
        out_specs=anyspec,
        out_shape=jax.ShapeDtypeStruct(xs.shape, xs.dtype),
        scratch_shapes=[pltpu.SemaphoreType.DMA(())],
        input_output_aliases={3: 0},
        compiler_params=_cparams(("arbitrary",)),
        name="moe_dispatch",
    )(d1, d2, h2, xs)


def _expert_kernel(nb_ref, fb_ref, xs_hbm, wg_hbm, wu_hbm, wd_hbm, y_hbm,
                   wg_r, wu_r, wd_r, wg_s, wu_s, wd_s, xbuf, ybuf, sem_w, sem_in, sem_out, *, n_blk):
    e = pl.program_id(0)
    last = pl.num_programs(0) - 1
    n = nb_ref[e]
    b0 = fb_ref[e]
    n_used = fb_ref[last] + nb_ref[last]
    half = xbuf.shape[2]
    ring = wg_r.shape[0]

    def fetch(ex):
        slot = ex % ring
        return [pltpu.make_async_copy(src.at[ex], dst.at[slot], sem_w.at[slot])
                for src, dst in ((wg_hbm, wg_r), (wu_hbm, wu_r), (wd_hbm, wd_r))]

    @pl.when(e == 0)
    def _():
        for ex in range(ring - 1):
            for cp in fetch(ex):
                cp.start()

    @pl.when(e + ring - 1 <= last)
    def _():
        for cp in fetch(e + ring - 1):
            cp.start()

    for cp in fetch(e):
        cp.wait()
    w_slot = e % ring

    def load(blk, slot):
        return pltpu.make_async_copy(xs_hbm.at[pl.ds(blk * MOE_BLK, MOE_BLK)],
                                     xbuf.at[slot], sem_in.at[slot])

    def store(blk, slot):
        return pltpu.make_async_copy(ybuf.at[slot], y_hbm.at[pl.ds(blk * MOE_BLK, MOE_BLK)],
                                     sem_out.at[slot])

    @pl.when(n > 0)
    def _():
        @pl.when(b0 == 0)
        def _():
            load(0, 0).start()

        wg_s[...] = wg_r[w_slot].astype(BF16)
        wu_s[...] = wu_r[w_slot].astype(BF16)
        wd_s[...] = wd_r[w_slot].astype(BF16)

        def body(blk, carry):
            slot = blk % 2
            load(blk, slot).wait()

            @pl.when(blk + 1 < n_used)
            def _():
                load(blk + 1, 1 - slot).start()

            @pl.when(blk >= 2)
            def _():
                store(blk - 2, slot).wait()

            x_lo, x_hi = _unpack_bf16_pairs(xbuf[slot])
            x_lo, x_hi = x_lo.astype(BF16), x_hi.astype(BF16)
            mm = lambda w_s: (jnp.dot(x_lo, w_s[0:half], preferred_element_type=F32)
                              + jnp.dot(x_hi, w_s[half:2 * half], preferred_element_type=F32))
            hg = mm(wg_s)
            hu = mm(wu_s)
            act = (hg * _sigmoid(hg) * hu).astype(BF16)
            ybuf[slot] = _pack_bf16_pairs(jnp.dot(act, wd_s[...], preferred_element_type=F32))
            store(blk, slot).start()
            return carry

        lax.fori_loop(b0, b0 + n, body, 0)

    @pl.when(e == last)
    def _():
        @pl.when(n_used >= 2)
        def _():
            store(n_used - 2, n_used % 2).wait()

        @pl.when(n_used >= 1)
        def _():
            store(n_used - 1, (n_used - 1) % 2).wait()

        ybuf[0] = jnp.zeros(ybuf.shape[1:], ybuf.dtype)

        def fill(blk, carry):
            cp = store(blk, 0)
            cp.start()
            cp.wait()
            return carry

        lax.fori_loop(n_used, n_blk, fill, 0)


def _experts(n_blocks, first_block, xs, wg, wu, wd):
    p, half = xs.shape
    d = 2 * half
    anyspec = pl.BlockSpec(memory_space=pl.ANY)
    ring = W_AHEAD + 1
    grid_spec = pltpu.PrefetchScalarGridSpec(
        num_scalar_prefetch=2,
        grid=(N_EXPERTS,),
        in_specs=[anyspec] * 4,
        out_specs=anyspec,
        scratch_shapes=[pltpu.VMEM((ring, d, D_EXPERT), F32), pltpu.VMEM((ring, d, D_EXPERT), F32),
                        pltpu.VMEM((ring, D_EXPERT, d), F32),
                        pltpu.VMEM((d, D_EXPERT), BF16), pltpu.VMEM((d, D_EXPERT), BF16),
                        pltpu.VMEM((D_EXPERT, d), BF16),
                        pltpu.VMEM((2, MOE_BLK, half), jnp.uint32),
                        pltpu.VMEM((2, MOE_BLK, half), jnp.uint32),
                        pltpu.SemaphoreType.DMA((ring,)),
                        pltpu.SemaphoreType.DMA((2,)), pltpu.SemaphoreType.DMA((2,))])
    return pl.pallas_call(
        functools.partial(_expert_kernel, n_blk=p // MOE_BLK),
        grid_spec=grid_spec,
        out_shape=jax.ShapeDtypeStruct((p, half), jnp.uint32),
        compiler_params=_cparams(("arbitrary",)),
        name="moe_experts",
    )(n_blocks, first_block, xs, wg, wu, wd)


def _combine_kernel(d1_ref, d2_ref, d1n_ref, d2n_ref, x1_ref, rg_ref, g_ref, yb_ref, o_ref, buf, sem,
                    *, tt):
    i = pl.program_id(0)
    slot = i % 2

    def gather(a_ref, b_ref, s, wait):
        def body(t, carry):
            for j, idx in enumerate((a_ref, b_ref)):
                cp = _row_copy(yb_ref, idx[t], buf.at[s, j], t, sem.at[s])
                cp.wait() if wait else cp.start()
            return carry
        lax.fori_loop(0, tt, body, 0, unroll=8)

    @pl.when(i == 0)
    def _():
        gather(d1_ref, d2_ref, 0, False)

    @pl.when(i + 1 < pl.num_programs(0))
    def _():
        gather(d1n_ref, d2n_ref, 1 - slot, False)

    gather(d1_ref, d2_ref, slot, True)
    rg = rg_ref[...]
    lo1, hi1 = _unpack_bf16_pairs(buf[slot, 0])
    lo2, hi2 = _unpack_bf16_pairs(buf[slot, 1])
    g1, g2 = rg[:, 0:1], rg[:, 1:2]
    x = x1_ref[...] + jnp.concatenate([g1 * lo1 + g2 * lo2, g1 * hi1 + g2 * hi2], axis=1)
    ms = jnp.mean(x * x, axis=-1, keepdims=True)
    o_ref[...] = x * lax.rsqrt(ms + NORM_EPS) * g_ref[...]


def _combine(d1, d2, x1, rg, g, yb, tt=256):
    n, d = x1.shape
    steps = n // tt
    smem = pl.BlockSpec((tt,), lambda i: (i,), memory_space=pltpu.SMEM)
    smem_next = pl.BlockSpec((tt,), lambda i: (jnp.minimum(i + 1, steps - 1),),
                             memory_space=pltpu.SMEM)
    rowblk = lambda w: pl.BlockSpec((tt, w), lambda i: (i, 0))
    return pl.pallas_call(
        functools.partial(_combine_kernel, tt=tt),
        grid=(steps,),
        in_specs=[smem, smem, smem_next, smem_next, rowblk(d), rowblk(LANES),
                  pl.BlockSpec((1, d), lambda i: (0, 0)), pl.BlockSpec(memory_space=pl.ANY)],
        out_specs=rowblk(d),
        out_shape=jax.ShapeDtypeStruct((n, d), F32),
        scratch_shapes=[pltpu.VMEM((2, 2, tt, d // 2), jnp.uint32), pltpu.SemaphoreType.DMA((2,))],
        compiler_params=_cparams(("arbitrary",)),
        name="moe_combine",
    )(d1, d2, d1, d2, x1, rg, g, yb)


def _pad_cols(x, width):
    return jnp.pad(x, ((0, 0), (0, width - x.shape[1])))


def _rwkv_layout(x):
    return jnp.concatenate([
        x[:, :OFF_WD],
        _pad_cols(x[:, 3 * A_W:3 * A_W + W_LORA], LANES),
        _pad_cols(x[:, 3 * A_W + W_LORA:3 * A_W + W_LORA + A_LORA], LANES),
        _pad_cols(x[:, 3 * A_W + W_LORA + A_LORA:RWKV_PROJ], 2 * LANES)], axis=1)


def _rwkv_unlayout(x):
    return jnp.concatenate([x[:, :OFF_WD + W_LORA], x[:, OFF_AD:OFF_AD + A_LORA],
                            x[:, OFF_GD:OFF_GD + G_LORA]], axis=1)


def _pad_rows(w, rows):
    return jnp.pad(w, ((0, rows - w.shape[0]), (0, 0)))


def kernel(x_prompt, x_sample, state_rwkv_shift, state_rwkv_wkv, state_retention, norm_mix_g, w_in, mu_shift, decay_w0, decay_up, iclr_a0, iclr_up, gate_up, k_k, k_a, r_k, lnx_g, lnx_b, w_out, norm_ffn_g, router_coarse, router_fine, exp_w_gate, exp_w_up, exp_w_down, norm_final_g):
    bp, tp, d = x_prompt.shape
    bs, ts, _ = x_sample.shape
    row = lambda t: t.reshape(1, -1)
    l = 0

    w_in_p = jnp.concatenate([_rwkv_layout(w_in[l][:, :RWKV_PROJ]), w_in[l][:, RWKV_PROJ:]],
                             axis=1).astype(BF16)
    rwkv_wts = (_rwkv_layout(row(mu_shift[l])), row(decay_w0[l]),
                _pad_rows(decay_up[l], LANES).astype(BF16), row(iclr_a0[l]),
                _pad_rows(iclr_up[l], LANES).astype(BF16),
                _pad_rows(gate_up[l], 2 * LANES).astype(BF16),
                row(k_k[l]), row(k_a[l]), row(r_k[l]), row(lnx_g[l]), row(lnx_b[l]))
    w_out_a = w_out[l][:A_W].astype(BF16)
    w_out_b = w_out[l][A_W:].astype(BF16)
    w_router = _pad_cols(jnp.concatenate(
        [router_coarse[l], jnp.transpose(router_fine[l], (1, 0, 2)).reshape(d, N_EXPERTS)], axis=1),
        LANES).astype(BF16)

    groups = (
        (x_prompt.reshape(bp * tp, d), bp, tp, WKV_CHUNK, RET_CHUNK, 1, 0.0,
         jnp.zeros((bp, RWKV_PAD), F32), jnp.zeros((bp, A_HEADS, A_HD, A_HD), F32),
         jnp.zeros((bp, R_HEADS, R_HD, R_HD), F32)),
        (x_sample.reshape(bs * ts, d), bs, ts, ts, math.gcd(ts, RET_CHUNK), 8, float(PAST_LEN),
         _rwkv_layout(state_rwkv_shift[l]), state_rwkv_wkv[l], state_retention[l]),
    )
    x_last = jnp.concatenate([x_prompt[:, -1, :], x_sample[:, -1, :]], axis=0)
    n_last = -(-(bp + bs) // 8) * 8
    shifts = _rwkv_unlayout(_inproj(_pad_rows(x_last, n_last), row(norm_mix_g[l]), w_in_p, F32,
                                    RWKV_PAD, n_last, RWKV_PAD // 2))
    new_shifts = (shifts[:bp], shifts[bp:bp + bs])

    mixed = []
    for (x, b, t, wkv_chunk, ret_chunk, nb, pos0, shift0, wkv0, ret0), new_shift in zip(groups, new_shifts):
        proj = _inproj(x, row(norm_mix_g[l]), w_in_p, BF16, PROJ_PAD, 1024, 1536)
        y_a, new_wkv = _rwkv(proj, shift0, wkv0, rwkv_wts, b, t, wkv_chunk, nb,
                             WKV_SUB if nb == 1 else 1)
        y_b, new_ret = _retention(proj, ret0, b, t, ret_chunk, nb, pos0)
        x1, h2, ri, rg = _outproj(x, y_a, y_b, w_out_a, w_out_b, row(norm_ffn_g[l]), w_router)
        mixed.append((x1, h2, ri, rg, new_shift, new_wkv, new_ret))

    n_tok = [m[0].shape[0] for m in mixed]
    n_all = sum(n_tok)
    n_blk = -(-2 * n_all // MOE_BLK) + N_EXPERTS
    dest, info = _sort(jnp.concatenate([m[2] for m in mixed], axis=0))
    n_blocks, first_block = info[0, :N_EXPERTS], info[1, :N_EXPERTS]
    xs = jnp.zeros((n_blk * MOE_BLK, d // 2), jnp.uint32)
    off = 0
    dests = []
    for m, n in zip(mixed, n_tok):
        d1, d2 = dest[off:off + n, 0], dest[off:off + n, 1]
        dests.append((d1, d2))
        xs = _dispatch(d1, d2, m[1], xs)
        off += n
    yb = _experts(n_blocks, first_block, xs, exp_w_gate[l], exp_w_up[l], exp_w_down[l])
    outs = [_combine(d1, d2, m[0], m[3], row(norm_final_g), yb)
            for m, (d1, d2) in zip(mixed, dests)]

    (_, _, _, _, p_shift, p_wkv, p_ret), (_, _, _, _, s_shift, s_wkv, s_ret) = mixed
    return (outs[0].reshape(bp, tp, d), outs[1].reshape(bs, ts, d),
            p_shift[None], p_wkv[None], p_ret[None], s_shift[None], s_wkv[None], s_ret[None])
```

```python
import functools
import math

import jax
import jax.numpy as jnp
from jax import lax
from jax.experimental import pallas as pl
from jax.experimental.pallas import tpu as pltpu

F32 = jnp.float32
BF16 = jnp.bfloat16

D_MODEL = 2048
A_W = 1024
A_HD = 64
A_HEADS = 16
W_LORA = 64
A_LORA = 64
G_LORA = 160
RWKV_PROJ = 3 * A_W + W_LORA + A_LORA + G_LORA
R_W = 1024
R_HEADS = 8
R_HD = 128
RET_CHUNK = 128
N_GROUPS = 4
E_PER_GROUP = 8
N_EXPERTS = 32
D_EXPERT = 512
NORM_EPS = 1e-6
LNX_EPS = 64e-5
DECAY_SCALE = math.exp(-0.5)
PAST_LEN = 16384

LANES = 128
OFF_WD = 3 * A_W
OFF_AD = OFF_WD + LANES
OFF_GD = OFF_AD + LANES
RWKV_PAD = OFF_GD + 2 * LANES
PROJ_PAD = RWKV_PAD + 4 * R_W
RET_BLK = 512
WKV_CHUNK = 64
WKV_SUB = 2
MOE_BLK = 256
W_AHEAD = 2
W_DMA_PRIORITY = 1
VMEM_LIMIT = 56 * 1024 * 1024


def _cparams(sem):
    return pltpu.CompilerParams(dimension_semantics=sem, vmem_limit_bytes=VMEM_LIMIT)


def _bdot(a, b):
    return jnp.dot(a.astype(BF16), b.astype(BF16), preferred_element_type=F32)


def _bdot_nt(a, b):
    return lax.dot_general(a.astype(BF16), b.astype(BF16), (((1,), (1,)), ((), ())),
                           preferred_element_type=F32)


def _bdot_tn(a, b):
    return lax.dot_general(a.astype(BF16), b.astype(BF16), (((0,), (0,)), ((), ())),
                           preferred_element_type=F32)


def _split3(x):
    hi = x.astype(BF16)
    r1 = x - hi.astype(F32)
    mid = r1.astype(BF16)
    lo = (r1 - mid.astype(F32)).astype(BF16)
    return hi, mid, lo


def _dot3(x, w_bf16):
    hi, mid, lo = _split3(x)
    d = lambda p: jnp.dot(p, w_bf16, preferred_element_type=F32)
    return d(hi) + d(mid) + d(lo)


def _sigmoid(x):
    return 1.0 / (1.0 + jnp.exp(-x))


def _pack_bf16_pairs(x):
    w = x.shape[1] // 2
    lo = lax.bitcast_convert_type(x[:, :w].astype(BF16).astype(F32), jnp.uint32)
    hi = lax.bitcast_convert_type(x[:, w:].astype(BF16).astype(F32), jnp.uint32)
    return hi | (lo >> 16)


def _unpack_bf16_pairs(p):
    lo = lax.bitcast_convert_type(p << 16, F32)
    hi = lax.bitcast_convert_type(p & jnp.uint32(0xFFFF0000), F32)
    return lo, hi


def _inproj_kernel(x_ref, g_ref, w_ref, o_ref, h_ref):
    @pl.when(pl.program_id(1) == 0)
    def _():
        x = x_ref[...]
        ms = jnp.mean(x * x, axis=-1, keepdims=True)
        h_ref[...] = (x * lax.rsqrt(ms + NORM_EPS) * g_ref[...]).astype(BF16)

    o_ref[...] = jnp.dot(h_ref[...], w_ref[...], preferred_element_type=F32).astype(o_ref.dtype)


def _inproj(x, g, w_bf16, out_dtype, n_cols, tm, tn):
    n, d = x.shape
    return pl.pallas_call(
        _inproj_kernel,
        grid=(n // tm, n_cols // tn),
        in_specs=[pl.BlockSpec((tm, d), lambda i, j: (i, 0)),
                  pl.BlockSpec((1, d), lambda i, j: (0, 0)),
                  pl.BlockSpec((d, tn), lambda i, j: (0, j))],
        out_specs=pl.BlockSpec((tm, tn), lambda i, j: (i, j)),
        out_shape=jax.ShapeDtypeStruct((n, n_cols), out_dtype),
        scratch_shapes=[pltpu.VMEM((tm, d), BF16)],
        compiler_params=_cparams(("parallel", "arbitrary")),
        name="inproj",
    )(x, g, w_bf16)


def _seg_sum(x, ones_bd):
    nslab = x.shape[1] // LANES
    stacked = jnp.concatenate([x[:, s * LANES:(s + 1) * LANES] for s in range(nslab)], axis=0)
    red = _dot3(stacked, ones_bd)
    c = x.shape[0]
    return jnp.concatenate([red[s * c:(s + 1) * c] for s in range(nslab)], axis=1)


def _rwkv_chunk(p, before, sbd_ref, mu_ref, w0_ref, wup_ref, a0_ref, aup_ref, gup_ref,
                kk_ref, ka_ref, rk_ref, lng_ref, lnb_ref, *, chunk, nb):
    C = chunk
    R = nb * C
    pairs = range(A_HEADS // 2)
    seqs = range(nb)
    lane = lax.broadcasted_iota(jnp.int32, (1, LANES), 1)
    in_a = lane < A_HD

    row = lax.broadcasted_iota(jnp.int32, (R, 1), 0)
    pos = row % C
    first = jnp.broadcast_to(before, (nb, C, RWKV_PAD)).reshape(R, RWKV_PAD)
    prev = jnp.where(pos == 0, first, pltpu.roll(p, 1, axis=0))
    xm = p + (prev - p) * mu_ref[...]

    r = xm[:, 0:A_W]
    k = xm[:, A_W:2 * A_W]
    v = xm[:, 2 * A_W:3 * A_W]
    wd = xm[:, OFF_WD:OFF_AD]
    ad = xm[:, OFF_AD:OFF_GD]
    gd = xm[:, OFF_GD:RWKV_PAD]

    logw = -DECAY_SCALE * _sigmoid(w0_ref[...] + _bdot(jnp.tanh(wd), wup_ref[...]))
    a = _sigmoid(a0_ref[...] + _bdot(ad, aup_ref[...]))
    g = _bdot(_sigmoid(gd), gup_ref[...])

    ones_bd = (lax.broadcasted_iota(jnp.int32, (LANES, LANES), 0) // A_HD ==
               lax.broadcasted_iota(jnp.int32, (LANES, LANES), 1) // A_HD).astype(BF16)

    kk = k * kk_ref[...]
    k2 = k * (1.0 + (a - 1.0) * ka_ref[...])
    sums = _seg_sum(jnp.concatenate([kk * kk, r * k2 * rk_ref[...]], axis=0), ones_bd)
    kk = kk / jnp.maximum(jnp.sqrt(sums[0:R]), 1e-12)
    bonus = sums[R:2 * R]
    b = kk * a

    cum = logw
    suf = logw
    d = 1
    while d < C:
        cum = cum + jnp.where(pos >= d, pltpu.roll(cum, d, axis=0), 0.0)
        suf = suf + jnp.where(pos < C - d, pltpu.roll(suf, R - d, axis=0), 0.0)
        d *= 2
    after = suf - logw
    w_incl = jnp.exp(cum)
    w_excl = jnp.exp(cum - logw)
    w_inv = jnp.exp(-cum)
    w_end = jnp.exp(after)
    w_all = jnp.exp(cum + after)

    kk_d = kk * w_excl
    r_d = r * w_incl
    b_h = b * w_inv
    k_h = k2 * w_inv
    b_e = b * w_end
    k_e = k2 * w_end

    R2 = 2 * R
    ri = lax.broadcasted_iota(jnp.int32, (R2, R2), 0)
    ci = lax.broadcasted_iota(jnp.int32, (R2, R2), 1)
    same = (ri // C) == (ci // C)
    strict = same & (ci < ri)
    incl = same & (ci <= ri)
    eye = (ri == ci).astype(F32)

    def expand(x):
        return jnp.concatenate([jnp.where(in_a, x, 0.0), jnp.where(in_a, 0.0, x)], axis=0)

    def seq_rows(x, s):
        if nb == 1:
            return x
        return jnp.concatenate([x[s * C:(s + 1) * C], x[R + s * C:R + (s + 1) * C]], axis=0)

    def from_seqs(parts):
        if nb == 1:
            return parts[0]
        return jnp.concatenate([q[0:C] for q in parts] + [q[C:2 * C] for q in parts], axis=0)

    sl = [slice(pr * LANES, (pr + 1) * LANES) for pr in pairs]
    ex = lambda x: [expand(x[:, sl[pr]]) for pr in pairs]
    e_kk, e_r, e_bh, e_kh, e_be, e_ke = ex(kk_d), ex(r_d), ex(b_h), ex(k_h), ex(b_e), ex(k_e)
    v_e = [t.astype(BF16) for t in ex(v)]
    kk_bf = [t.astype(BF16) for t in e_kk]
    r_bf = [t.astype(BF16) for t in e_r]
    gram = [_bdot_nt(jnp.concatenate([kk_bf[pr], r_bf[pr]], axis=0),
                     jnp.concatenate([e_bh[pr], e_kh[pr]], axis=0)) for pr in pairs]
    l_b = [jnp.where(strict, gram[pr][0:R2, 0:R2], 0.0) for pr in pairs]
    l_k = [jnp.where(strict, gram[pr][0:R2, R2:2 * R2], 0.0) for pr in pairs]
    a_bk = [jnp.concatenate([jnp.where(incl, gram[pr][R2:2 * R2, 0:R2], 0.0),
                             jnp.where(incl, gram[pr][R2:2 * R2, R2:2 * R2], 0.0)],
                            axis=1).astype(BF16) for pr in pairs]
    t_inv = [eye - l_b[pr] for pr in pairs]
    pw = [t.astype(BF16) for t in l_b]
    e = 2
    while e < C:
        pw = [_bdot(pw[pr], pw[pr]).astype(BF16) for pr in pairs]
        t_inv = [t_inv[pr] + _bdot(t_inv[pr], pw[pr]) for pr in pairs]
        e *= 2
    lkv = [_bdot(l_k[pr], v_e[pr]) for pr in pairs]
    xs_kk, xs_r = [], []
    for pr in pairs:
        parts = [_bdot_nt(jnp.concatenate([seq_rows(kk_bf[pr], s), seq_rows(r_bf[pr], s)], axis=0),
                          sbd_ref[s, pr]) for s in seqs]
        xs_kk.append(from_seqs([q[0:2 * C] for q in parts]))
        xs_r.append(from_seqs([q[2 * C:4 * C] for q in parts]))
    u_e = [-_bdot(t_inv[pr], xs_kk[pr] + lkv[pr]).astype(BF16) for pr in pairs]
    uv = [jnp.concatenate([u_e[pr], v_e[pr]], axis=0) for pr in pairs]
    y_e = [xs_r[pr] + _bdot(a_bk[pr], uv[pr]) for pr in pairs]
    y = jnp.concatenate([y_e[pr][0:R] + y_e[pr][R:R2] for pr in pairs], axis=1)
    for pr in pairs:
        be_bf, ke_bf = e_be[pr].astype(BF16), e_ke[pr].astype(BF16)
        for s in seqs:
            upd = _bdot_tn(jnp.concatenate([seq_rows(u_e[pr], s), seq_rows(v_e[pr], s)], axis=0),
                           jnp.concatenate([seq_rows(be_bf, s), seq_rows(ke_bf, s)], axis=0))
            sbd_ref[s, pr] = sbd_ref[s, pr] * w_all[s * C:s * C + 1, sl[pr]] + upd

    inv_hd = 1.0 / A_HD
    mean = _seg_sum(y, ones_bd) * inv_hd
    yc = y - mean
    var = _seg_sum(yc * yc, ones_bd) * inv_hd
    yn = yc * lax.rsqrt(var + LNX_EPS) * lng_ref[...] + lnb_ref[...]
    yn = yn + bonus * v
    return yn * g


def _rwkv_kernel(p_ref, shift_ref, s0_ref, *rest, chunk, nb, sub):
    wrefs, (y_ref, sout_ref, sbd_ref, carry_ref) = rest[:-4], rest[-4:]
    c_idx = pl.program_id(1)
    pairs = range(A_HEADS // 2)
    rows = nb * chunk

    @pl.when(c_idx == 0)
    def _():
        carry_ref[...] = shift_ref[...]
        z = jnp.zeros((A_HD, A_HD), F32)
        for s in range(nb):
            for pr in pairs:
                sa = s0_ref[s, 2 * pr]
                sb = s0_ref[s, 2 * pr + 1]
                sbd_ref[s, pr] = jnp.concatenate(
                    [jnp.concatenate([sa, z], axis=1), jnp.concatenate([z, sb], axis=1)], axis=0)

    p_all = p_ref[...].astype(F32)
    before = carry_ref[...]
    for i in range(sub):
        p = p_all[i * rows:(i + 1) * rows]
        y = _rwkv_chunk(p, before, sbd_ref, *wrefs, chunk=chunk, nb=nb)
        y_ref[i * rows:(i + 1) * rows, :] = y.astype(y_ref.dtype)
        before = p[rows - 1:rows, :].reshape(1, 1, RWKV_PAD)
    if nb == 1:
        carry_ref[...] = before

    @pl.when(c_idx == pl.num_programs(1) - 1)
    def _():
        for s in range(nb):
            for pr in pairs:
                s_bd = sbd_ref[s, pr]
                sout_ref[s, 2 * pr] = s_bd[0:A_HD, 0:A_HD]
                sout_ref[s, 2 * pr + 1] = s_bd[A_HD:2 * A_HD, A_HD:2 * A_HD]


def _rwkv(proj, shift0, s0, wts, batch, seq, chunk, nb, sub):
    n_chunks = seq // (chunk * sub)
    assert nb == 1 or (n_chunks == 1 and sub == 1)
    rows = nb * chunk * sub
    vec = lambda w: pl.BlockSpec(w.shape, lambda b, c: (0,) * w.ndim)
    st = pl.BlockSpec((nb, A_HEADS, A_HD, A_HD), lambda b, c: (b, 0, 0, 0))
    return pl.pallas_call(
        functools.partial(_rwkv_kernel, chunk=chunk, nb=nb, sub=sub),
        grid=(batch // nb, n_chunks),
        in_specs=[pl.BlockSpec((rows, RWKV_PAD), lambda b, c: (b * n_chunks + c, 0)),
                  pl.BlockSpec((nb, 1, RWKV_PAD), lambda b, c: (b, 0, 0)), st]
                 + [vec(w) for w in wts],
        out_specs=[pl.BlockSpec((rows, A_W), lambda b, c: (b * n_chunks + c, 0)), st],
        out_shape=[jax.ShapeDtypeStruct((batch * seq, A_W), BF16),
                   jax.ShapeDtypeStruct((batch, A_HEADS, A_HD, A_HD), F32)],
        scratch_shapes=[pltpu.VMEM((nb, A_HEADS // 2, LANES, LANES), F32),
                        pltpu.VMEM((nb, 1, RWKV_PAD), F32)],
        compiler_params=_cparams(("parallel", "arbitrary")),
        name="rwkv7",
    )(proj, shift0.reshape(batch, 1, RWKV_PAD), s0, *wts)


def _ret_kernel(q0_ref, q1_ref, k0_ref, k1_ref, v0_ref, v1_ref, g0_ref, g1_ref, cos_ref, sin_ref,
                din_ref, dq_ref, dk_ref, dc_ref, r0_ref, y_ref, rout_ref, r_scr, *, chunk, nb):
    C = chunk
    half = RET_BLK // R_HD
    c_idx = pl.program_id(1)

    @pl.when(c_idx == 0)
    def _():
        r_scr[...] = r0_ref[...]

    even = lax.broadcasted_iota(jnp.int32, (1, R_HD), 1) % 2 == 0
    cos = cos_ref[...]
    sin = sin_ref[...]

    def rot(t):
        partner = jnp.where(even, pltpu.roll(t, R_HD - 1, axis=1), pltpu.roll(t, 1, axis=1))
        return t * cos + partner * sin

    def load(halves, s, h):
        return halves[h // half][s * C:(s + 1) * C, (h % half) * R_HD:(h % half + 1) * R_HD]

    f32 = lambda *refs: tuple(r[...].astype(F32) for r in refs)
    items = [(s, h) for s in range(nb) for h in range(R_HEADS)]
    q_in, k_in, v_in, g_in = f32(q0_ref, q1_ref), f32(k0_ref, k1_ref), f32(v0_ref, v1_ref), f32(g0_ref, g1_ref)
    q = [rot(load(q_in, s, h)) for s, h in items]
    k = [rot(load(k_in, s, h)) * (R_HD ** -0.5) for s, h in items]
    v = [load(v_in, s, h).astype(BF16) for s, h in items]
    state = [r_scr[s, h] for s, h in items]
    scores = [_bdot_nt(q[i], k[i]) * din_ref[h] for i, (s, h) in enumerate(items)]
    o = [_bdot(jnp.concatenate([scores[i], q[i] * dq_ref[h]], axis=1),
               jnp.concatenate([v[i], state[i].astype(BF16)], axis=0))
         for i, (s, h) in enumerate(items)]
    for i, (s, h) in enumerate(items):
        r_scr[s, h] = state[i] * dc_ref[h, 0:1, :] + _bdot_tn(k[i] * dk_ref[h], v[i])
    for i, (s, h) in enumerate(items):
        on = o[i] * lax.rsqrt(jnp.mean(o[i] * o[i], axis=-1, keepdims=True) + NORM_EPS)
        gate = load(g_in, s, h)
        y_ref[s * C:(s + 1) * C, h * R_HD:(h + 1) * R_HD] = (
            gate * _sigmoid(gate) * on).astype(y_ref.dtype)

    @pl.when(c_idx == pl.num_programs(1) - 1)
    def _():
        rout_ref[...] = r_scr[...]


def _ret_tables(chunk, pos0, seq):
    pos = pos0 + jnp.arange(seq, dtype=F32)
    theta = 1.0 / (10000.0 ** jnp.linspace(0.0, 1.0, R_HD // 2, dtype=F32))
    ang = pos[:, None] * theta[None, :]
    cos2 = jnp.repeat(jnp.cos(ang), 2, axis=1)
    sin2 = jnp.stack([-jnp.sin(ang), jnp.sin(ang)], axis=-1).reshape(seq, R_HD)
    lg = jnp.log1p(-(2.0 ** (-5.0 - jnp.arange(R_HEADS, dtype=F32))))
    idx = jnp.arange(chunk, dtype=F32)
    diff = idx[:, None] - idx[None, :]
    causal = diff >= 0
    din = jnp.where(causal, jnp.exp(jnp.where(causal, diff, 0.0)[None] * lg[:, None, None]), 0.0)
    bc = lambda t: jnp.broadcast_to(t[:, :, None], (R_HEADS, t.shape[1], R_HD))
    dq = bc(jnp.exp((idx + 1.0)[None, :] * lg[:, None]))
    dk = bc(jnp.exp((chunk - 1.0 - idx)[None, :] * lg[:, None]))
    dc = jnp.broadcast_to(jnp.exp(chunk * lg)[:, None, None], (R_HEADS, 8, R_HD))
    return cos2, sin2, din, dq, dk, dc


def _retention(proj, r0, batch, seq, chunk, nb, pos0):
    n_chunks = seq // chunk
    cos2, sin2, din, dq, dk, dc = _ret_tables(chunk, pos0, seq)
    col0 = RWKV_PAD // RET_BLK
    rows = nb * chunk

    def sec(i):
        return pl.BlockSpec((rows, RET_BLK), lambda b, c: (b * n_chunks + c, col0 + i))

    tab = lambda t: pl.BlockSpec(t.shape, lambda b, c: (0, 0, 0))
    st = pl.BlockSpec((nb, R_HEADS, R_HD, R_HD), lambda b, c: (b, 0, 0, 0))
    return pl.pallas_call(
        functools.partial(_ret_kernel, chunk=chunk, nb=nb),
        grid=(batch // nb, n_chunks),
        in_specs=[sec(i) for i in range(8)]
                 + [pl.BlockSpec((chunk, R_HD), lambda b, c: (c, 0)),
                    pl.BlockSpec((chunk, R_HD), lambda b, c: (c, 0)),
                    tab(din), tab(dq), tab(dk), tab(dc), st],
        out_specs=[pl.BlockSpec((rows, R_W), lambda b, c: (b * n_chunks + c, 0)), st],
        out_shape=[jax.ShapeDtypeStruct((batch * seq, R_W), BF16),
                   jax.ShapeDtypeStruct((batch, R_HEADS, R_HD, R_HD), F32)],
        scratch_shapes=[pltpu.VMEM((nb, R_HEADS, R_HD, R_HD), F32)],
        compiler_params=_cparams(("parallel", "arbitrary")),
        name="retention",
    )(*([proj] * 8), cos2, sin2, din, dq, dk, dc, r0)


def _outproj_kernel(x_ref, ya_ref, yb_ref, wa_ref, wb_ref, g_ref, wr_ref,
                    x1_ref, h2_ref, ri_ref, rg_ref):
    x1 = (x_ref[...] + jnp.dot(ya_ref[...], wa_ref[...], preferred_element_type=F32)
          + jnp.dot(yb_ref[...], wb_ref[...], preferred_element_type=F32))
    x1_ref[...] = x1
    ms = jnp.mean(x1 * x1, axis=-1, keepdims=True)
    h2 = x1 * lax.rsqrt(ms + NORM_EPS) * g_ref[...]
    h2_ref[...] = _pack_bf16_pairs(h2)
    logits = _bdot(h2, wr_ref[...])
    lane = lax.broadcasted_iota(jnp.int32, (1, LANES), 1)
    neg = -1e30
    big = LANES
    rmax = lambda t: jnp.max(t, axis=-1, keepdims=True)
    rmin = lambda t: jnp.min(t, axis=-1, keepdims=True)
    is_c = lane < N_GROUPS
    lc = jnp.where(is_c, logits, neg)
    mc = rmax(lc)
    grp = rmin(jnp.where(lc == mc, lane, big))
    p_grp = 1.0 / jnp.sum(jnp.where(is_c, jnp.exp(lc - mc), 0.0), axis=-1, keepdims=True)
    fine = lane - N_GROUPS
    in_g = (fine >= 0) & (fine < N_EXPERTS) & ((fine // E_PER_GROUP) == grp)
    lf = jnp.where(in_g, logits, neg)
    m1 = rmax(lf)
    i1 = rmin(jnp.where(lf == m1, lane, big))
    lf2 = jnp.where(lane == i1, neg, lf)
    m2 = rmax(lf2)
    i2 = rmin(jnp.where(lf2 == m2, lane, big))
    e2 = jnp.exp(m2 - m1)
    g1 = p_grp / (1.0 + e2)
    g2 = p_grp * e2 / (1.0 + e2)
    ri_ref[...] = jnp.where(lane == 0, i1 - N_GROUPS, jnp.where(lane == 1, i2 - N_GROUPS, 0))
    rg_ref[...] = jnp.where(lane == 0, g1, jnp.where(lane == 1, g2, 0.0))


def _outproj(x, ya, yb, wa, wb, g, wr, tm=512):
    n, d = x.shape
    rowblk = lambda w: pl.BlockSpec((tm, w), lambda i: (i, 0))
    full = lambda w: pl.BlockSpec(w.shape, lambda i: (0, 0))
    return pl.pallas_call(
        _outproj_kernel,
        grid=(n // tm,),
        in_specs=[rowblk(d), rowblk(A_W), rowblk(R_W), full(wa), full(wb), full(g), full(wr)],
        out_specs=[rowblk(d), rowblk(d // 2), rowblk(LANES), rowblk(LANES)],
        out_shape=[jax.ShapeDtypeStruct((n, d), F32), jax.ShapeDtypeStruct((n, d // 2), jnp.uint32),
                   jax.ShapeDtypeStruct((n, LANES), jnp.int32),
                   jax.ShapeDtypeStruct((n, LANES), F32)],
        compiler_params=_cparams(("parallel",)),
        name="outproj_router",
    )(x, ya, yb, wa, wb, g, wr)


def _sort_kernel(ri_ref, tri_ref, dest_ref, info_ref, cnt_ref, base_ref):
    ph = pl.program_id(0)
    i = pl.program_id(1)
    lane = lax.broadcasted_iota(jnp.int32, (1, LANES), 1)
    ri = ri_ref[...]
    oh1 = (lane == ri[:, 0:1]).astype(F32)
    oh2 = (lane == ri[:, 1:2]).astype(F32)
    tot1 = jnp.sum(oh1, axis=0, keepdims=True)
    tot2 = jnp.sum(oh2, axis=0, keepdims=True)

    @pl.when(ph == 0)
    def _():
        @pl.when(i == 0)
        def _():
            cnt_ref[...] = jnp.zeros_like(cnt_ref)
        cnt_ref[...] += tot1 + tot2

    @pl.when(ph == 1)
    def _():
        @pl.when(i == 0)
        def _():
            nblk = jnp.floor((cnt_ref[...] + (MOE_BLK - 1)) * (1.0 / MOE_BLK))
            upper = (lax.broadcasted_iota(jnp.int32, (LANES, LANES), 0) <=
                     lax.broadcasted_iota(jnp.int32, (LANES, LANES), 1))
            pend = _bdot(jnp.broadcast_to(nblk, (8, LANES)), upper.astype(F32))[0:1]
            first = pend - nblk
            base_ref[...] = first * MOE_BLK
            srow = lax.broadcasted_iota(jnp.int32, (8, LANES), 0)
            info_ref[...] = jnp.where(srow == 0, nblk, jnp.where(srow == 1, first, 0.0)).astype(jnp.int32)

        base = base_ref[...]
        tri = tri_ref[...]
        pre1 = jnp.dot(tri, oh1.astype(BF16), preferred_element_type=F32)
        pre2 = jnp.dot(tri, oh2.astype(BF16), preferred_element_type=F32) + tot1
        d1 = jnp.sum(oh1 * (base + pre1), axis=1, keepdims=True)
        d2 = jnp.sum(oh2 * (base + pre2), axis=1, keepdims=True)
        dest_ref[...] = jnp.where(lane == 0, d1, jnp.where(lane == 1, d2, 0.0)).astype(jnp.int32)
        base_ref[...] = base + tot1 + tot2


def _sort(ri, tt=512):
    n = ri.shape[0]
    tri = (jnp.arange(tt)[:, None] > jnp.arange(tt)[None, :]).astype(BF16)
    return pl.pallas_call(
        _sort_kernel,
        grid=(2, n // tt),
        in_specs=[pl.BlockSpec((tt, LANES), lambda ph, i: (i, 0)),
                  pl.BlockSpec((tt, tt), lambda ph, i: (0, 0))],
        out_specs=[pl.BlockSpec((tt, LANES), lambda ph, i: (i * ph, 0)),
                   pl.BlockSpec((8, LANES), lambda ph, i: (0, 0))],
        out_shape=[jax.ShapeDtypeStruct((n, LANES), jnp.int32),
                   jax.ShapeDtypeStruct((8, LANES), jnp.int32)],
        scratch_shapes=[pltpu.VMEM((1, LANES), F32), pltpu.VMEM((1, LANES), F32)],
        compiler_params=_cparams(("arbitrary", "arbitrary")),
        name="moe_sort",
    )(ri, tri)


def _row_copy(src_ref, src_row, dst_ref, dst_row, sem):
    return pltpu.make_async_copy(src_ref.at[pl.ds(src_row, 1)], dst_ref.at[pl.ds(dst_row, 1)], sem)


def _dispatch_kernel(d1_ref, d2_ref, h_ref, xs_in_ref, xs_ref, sem, *, tt):
    del xs_in_ref

    def issue(t, carry):
        _row_copy(h_ref, t, xs_ref, d1_ref[t], sem).start()
        _row_copy(h_ref, t, xs_ref, d2_ref[t], sem).start(priority=1)
        return carry

    def drain(t, carry):
        _row_copy(h_ref, t, xs_ref, d1_ref[t], sem).wait()
        _row_copy(h_ref, t, xs_ref, d2_ref[t], sem).wait()
        return carry

    lax.fori_loop(0, tt, issue, 0, unroll=8)
    lax.fori_loop(0, tt, drain, 0, unroll=8)


def _dispatch(d1, d2, h2, xs, tt=512):
    n, d = h2.shape
    smem = pl.BlockSpec((tt,), lambda i: (i,), memory_space=pltpu.SMEM)
    anyspec = pl.BlockSpec(memory_space=pl.ANY)
    return pl.pallas_call(
        functools.partial(_dispatch_kernel, tt=tt),
        grid=(n // tt,),
        in_specs=[smem, smem, pl.BlockSpec((tt, d), lambda i: (i, 0)), anyspec],
        out_specs=anyspec,
        out_shape=jax.ShapeDtypeStruct(xs.shape, xs.dtype),
        scratch_shapes=[pltpu.SemaphoreType.DMA(())],
        input_output_aliases={3: 0},
        compiler_params=_cparams(("arbitrary",)),
        name="moe_dispatch",
    )(d1, d2, h2, xs)


def _expert_kernel(nb_ref, fb_ref, xs_hbm, wg_hbm, wu_hbm, wd_hbm, y_hbm,
                   wg_r, wu_r, wd_r, wg_s, wu_s, wd_s, xbuf, ybuf, sem_w, sem_in, sem_out, *, n_blk):
    e = pl.program_id(0)
    last = pl.num_programs(0) - 1
    n = nb_ref[e]
    b0 = fb_ref[e]
    n_used = fb_ref[last] + nb_ref[last]
    half = xbuf.shape[2]
    ring = wg_r.shape[0]

    def fetch(ex):
        slot = ex % ring
        return [pltpu.make_async_copy(src.at[ex], dst.at[slot], sem_w.at[slot])
                for src, dst in ((wg_hbm, wg_r), (wu_hbm, wu_r), (wd_hbm, wd_r))]

    @pl.when(e == 0)
    def _():
        for ex in range(ring - 1):
            for cp in fetch(ex):
                cp.start(priority=W_DMA_PRIORITY)

    @pl.when(e + ring - 1 <= last)
    def _():
        for cp in fetch(e + ring - 1):
            cp.start(priority=W_DMA_PRIORITY)

    for cp in fetch(e):
        cp.wait()
    w_slot = e % ring

    def load(blk, slot):
        return pltpu.make_async_copy(xs_hbm.at[pl.ds(blk * MOE_BLK, MOE_BLK)],
                                     xbuf.at[slot], sem_in.at[slot])

    def store(blk, slot):
        return pltpu.make_async_copy(ybuf.at[slot], y_hbm.at[pl.ds(blk * MOE_BLK, MOE_BLK)],
                                     sem_out.at[slot])

    @pl.when(n > 0)
    def _():
        @pl.when(b0 == 0)
        def _():
            load(0, 0).start()

        wg_s[...] = wg_r[w_slot].astype(BF16)
        wu_s[...] = wu_r[w_slot].astype(BF16)
        wd_s[...] = wd_r[w_slot].astype(BF16)

        def body(blk, carry):
            slot = blk % 2
            load(blk, slot).wait()

            @pl.when(blk + 1 < n_used)
            def _():
                load(blk + 1, 1 - slot).start()

            @pl.when(blk >= 2)
            def _():
                store(blk - 2, slot).wait()

            x_lo, x_hi = _unpack_bf16_pairs(xbuf[slot])
            x_lo, x_hi = x_lo.astype(BF16), x_hi.astype(BF16)
            mm = lambda w_s: (jnp.dot(x_lo, w_s[0:half], preferred_element_type=F32)
                              + jnp.dot(x_hi, w_s[half:2 * half], preferred_element_type=F32))
            hg = mm(wg_s)
            hu = mm(wu_s)
            act = (hg * _sigmoid(hg) * hu).astype(BF16)
            ybuf[slot] = _pack_bf16_pairs(jnp.dot(act, wd_s[...], preferred_element_type=F32))
            store(blk, slot).start()
            return carry

        lax.fori_loop(b0, b0 + n, body, 0)

    @pl.when(e == last)
    def _():
        @pl.when(n_used >= 2)
        def _():
            store(n_used - 2, n_used % 2).wait()

        @pl.when(n_used >= 1)
        def _():
            store(n_used - 1, (n_used - 1) % 2).wait()

        ybuf[0] = jnp.zeros(ybuf.shape[1:], ybuf.dtype)

        def fill(blk, carry):
            cp = store(blk, 0)
            cp.start()
            cp.wait()
            return carry

        lax.fori_loop(n_used, n_blk, fill, 0)


def _experts(n_blocks, first_block, xs, wg, wu, wd):
    p, half = xs.shape
    d = 2 * half
    anyspec = pl.BlockSpec(memory_space=pl.ANY)
    ring = W_AHEAD + 1
    grid_spec = pltpu.PrefetchScalarGridSpec(
        num_scalar_prefetch=2,
        grid=(N_EXPERTS,),
        in_specs=[anyspec] * 4,
        out_specs=anyspec,
        scratch_shapes=[pltpu.VMEM((ring, d, D_EXPERT), F32), pltpu.VMEM((ring, d, D_EXPERT), F32),
                        pltpu.VMEM((ring, D_EXPERT, d), F32),
                        pltpu.VMEM((d, D_EXPERT), BF16), pltpu.VMEM((d, D_EXPERT), BF16),
                        pltpu.VMEM((D_EXPERT, d), BF16),
                        pltpu.VMEM((2, MOE_BLK, half), jnp.uint32),
                        pltpu.VMEM((2, MOE_BLK, half), jnp.uint32),
                        pltpu.SemaphoreType.DMA((ring,)),
                        pltpu.SemaphoreType.DMA((2,)), pltpu.SemaphoreType.DMA((2,))])
    return pl.pallas_call(
        functools.partial(_expert_kernel, n_blk=p // MOE_BLK),
        grid_spec=grid_spec,
        out_shape=jax.ShapeDtypeStruct((p, half), jnp.uint32),
        compiler_params=_cparams(("arbitrary",)),
        name="moe_experts",
    )(n_blocks, first_block, xs, wg, wu, wd)


def _combine_kernel(d1_ref, d2_ref, d1n_ref, d2n_ref, x1_ref, rg_ref, g_ref, yb_ref, o_ref, buf, sem,
                    *, tt):
    i = pl.program_id(0)
    slot = i % 2

    def gather(a_ref, b_ref, s, wait):
        def body(t, carry):
            for j, idx in enumerate((a_ref, b_ref)):
                cp = _row_copy(yb_ref, idx[t], buf.at[s, j], t, sem.at[s])
                cp.wait() if wait else cp.start(priority=j)
            return carry
        lax.fori_loop(0, tt, body, 0, unroll=8)

    @pl.when(i == 0)
    def _():
        gather(d1_ref, d2_ref, 0, False)

    @pl.when(i + 1 < pl.num_programs(0))
    def _():
        gather(d1n_ref, d2n_ref, 1 - slot, False)

    gather(d1_ref, d2_ref, slot, True)
    rg = rg_ref[...]
    lo1, hi1 = _unpack_bf16_pairs(buf[slot, 0])
    lo2, hi2 = _unpack_bf16_pairs(buf[slot, 1])
    g1, g2 = rg[:, 0:1], rg[:, 1:2]
    x = x1_ref[...] + jnp.concatenate([g1 * lo1 + g2 * lo2, g1 * hi1 + g2 * hi2], axis=1)
    ms = jnp.mean(x * x, axis=-1, keepdims=True)
    o_ref[...] = x * lax.rsqrt(ms + NORM_EPS) * g_ref[...]


def _combine(d1, d2, x1, rg, g, yb, tt=256):
    n, d = x1.shape
    steps = n // tt
    smem = pl.BlockSpec((tt,), lambda i: (i,), memory_space=pltpu.SMEM)
    smem_next = pl.BlockSpec((tt,), lambda i: (jnp.minimum(i + 1, steps - 1),),
                             memory_space=pltpu.SMEM)
    rowblk = lambda w: pl.BlockSpec((tt, w), lambda i: (i, 0))
    return pl.pallas_call(
        functools.partial(_combine_kernel, tt=tt),
        grid=(steps,),
        in_specs=[smem, smem, smem_next, smem_next, rowblk(d), rowblk(LANES),
                  pl.BlockSpec((1, d), lambda i: (0, 0)), pl.BlockSpec(memory_space=pl.ANY)],
        out_specs=rowblk(d),
        out_shape=jax.ShapeDtypeStruct((n, d), F32),
        scratch_shapes=[pltpu.VMEM((2, 2, tt, d // 2), jnp.uint32), pltpu.SemaphoreType.DMA((2,))],
        compiler_params=_cparams(("arbitrary",)),
        name="moe_combine",
    )(d1, d2, d1, d2, x1, rg, g, yb)


def _pad_cols(x, width):
    return jnp.pad(x, ((0, 0), (0, width - x.shape[1])))


def _rwkv_layout(x):
    return jnp.concatenate([
        x[:, :OFF_WD],
        _pad_cols(x[:, 3 * A_W:3 * A_W + W_LORA], LANES),
        _pad_cols(x[:, 3 * A_W + W_LORA:3 * A_W + W_LORA + A_LORA], LANES),
        _pad_cols(x[:, 3 * A_W + W_LORA + A_LORA:RWKV_PROJ], 2 * LANES)], axis=1)


def _rwkv_unlayout(x):
    return jnp.concatenate([x[:, :OFF_WD + W_LORA], x[:, OFF_AD:OFF_AD + A_LORA],
                            x[:, OFF_GD:OFF_GD + G_LORA]], axis=1)


def _pad_rows(w, rows):
    return jnp.pad(w, ((0, rows - w.shape[0]), (0, 0)))


def kernel(x_prompt, x_sample, state_rwkv_shift, state_rwkv_wkv, state_retention, norm_mix_g, w_in, mu_shift, decay_w0, decay_up, iclr_a0, iclr_up, gate_up, k_k, k_a, r_k, lnx_g, lnx_b, w_out, norm_ffn_g, router_coarse, router_fine, exp_w_gate, exp_w_up, exp_w_down, norm_final_g):
    bp, tp, d = x_prompt.shape
    bs, ts, _ = x_sample.shape
    row = lambda t: t.reshape(1, -1)
    l = 0

    w_in_p = jnp.concatenate([_rwkv_layout(w_in[l][:, :RWKV_PROJ]), w_in[l][:, RWKV_PROJ:]],
                             axis=1).astype(BF16)
    rwkv_wts = (_rwkv_layout(row(mu_shift[l])), row(decay_w0[l]),
                _pad_rows(decay_up[l], LANES).astype(BF16), row(iclr_a0[l]),
                _pad_rows(iclr_up[l], LANES).astype(BF16),
                _pad_rows(gate_up[l], 2 * LANES).astype(BF16),
                row(k_k[l]), row(k_a[l]), row(r_k[l]), row(lnx_g[l]), row(lnx_b[l]))
    w_out_a = w_out[l][:A_W].astype(BF16)
    w_out_b = w_out[l][A_W:].astype(BF16)
    w_router = _pad_cols(jnp.concatenate(
        [router_coarse[l], jnp.transpose(router_fine[l], (1, 0, 2)).reshape(d, N_EXPERTS)], axis=1),
        LANES).astype(BF16)

    groups = (
        (x_prompt.reshape(bp * tp, d), bp, tp, WKV_CHUNK, RET_CHUNK, 1, 0.0,
         jnp.zeros((bp, RWKV_PAD), F32), jnp.zeros((bp, A_HEADS, A_HD, A_HD), F32),
         jnp.zeros((bp, R_HEADS, R_HD, R_HD), F32)),
        (x_sample.reshape(bs * ts, d), bs, ts, ts, math.gcd(ts, RET_CHUNK), 8, float(PAST_LEN),
         _rwkv_layout(state_rwkv_shift[l]), state_rwkv_wkv[l], state_retention[l]),
    )
    x_last = jnp.concatenate([x_prompt[:, -1, :], x_sample[:, -1, :]], axis=0)
    n_last = -(-(bp + bs) // 8) * 8
    shifts = _rwkv_unlayout(_inproj(_pad_rows(x_last, n_last), row(norm_mix_g[l]), w_in_p, F32,
                                    RWKV_PAD, n_last, RWKV_PAD // 2))
    new_shifts = (shifts[:bp], shifts[bp:bp + bs])

    mixed = []
    for (x, b, t, wkv_chunk, ret_chunk, nb, pos0, shift0, wkv0, ret0), new_shift in zip(groups, new_shifts):
        proj = _inproj(x, row(norm_mix_g[l]), w_in_p, BF16, PROJ_PAD, 1024, 1536)
        y_a, new_wkv = _rwkv(proj, shift0, wkv0, rwkv_wts, b, t, wkv_chunk, nb,
                             WKV_SUB if nb == 1 else 1)
        y_b, new_ret = _retention(proj, ret0, b, t, ret_chunk, nb, pos0)
        x1, h2, ri, rg = _outproj(x, y_a, y_b, w_out_a, w_out_b, row(norm_ffn_g[l]), w_router)
        mixed.append((x1, h2, ri, rg, new_shift, new_wkv, new_ret))

    n_tok = [m[0].shape[0] for m in mixed]
    n_all = sum(n_tok)
    n_blk = -(-2 * n_all // MOE_BLK) + N_EXPERTS
    dest, info = _sort(jnp.concatenate([m[2] for m in mixed], axis=0))
    n_blocks, first_block = info[0, :N_EXPERTS], info[1, :N_EXPERTS]
    xs = jnp.zeros((n_blk * MOE_BLK, d // 2), jnp.uint32)
    off = 0
    dests = []
    for m, n in zip(mixed, n_tok):
        d1, d2 = dest[off:off + n, 0], dest[off:off + n, 1]
        dests.append((d1, d2))
        xs = _dispatch(d1, d2, m[1], xs)
        off += n
    yb = _experts(n_blocks, first_block, xs, exp_w_gate[l], exp_w_up[l], exp_w_down[l])
    outs = [_combine(d1, d2, m[0], m[3], row(norm_final_g), yb)
            for m, (d1, d2) in zip(mixed, dests)]

    (_, _, _, _, p_shift, p_wkv, p_ret), (_, _, _, _, s_shift, s_wkv, s_ret) = mixed
    return (outs[0].reshape(bp, tp, d), outs[1].reshape(bs, ts, d),
            p_shift[None], p_wkv[None], p_ret[None], s_shift[None], s_wkv[None], s_ret[None])
```

```python
import functools
import math

import jax
import jax.numpy as jnp
from jax import lax
from jax.experimental import pallas as pl
from jax.experimental.pallas import tpu as pltpu

F32 = jnp.float32
BF16 = jnp.bfloat16

D_MODEL = 2048
A_W = 1024
A_HD = 64
A_HEADS = 16
W_LORA = 64
A_LORA = 64
G_LORA = 160
RWKV_PROJ = 3 * A_W + W_LORA + A_LORA + G_LORA
R_W = 1024
R_HEADS = 8
R_HD = 128
RET_CHUNK = 128
N_GROUPS = 4
E_PER_GROUP = 8
N_EXPERTS = 32
D_EXPERT = 512
NORM_EPS = 1e-6
LNX_EPS = 64e-5
DECAY_SCALE = math.exp(-0.5)
PAST_LEN = 16384

LANES = 128
OFF_WD = 3 * A_W
OFF_AD = OFF_WD + LANES
OFF_GD = OFF_AD + LANES
RWKV_PAD = OFF_GD + 2 * LANES
PROJ_PAD = RWKV_PAD + 4 * R_W
RET_BLK = 512
WKV_CHUNK = 64
WKV_SUB = 4
MOE_BLK = 256
W_AHEAD = 2
W_DMA_PRIORITY = 1
VMEM_LIMIT = 56 * 1024 * 1024


def _cparams(sem):
    return pltpu.CompilerParams(dimension_semantics=sem, vmem_limit_bytes=VMEM_LIMIT)


def _bdot(a, b):
    return jnp.dot(a.astype(BF16), b.astype(BF16), preferred_element_type=F32)


def _bdot_nt(a, b):
    return lax.dot_general(a.astype(BF16), b.astype(BF16), (((1,), (1,)), ((), ())),
                           preferred_element_type=F32)


def _bdot_tn(a, b):
    return lax.dot_general(a.astype(BF16), b.astype(BF16), (((0,), (0,)), ((), ())),
                           preferred_element_type=F32)


def _split3(x):
    hi = x.astype(BF16)
    r1 = x - hi.astype(F32)
    mid = r1.astype(BF16)
    lo = (r1 - mid.astype(F32)).astype(BF16)
    return hi, mid, lo


def _dot3(x, w_bf16):
    hi, mid, lo = _split3(x)
    d = lambda p: jnp.dot(p, w_bf16, preferred_element_type=F32)
    return d(hi) + d(mid) + d(lo)


def _sigmoid(x):
    return 1.0 / (1.0 + jnp.exp(-x))


def _pack_bf16_pairs(x):
    w = x.shape[1] // 2
    lo = lax.bitcast_convert_type(x[:, :w].astype(BF16).astype(F32), jnp.uint32)
    hi = lax.bitcast_convert_type(x[:, w:].astype(BF16).astype(F32), jnp.uint32)
    return hi | (lo >> 16)


def _unpack_bf16_pairs(p):
    lo = lax.bitcast_convert_type(p << 16, F32)
    hi = lax.bitcast_convert_type(p & jnp.uint32(0xFFFF0000), F32)
    return lo, hi


def _inproj_kernel(x_ref, g_ref, w_ref, o_ref, h_ref):
    @pl.when(pl.program_id(1) == 0)
    def _():
        x = x_ref[...]
        ms = jnp.mean(x * x, axis=-1, keepdims=True)
        h_ref[...] = (x * lax.rsqrt(ms + NORM_EPS) * g_ref[...]).astype(BF16)

    o_ref[...] = jnp.dot(h_ref[...], w_ref[...], preferred_element_type=F32).astype(o_ref.dtype)


def _inproj(x, g, w_bf16, out_dtype, n_cols, tm, tn):
    n, d = x.shape
    return pl.pallas_call(
        _inproj_kernel,
        grid=(n // tm, n_cols // tn),
        in_specs=[pl.BlockSpec((tm, d), lambda i, j: (i, 0)),
                  pl.BlockSpec((1, d), lambda i, j: (0, 0)),
                  pl.BlockSpec((d, tn), lambda i, j: (0, j))],
        out_specs=pl.BlockSpec((tm, tn), lambda i, j: (i, j)),
        out_shape=jax.ShapeDtypeStruct((n, n_cols), out_dtype),
        scratch_shapes=[pltpu.VMEM((tm, d), BF16)],
        compiler_params=_cparams(("parallel", "arbitrary")),
        name="inproj",
    )(x, g, w_bf16)


def _seg_sum(x):
    ones_bd = (lax.broadcasted_iota(jnp.int32, (LANES, LANES), 0) // A_HD ==
               lax.broadcasted_iota(jnp.int32, (LANES, LANES), 1) // A_HD).astype(BF16)
    nslab = x.shape[1] // LANES
    stacked = jnp.concatenate([x[:, s * LANES:(s + 1) * LANES] for s in range(nslab)], axis=0)
    red = _dot3(stacked, ones_bd)
    c = x.shape[0]
    return jnp.concatenate([red[s * c:(s + 1) * c] for s in range(nslab)], axis=1)


def _rwkv_pre(p, before, mu_ref, w0_ref, wup_ref, a0_ref, aup_ref, gup_ref, kk_ref, ka_ref, rk_ref,
              *, chunk, nb):
    C = chunk
    R = nb * C
    row = lax.broadcasted_iota(jnp.int32, (R, 1), 0)
    pos = row % C
    first = jnp.broadcast_to(before, (nb, C, RWKV_PAD)).reshape(R, RWKV_PAD)
    prev = jnp.where(pos == 0, first, pltpu.roll(p, 1, axis=0))
    xm = p + (prev - p) * mu_ref[...]

    r = xm[:, 0:A_W]
    k = xm[:, A_W:2 * A_W]
    v = xm[:, 2 * A_W:3 * A_W]
    wd = xm[:, OFF_WD:OFF_AD]
    ad = xm[:, OFF_AD:OFF_GD]
    gd = xm[:, OFF_GD:RWKV_PAD]

    logw = -DECAY_SCALE * _sigmoid(w0_ref[...] + _bdot(jnp.tanh(wd), wup_ref[...]))
    a = _sigmoid(a0_ref[...] + _bdot(ad, aup_ref[...]))
    g = _bdot(_sigmoid(gd), gup_ref[...])

    kk = k * kk_ref[...]
    k2 = k * (1.0 + (a - 1.0) * ka_ref[...])
    sums = _seg_sum(jnp.concatenate([kk * kk, r * k2 * rk_ref[...]], axis=0))
    kk = kk / jnp.maximum(jnp.sqrt(sums[0:R]), 1e-12)
    bonus = sums[R:2 * R]
    b = kk * a

    cum = logw
    suf = logw
    d = 1
    while d < C:
        cum = cum + jnp.where(pos >= d, pltpu.roll(cum, d, axis=0), 0.0)
        suf = suf + jnp.where(pos < C - d, pltpu.roll(suf, R - d, axis=0), 0.0)
        d *= 2
    after = suf - logw
    w_incl = jnp.exp(cum)
    w_excl = jnp.exp(cum - logw)
    w_inv = jnp.exp(-cum)
    w_end = jnp.exp(after)
    w_all = jnp.exp(cum + after)

    kk_d = kk * w_excl
    r_d = r * w_incl
    b_h = b * w_inv
    k_h = k2 * w_inv
    b_e = b * w_end
    k_e = k2 * w_end
    return kk_d, r_d, b_h, k_h, b_e, k_e, v, w_all, bonus, g


def _rwkv_mix(pre, sbd_ref, lng_ref, lnb_ref, *, chunk, nb):
    kk_d, r_d, b_h, k_h, b_e, k_e, v, w_all, bonus, g = pre
    C = chunk
    R = nb * C
    pairs = range(A_HEADS // 2)
    seqs = range(nb)
    in_a = lax.broadcasted_iota(jnp.int32, (1, LANES), 1) < A_HD
    R2 = 2 * R
    ri = lax.broadcasted_iota(jnp.int32, (R2, R2), 0)
    ci = lax.broadcasted_iota(jnp.int32, (R2, R2), 1)
    same = (ri // C) == (ci // C)
    strict = same & (ci < ri)
    incl = same & (ci <= ri)
    eye = (ri == ci).astype(F32)

    def expand(x):
        return jnp.concatenate([jnp.where(in_a, x, 0.0), jnp.where(in_a, 0.0, x)], axis=0)

    def seq_rows(x, s):
        if nb == 1:
            return x
        return jnp.concatenate([x[s * C:(s + 1) * C], x[R + s * C:R + (s + 1) * C]], axis=0)

    def from_seqs(parts):
        if nb == 1:
            return parts[0]
        return jnp.concatenate([q[0:C] for q in parts] + [q[C:2 * C] for q in parts], axis=0)

    sl = [slice(pr * LANES, (pr + 1) * LANES) for pr in pairs]
    ex = lambda x: [expand(x[:, sl[pr]]) for pr in pairs]
    e_kk, e_r, e_bh, e_kh, e_be, e_ke = ex(kk_d), ex(r_d), ex(b_h), ex(k_h), ex(b_e), ex(k_e)
    v_e = [t.astype(BF16) for t in ex(v)]
    kk_bf = [t.astype(BF16) for t in e_kk]
    r_bf = [t.astype(BF16) for t in e_r]
    gram = [_bdot_nt(jnp.concatenate([kk_bf[pr], r_bf[pr]], axis=0),
                     jnp.concatenate([e_bh[pr], e_kh[pr]], axis=0)) for pr in pairs]
    l_b = [jnp.where(strict, gram[pr][0:R2, 0:R2], 0.0) for pr in pairs]
    l_k = [jnp.where(strict, gram[pr][0:R2, R2:2 * R2], 0.0) for pr in pairs]
    a_bk = [jnp.concatenate([jnp.where(incl, gram[pr][R2:2 * R2, 0:R2], 0.0),
                             jnp.where(incl, gram[pr][R2:2 * R2, R2:2 * R2], 0.0)],
                            axis=1).astype(BF16) for pr in pairs]
    t_inv = [eye - l_b[pr] for pr in pairs]
    pw = [t.astype(BF16) for t in l_b]
    e = 2
    while e < C:
        pw = [_bdot(pw[pr], pw[pr]).astype(BF16) for pr in pairs]
        t_inv = [t_inv[pr] + _bdot(t_inv[pr], pw[pr]) for pr in pairs]
        e *= 2
    lkv = [_bdot(l_k[pr], v_e[pr]) for pr in pairs]
    xs_kk, xs_r = [], []
    for pr in pairs:
        parts = [_bdot_nt(jnp.concatenate([seq_rows(kk_bf[pr], s), seq_rows(r_bf[pr], s)], axis=0),
                          sbd_ref[s, pr]) for s in seqs]
        xs_kk.append(from_seqs([q[0:2 * C] for q in parts]))
        xs_r.append(from_seqs([q[2 * C:4 * C] for q in parts]))
    u_e = [-_bdot(t_inv[pr], xs_kk[pr] + lkv[pr]).astype(BF16) for pr in pairs]
    uv = [jnp.concatenate([u_e[pr], v_e[pr]], axis=0) for pr in pairs]
    y_e = [xs_r[pr] + _bdot(a_bk[pr], uv[pr]) for pr in pairs]
    y = jnp.concatenate([y_e[pr][0:R] + y_e[pr][R:R2] for pr in pairs], axis=1)
    for pr in pairs:
        be_bf, ke_bf = e_be[pr].astype(BF16), e_ke[pr].astype(BF16)
        for s in seqs:
            upd = _bdot_tn(jnp.concatenate([seq_rows(u_e[pr], s), seq_rows(v_e[pr], s)], axis=0),
                           jnp.concatenate([seq_rows(be_bf, s), seq_rows(ke_bf, s)], axis=0))
            sbd_ref[s, pr] = sbd_ref[s, pr] * w_all[s * C:s * C + 1, sl[pr]] + upd

    inv_hd = 1.0 / A_HD
    mean = _seg_sum(y) * inv_hd
    yc = y - mean
    var = _seg_sum(yc * yc) * inv_hd
    yn = yc * lax.rsqrt(var + LNX_EPS) * lng_ref[...] + lnb_ref[...]
    yn = yn + bonus * v
    return yn * g


def _rwkv_kernel(p_ref, shift_ref, s0_ref, *rest, chunk, nb, sub):
    wrefs, (y_ref, sout_ref, sbd_ref, carry_ref) = rest[:-4], rest[-4:]
    c_idx = pl.program_id(1)
    pairs = range(A_HEADS // 2)
    rows = nb * chunk

    @pl.when(c_idx == 0)
    def _():
        carry_ref[...] = shift_ref[...]
        z = jnp.zeros((A_HD, A_HD), F32)
        for s in range(nb):
            for pr in pairs:
                sa = s0_ref[s, 2 * pr]
                sb = s0_ref[s, 2 * pr + 1]
                sbd_ref[s, pr] = jnp.concatenate(
                    [jnp.concatenate([sa, z], axis=1), jnp.concatenate([z, sb], axis=1)], axis=0)

    p_all = p_ref[...].astype(F32)
    chunks = [p_all[i * rows:(i + 1) * rows] for i in range(sub)]
    befores = [carry_ref[...]] + [c[rows - 1:rows, :].reshape(1, 1, RWKV_PAD) for c in chunks]
    if nb == 1:
        carry_ref[...] = befores[sub]
    pre = [_rwkv_pre(chunks[i], befores[i], *wrefs[:-2], chunk=chunk, nb=nb) for i in range(sub)]
    for i in range(sub):
        y = _rwkv_mix(pre[i], sbd_ref, *wrefs[-2:], chunk=chunk, nb=nb)
        y_ref[i * rows:(i + 1) * rows, :] = y.astype(y_ref.dtype)

    @pl.when(c_idx == pl.num_programs(1) - 1)
    def _():
        for s in range(nb):
            for pr in pairs:
                s_bd = sbd_ref[s, pr]
                sout_ref[s, 2 * pr] = s_bd[0:A_HD, 0:A_HD]
                sout_ref[s, 2 * pr + 1] = s_bd[A_HD:2 * A_HD, A_HD:2 * A_HD]


def _rwkv(proj, shift0, s0, wts, batch, seq, chunk, nb, sub):
    n_chunks = seq // (chunk * sub)
    assert nb == 1 or (n_chunks == 1 and sub == 1)
    rows = nb * chunk * sub
    vec = lambda w: pl.BlockSpec(w.shape, lambda b, c: (0,) * w.ndim)
    st = pl.BlockSpec((nb, A_HEADS, A_HD, A_HD), lambda b, c: (b, 0, 0, 0))
    return pl.pallas_call(
        functools.partial(_rwkv_kernel, chunk=chunk, nb=nb, sub=sub),
        grid=(batch // nb, n_chunks),
        in_specs=[pl.BlockSpec((rows, RWKV_PAD), lambda b, c: (b * n_chunks + c, 0)),
                  pl.BlockSpec((nb, 1, RWKV_PAD), lambda b, c: (b, 0, 0)), st]
                 + [vec(w) for w in wts],
        out_specs=[pl.BlockSpec((rows, A_W), lambda b, c: (b * n_chunks + c, 0)), st],
        out_shape=[jax.ShapeDtypeStruct((batch * seq, A_W), BF16),
                   jax.ShapeDtypeStruct((batch, A_HEADS, A_HD, A_HD), F32)],
        scratch_shapes=[pltpu.VMEM((nb, A_HEADS // 2, LANES, LANES), F32),
                        pltpu.VMEM((nb, 1, RWKV_PAD), F32)],
        compiler_params=_cparams(("parallel", "arbitrary")),
        name="rwkv7",
    )(proj, shift0.reshape(batch, 1, RWKV_PAD), s0, *wts)


def _ret_kernel(q0_ref, q1_ref, k0_ref, k1_ref, v0_ref, v1_ref, g0_ref, g1_ref, cos_ref, sin_ref,
                din_ref, dq_ref, dk_ref, dc_ref, r0_ref, y_ref, rout_ref, r_scr, *, chunk, nb):
    C = chunk
    half = RET_BLK // R_HD
    c_idx = pl.program_id(1)

    @pl.when(c_idx == 0)
    def _():
        r_scr[...] = r0_ref[...]

    even = lax.broadcasted_iota(jnp.int32, (1, R_HD), 1) % 2 == 0
    cos = cos_ref[...]
    sin = sin_ref[...]

    def rot(t):
        partner = jnp.where(even, pltpu.roll(t, R_HD - 1, axis=1), pltpu.roll(t, 1, axis=1))
        return t * cos + partner * sin

    def load(halves, s, h):
        return halves[h // half][s * C:(s + 1) * C, (h % half) * R_HD:(h % half + 1) * R_HD]

    f32 = lambda *refs: tuple(r[...].astype(F32) for r in refs)
    items = [(s, h) for s in range(nb) for h in range(R_HEADS)]
    q_in, k_in, v_in, g_in = f32(q0_ref, q1_ref), f32(k0_ref, k1_ref), f32(v0_ref, v1_ref), f32(g0_ref, g1_ref)
    q = [rot(load(q_in, s, h)) for s, h in items]
    k = [rot(load(k_in, s, h)) * (R_HD ** -0.5) for s, h in items]
    v = [load(v_in, s, h).astype(BF16) for s, h in items]
    state = [r_scr[s, h] for s, h in items]
    scores = [_bdot_nt(q[i], k[i]) * din_ref[h] for i, (s, h) in enumerate(items)]
    o = [_bdot(jnp.concatenate([scores[i], q[i] * dq_ref[h]], axis=1),
               jnp.concatenate([v[i], state[i].astype(BF16)], axis=0))
         for i, (s, h) in enumerate(items)]
    for i, (s, h) in enumerate(items):
        r_scr[s, h] = state[i] * dc_ref[h, 0:1, :] + _bdot_tn(k[i] * dk_ref[h], v[i])
    for i, (s, h) in enumerate(items):
        on = o[i] * lax.rsqrt(jnp.mean(o[i] * o[i], axis=-1, keepdims=True) + NORM_EPS)
        gate = load(g_in, s, h)
        y_ref[s * C:(s + 1) * C, h * R_HD:(h + 1) * R_HD] = (
            gate * _sigmoid(gate) * on).astype(y_ref.dtype)

    @pl.when(c_idx == pl.num_programs(1) - 1)
    def _():
        rout_ref[...] = r_scr[...]


def _ret_tables(chunk, pos0, seq):
    pos = pos0 + jnp.arange(seq, dtype=F32)
    theta = 1.0 / (10000.0 ** jnp.linspace(0.0, 1.0, R_HD // 2, dtype=F32))
    ang = pos[:, None] * theta[None, :]
    cos2 = jnp.repeat(jnp.cos(ang), 2, axis=1)
    sin2 = jnp.stack([-jnp.sin(ang), jnp.sin(ang)], axis=-1).reshape(seq, R_HD)
    lg = jnp.log1p(-(2.0 ** (-5.0 - jnp.arange(R_HEADS, dtype=F32))))
    idx = jnp.arange(chunk, dtype=F32)
    diff = idx[:, None] - idx[None, :]
    causal = diff >= 0
    din = jnp.where(causal, jnp.exp(jnp.where(causal, diff, 0.0)[None] * lg[:, None, None]), 0.0)
    bc = lambda t: jnp.broadcast_to(t[:, :, None], (R_HEADS, t.shape[1], R_HD))
    dq = bc(jnp.exp((idx + 1.0)[None, :] * lg[:, None]))
    dk = bc(jnp.exp((chunk - 1.0 - idx)[None, :] * lg[:, None]))
    dc = jnp.broadcast_to(jnp.exp(chunk * lg)[:, None, None], (R_HEADS, 8, R_HD))
    return cos2, sin2, din, dq, dk, dc


def _retention(proj, r0, batch, seq, chunk, nb, pos0):
    n_chunks = seq // chunk
    cos2, sin2, din, dq, dk, dc = _ret_tables(chunk, pos0, seq)
    col0 = RWKV_PAD // RET_BLK
    rows = nb * chunk

    def sec(i):
        return pl.BlockSpec((rows, RET_BLK), lambda b, c: (b * n_chunks + c, col0 + i))

    tab = lambda t: pl.BlockSpec(t.shape, lambda b, c: (0, 0, 0))
    st = pl.BlockSpec((nb, R_HEADS, R_HD, R_HD), lambda b, c: (b, 0, 0, 0))
    return pl.pallas_call(
        functools.partial(_ret_kernel, chunk=chunk, nb=nb),
        grid=(batch // nb, n_chunks),
        in_specs=[sec(i) for i in range(8)]
                 + [pl.BlockSpec((chunk, R_HD), lambda b, c: (c, 0)),
                    pl.BlockSpec((chunk, R_HD), lambda b, c: (c, 0)),
                    tab(din), tab(dq), tab(dk), tab(dc), st],
        out_specs=[pl.BlockSpec((rows, R_W), lambda b, c: (b * n_chunks + c, 0)), st],
        out_shape=[jax.ShapeDtypeStruct((batch * seq, R_W), BF16),
                   jax.ShapeDtypeStruct((batch, R_HEADS, R_HD, R_HD), F32)],
        scratch_shapes=[pltpu.VMEM((nb, R_HEADS, R_HD, R_HD), F32)],
        compiler_params=_cparams(("parallel", "arbitrary")),
        name="retention",
    )(*([proj] * 8), cos2, sin2, din, dq, dk, dc, r0)


def _outproj_kernel(x_ref, ya_ref, yb_ref, wa_ref, wb_ref, g_ref, wr_ref,
                    x1_ref, h2_ref, ri_ref, rg_ref):
    x1 = (x_ref[...] + jnp.dot(ya_ref[...], wa_ref[...], preferred_element_type=F32)
          + jnp.dot(yb_ref[...], wb_ref[...], preferred_element_type=F32))
    x1_ref[...] = x1
    ms = jnp.mean(x1 * x1, axis=-1, keepdims=True)
    h2 = x1 * lax.rsqrt(ms + NORM_EPS) * g_ref[...]
    h2_ref[...] = _pack_bf16_pairs(h2)
    logits = _bdot(h2, wr_ref[...])
    lane = lax.broadcasted_iota(jnp.int32, (1, LANES), 1)
    neg = -1e30
    big = LANES
    rmax = lambda t: jnp.max(t, axis=-1, keepdims=True)
    rmin = lambda t: jnp.min(t, axis=-1, keepdims=True)
    is_c = lane < N_GROUPS
    lc = jnp.where(is_c, logits, neg)
    mc = rmax(lc)
    grp = rmin(jnp.where(lc == mc, lane, big))
    p_grp = 1.0 / jnp.sum(jnp.where(is_c, jnp.exp(lc - mc), 0.0), axis=-1, keepdims=True)
    fine = lane - N_GROUPS
    in_g = (fine >= 0) & (fine < N_EXPERTS) & ((fine // E_PER_GROUP) == grp)
    lf = jnp.where(in_g, logits, neg)
    m1 = rmax(lf)
    i1 = rmin(jnp.where(lf == m1, lane, big))
    lf2 = jnp.where(lane == i1, neg, lf)
    m2 = rmax(lf2)
    i2 = rmin(jnp.where(lf2 == m2, lane, big))
    e2 = jnp.exp(m2 - m1)
    g1 = p_grp / (1.0 + e2)
    g2 = p_grp * e2 / (1.0 + e2)
    ri_ref[...] = jnp.where(lane == 0, i1 - N_GROUPS, jnp.where(lane == 1, i2 - N_GROUPS, 0))
    rg_ref[...] = jnp.where(lane == 0, g1, jnp.where(lane == 1, g2, 0.0))


def _outproj(x, ya, yb, wa, wb, g, wr, tm=512):
    n, d = x.shape
    rowblk = lambda w: pl.BlockSpec((tm, w), lambda i: (i, 0))
    full = lambda w: pl.BlockSpec(w.shape, lambda i: (0, 0))
    return pl.pallas_call(
        _outproj_kernel,
        grid=(n // tm,),
        in_specs=[rowblk(d), rowblk(A_W), rowblk(R_W), full(wa), full(wb), full(g), full(wr)],
        out_specs=[rowblk(d), rowblk(d // 2), rowblk(LANES), rowblk(LANES)],
        out_shape=[jax.ShapeDtypeStruct((n, d), F32), jax.ShapeDtypeStruct((n, d // 2), jnp.uint32),
                   jax.ShapeDtypeStruct((n, LANES), jnp.int32),
                   jax.ShapeDtypeStruct((n, LANES), F32)],
        compiler_params=_cparams(("parallel",)),
        name="outproj_router",
    )(x, ya, yb, wa, wb, g, wr)


def _sort_kernel(ri_ref, tri_ref, dest_ref, info_ref, cnt_ref, base_ref):
    ph = pl.program_id(0)
    i = pl.program_id(1)
    lane = lax.broadcasted_iota(jnp.int32, (1, LANES), 1)
    ri = ri_ref[...]
    oh1 = (lane == ri[:, 0:1]).astype(F32)
    oh2 = (lane == ri[:, 1:2]).astype(F32)
    tot1 = jnp.sum(oh1, axis=0, keepdims=True)
    tot2 = jnp.sum(oh2, axis=0, keepdims=True)

    @pl.when(ph == 0)
    def _():
        @pl.when(i == 0)
        def _():
            cnt_ref[...] = jnp.zeros_like(cnt_ref)
        cnt_ref[...] += tot1 + tot2

    @pl.when(ph == 1)
    def _():
        @pl.when(i == 0)
        def _():
            nblk = jnp.floor((cnt_ref[...] + (MOE_BLK - 1)) * (1.0 / MOE_BLK))
            upper = (lax.broadcasted_iota(jnp.int32, (LANES, LANES), 0) <=
                     lax.broadcasted_iota(jnp.int32, (LANES, LANES), 1))
            pend = _bdot(jnp.broadcast_to(nblk, (8, LANES)), upper.astype(F32))[0:1]
            first = pend - nblk
            base_ref[...] = first * MOE_BLK
            srow = lax.broadcasted_iota(jnp.int32, (8, LANES), 0)
            info_ref[...] = jnp.where(srow == 0, nblk, jnp.where(srow == 1, first, 0.0)).astype(jnp.int32)

        base = base_ref[...]
        tri = tri_ref[...]
        pre1 = jnp.dot(tri, oh1.astype(BF16), preferred_element_type=F32)
        pre2 = jnp.dot(tri, oh2.astype(BF16), preferred_element_type=F32) + tot1
        d1 = jnp.sum(oh1 * (base + pre1), axis=1, keepdims=True)
        d2 = jnp.sum(oh2 * (base + pre2), axis=1, keepdims=True)
        dest_ref[...] = jnp.where(lane == 0, d1, jnp.where(lane == 1, d2, 0.0)).astype(jnp.int32)
        base_ref[...] = base + tot1 + tot2


def _sort(ri, tt=512):
    n = ri.shape[0]
    tri = (jnp.arange(tt)[:, None] > jnp.arange(tt)[None, :]).astype(BF16)
    return pl.pallas_call(
        _sort_kernel,
        grid=(2, n // tt),
        in_specs=[pl.BlockSpec((tt, LANES), lambda ph, i: (i, 0)),
                  pl.BlockSpec((tt, tt), lambda ph, i: (0, 0))],
        out_specs=[pl.BlockSpec((tt, LANES), lambda ph, i: (i * ph, 0)),
                   pl.BlockSpec((8, LANES), lambda ph, i: (0, 0))],
        out_shape=[jax.ShapeDtypeStruct((n, LANES), jnp.int32),
                   jax.ShapeDtypeStruct((8, LANES), jnp.int32)],
        scratch_shapes=[pltpu.VMEM((1, LANES), F32), pltpu.VMEM((1, LANES), F32)],
        compiler_params=_cparams(("arbitrary", "arbitrary")),
        name="moe_sort",
    )(ri, tri)


def _row_copy(src_ref, src_row, dst_ref, dst_row, sem):
    return pltpu.make_async_copy(src_ref.at[pl.ds(src_row, 1)], dst_ref.at[pl.ds(dst_row, 1)], sem)


def _dispatch_kernel(d1_ref, d2_ref, h_ref, xs_in_ref, xs_ref, sem, *, tt):
    del xs_in_ref

    def issue(t, carry):
        _row_copy(h_ref, t, xs_ref, d1_ref[t], sem).start()
        _row_copy(h_ref, t, xs_ref, d2_ref[t], sem).start(priority=1)
        return carry

    def drain(t, carry):
        _row_copy(h_ref, t, xs_ref, d1_ref[t], sem).wait()
        _row_copy(h_ref, t, xs_ref, d2_ref[t], sem).wait()
        return carry

    lax.fori_loop(0, tt, issue, 0, unroll=8)
    lax.fori_loop(0, tt, drain, 0, unroll=8)


def _dispatch(d1, d2, h2, xs, tt=512):
    n, d = h2.shape
    smem = pl.BlockSpec((tt,), lambda i: (i,), memory_space=pltpu.SMEM)
    anyspec = pl.BlockSpec(memory_space=pl.ANY)
    return pl.pallas_call(
        functools.partial(_dispatch_kernel, tt=tt),
        grid=(n // tt,),
        in_specs=[smem, smem, pl.BlockSpec((tt, d), lambda i: (i, 0)), anyspec],
        out_specs=anyspec,
        out_shape=jax.ShapeDtypeStruct(xs.shape, xs.dtype),
        scratch_shapes=[pltpu.SemaphoreType.DMA(())],
        input_output_aliases={3: 0},
        compiler_params=_cparams(("arbitrary",)),
        name="moe_dispatch",
    )(d1, d2, h2, xs)


def _expert_kernel(nb_ref, fb_ref, xs_hbm, wg_hbm, wu_hbm, wd_hbm, y_hbm,
                   wg_r, wu_r, wd_r, wg_s, wu_s, wd_s, xbuf, ybuf, sem_w, sem_in, sem_out, *, n_blk):
    e = pl.program_id(0)
    last = pl.num_programs(0) - 1
    n = nb_ref[e]
    b0 = fb_ref[e]
    n_used = fb_ref[last] + nb_ref[last]
    half = xbuf.shape[2]
    ring = wg_r.shape[0]

    def fetch(ex):
        slot = ex % ring
        return [pltpu.make_async_copy(src.at[ex], dst.at[slot], sem_w.at[slot])
                for src, dst in ((wg_hbm, wg_r), (wu_hbm, wu_r), (wd_hbm, wd_r))]

    @pl.when(e == 0)
    def _():
        for ex in range(ring - 1):
            for cp in fetch(ex):
                cp.start(priority=W_DMA_PRIORITY)

    @pl.when(e + ring - 1 <= last)
    def _():
        for cp in fetch(e + ring - 1):
            cp.start(priority=W_DMA_PRIORITY)

    for cp in fetch(e):
        cp.wait()
    w_slot = e % ring

    def load(blk, slot):
        return pltpu.make_async_copy(xs_hbm.at[pl.ds(blk * MOE_BLK, MOE_BLK)],
                                     xbuf.at[slot], sem_in.at[slot])

    def store(blk, slot):
        return pltpu.make_async_copy(ybuf.at[slot], y_hbm.at[pl.ds(blk * MOE_BLK, MOE_BLK)],
                                     sem_out.at[slot])

    @pl.when(n > 0)
    def _():
        @pl.when(b0 == 0)
        def _():
            load(0, 0).start()

        wg_s[...] = wg_r[w_slot].astype(BF16)
        wu_s[...] = wu_r[w_slot].astype(BF16)
        wd_s[...] = wd_r[w_slot].astype(BF16)

        def body(blk, carry):
            slot = blk % 2
            load(blk, slot).wait()

            @pl.when(blk + 1 < n_used)
            def _():
                load(blk + 1, 1 - slot).start()

            @pl.when(blk >= 2)
            def _():
                store(blk - 2, slot).wait()

            x_lo, x_hi = _unpack_bf16_pairs(xbuf[slot])
            x_lo, x_hi = x_lo.astype(BF16), x_hi.astype(BF16)
            mm = lambda w_s: (jnp.dot(x_lo, w_s[0:half], preferred_element_type=F32)
                              + jnp.dot(x_hi, w_s[half:2 * half], preferred_element_type=F32))
            hg = mm(wg_s)
            hu = mm(wu_s)
            act = (hg * _sigmoid(hg) * hu).astype(BF16)
            ybuf[slot] = _pack_bf16_pairs(jnp.dot(act, wd_s[...], preferred_element_type=F32))
            store(blk, slot).start()
            return carry

        lax.fori_loop(b0, b0 + n, body, 0)

    @pl.when(e == last)
    def _():
        @pl.when(n_used >= 2)
        def _():
            store(n_used - 2, n_used % 2).wait()

        @pl.when(n_used >= 1)
        def _():
            store(n_used - 1, (n_used - 1) % 2).wait()

        ybuf[0] = jnp.zeros(ybuf.shape[1:], ybuf.dtype)

        def fill(blk, carry):
            cp = store(blk, 0)
            cp.start()
            cp.wait()
            return carry

        lax.fori_loop(n_used, n_blk, fill, 0)


def _experts(n_blocks, first_block, xs, wg, wu, wd):
    p, half = xs.shape
    d = 2 * half
    anyspec = pl.BlockSpec(memory_space=pl.ANY)
    ring = W_AHEAD + 1
    grid_spec = pltpu.PrefetchScalarGridSpec(
        num_scalar_prefetch=2,
        grid=(N_EXPERTS,),
        in_specs=[anyspec] * 4,
        out_specs=anyspec,
        scratch_shapes=[pltpu.VMEM((ring, d, D_EXPERT), F32), pltpu.VMEM((ring, d, D_EXPERT), F32),
                        pltpu.VMEM((ring, D_EXPERT, d), F32),
                        pltpu.VMEM((d, D_EXPERT), BF16), pltpu.VMEM((d, D_EXPERT), BF16),
                        pltpu.VMEM((D_EXPERT, d), BF16),
                        pltpu.VMEM((2, MOE_BLK, half), jnp.uint32),
                        pltpu.VMEM((2, MOE_BLK, half), jnp.uint32),
                        pltpu.SemaphoreType.DMA((ring,)),
                        pltpu.SemaphoreType.DMA((2,)), pltpu.SemaphoreType.DMA((2,))])
    return pl.pallas_call(
        functools.partial(_expert_kernel, n_blk=p // MOE_BLK),
        grid_spec=grid_spec,
        out_shape=jax.ShapeDtypeStruct((p, half), jnp.uint32),
        compiler_params=_cparams(("arbitrary",)),
        name="moe_experts",
    )(n_blocks, first_block, xs, wg, wu, wd)


def _combine_kernel(d1_ref, d2_ref, d1n_ref, d2n_ref, x1_ref, rg_ref, g_ref, yb_ref, o_ref, buf, sem,
                    *, tt):
    i = pl.program_id(0)
    slot = i % 2

    def gather(a_ref, b_ref, s, wait):
        def body(t, carry):
            for j, idx in enumerate((a_ref, b_ref)):
                cp = _row_copy(yb_ref, idx[t], buf.at[s, j], t, sem.at[s])
                cp.wait() if wait else cp.start(priority=j)
            return carry
        lax.fori_loop(0, tt, body, 0, unroll=8)

    @pl.when(i == 0)
    def _():
        gather(d1_ref, d2_ref, 0, False)

    @pl.when(i + 1 < pl.num_programs(0))
    def _():
        gather(d1n_ref, d2n_ref, 1 - slot, False)

    gather(d1_ref, d2_ref, slot, True)
    rg = rg_ref[...]
    lo1, hi1 = _unpack_bf16_pairs(buf[slot, 0])
    lo2, hi2 = _unpack_bf16_pairs(buf[slot, 1])
    g1, g2 = rg[:, 0:1], rg[:, 1:2]
    x = x1_ref[...] + jnp.concatenate([g1 * lo1 + g2 * lo2, g1 * hi1 + g2 * hi2], axis=1)
    ms = jnp.mean(x * x, axis=-1, keepdims=True)
    o_ref[...] = x * lax.rsqrt(ms + NORM_EPS) * g_ref[...]


def _combine(d1, d2, x1, rg, g, yb, tt=256):
    n, d = x1.shape
    steps = n // tt
    smem = pl.BlockSpec((tt,), lambda i: (i,), memory_space=pltpu.SMEM)
    smem_next = pl.BlockSpec((tt,), lambda i: (jnp.minimum(i + 1, steps - 1),),
                             memory_space=pltpu.SMEM)
    rowblk = lambda w: pl.BlockSpec((tt, w), lambda i: (i, 0))
    return pl.pallas_call(
        functools.partial(_combine_kernel, tt=tt),
        grid=(steps,),
        in_specs=[smem, smem, smem_next, smem_next, rowblk(d), rowblk(LANES),
                  pl.BlockSpec((1, d), lambda i: (0, 0)), pl.BlockSpec(memory_space=pl.ANY)],
        out_specs=rowblk(d),
        out_shape=jax.ShapeDtypeStruct((n, d), F32),
        scratch_shapes=[pltpu.VMEM((2, 2, tt, d // 2), jnp.uint32), pltpu.SemaphoreType.DMA((2,))],
        compiler_params=_cparams(("arbitrary",)),
        name="moe_combine",
    )(d1, d2, d1, d2, x1, rg, g, yb)


def _pad_cols(x, width):
    return jnp.pad(x, ((0, 0), (0, width - x.shape[1])))


def _rwkv_layout(x):
    return jnp.concatenate([
        x[:, :OFF_WD],
        _pad_cols(x[:, 3 * A_W:3 * A_W + W_LORA], LANES),
        _pad_cols(x[:, 3 * A_W + W_LORA:3 * A_W + W_LORA + A_LORA], LANES),
        _pad_cols(x[:, 3 * A_W + W_LORA + A_LORA:RWKV_PROJ], 2 * LANES)], axis=1)


def _rwkv_unlayout(x):
    return jnp.concatenate([x[:, :OFF_WD + W_LORA], x[:, OFF_AD:OFF_AD + A_LORA],
                            x[:, OFF_GD:OFF_GD + G_LORA]], axis=1)


def _pad_rows(w, rows):
    return jnp.pad(w, ((0, rows - w.shape[0]), (0, 0)))


def kernel(x_prompt, x_sample, state_rwkv_shift, state_rwkv_wkv, state_retention, norm_mix_g, w_in, mu_shift, decay_w0, decay_up, iclr_a0, iclr_up, gate_up, k_k, k_a, r_k, lnx_g, lnx_b, w_out, norm_ffn_g, router_coarse, router_fine, exp_w_gate, exp_w_up, exp_w_down, norm_final_g):
    bp, tp, d = x_prompt.shape
    bs, ts, _ = x_sample.shape
    row = lambda t: t.reshape(1, -1)
    l = 0

    w_in_p = jnp.concatenate([_rwkv_layout(w_in[l][:, :RWKV_PROJ]), w_in[l][:, RWKV_PROJ:]],
                             axis=1).astype(BF16)
    rwkv_wts = (_rwkv_layout(row(mu_shift[l])), row(decay_w0[l]),
                _pad_rows(decay_up[l], LANES).astype(BF16), row(iclr_a0[l]),
                _pad_rows(iclr_up[l], LANES).astype(BF16),
                _pad_rows(gate_up[l], 2 * LANES).astype(BF16),
                row(k_k[l]), row(k_a[l]), row(r_k[l]), row(lnx_g[l]), row(lnx_b[l]))
    w_out_a = w_out[l][:A_W].astype(BF16)
    w_out_b = w_out[l][A_W:].astype(BF16)
    w_router = _pad_cols(jnp.concatenate(
        [router_coarse[l], jnp.transpose(router_fine[l], (1, 0, 2)).reshape(d, N_EXPERTS)], axis=1),
        LANES).astype(BF16)

    groups = (
        (x_prompt.reshape(bp * tp, d), bp, tp, WKV_CHUNK, RET_CHUNK, 1, 0.0,
         jnp.zeros((bp, RWKV_PAD), F32), jnp.zeros((bp, A_HEADS, A_HD, A_HD), F32),
         jnp.zeros((bp, R_HEADS, R_HD, R_HD), F32)),
        (x_sample.reshape(bs * ts, d), bs, ts, ts, math.gcd(ts, RET_CHUNK), 8, float(PAST_LEN),
         _rwkv_layout(state_rwkv_shift[l]), state_rwkv_wkv[l], state_retention[l]),
    )
    x_last = jnp.concatenate([x_prompt[:, -1, :], x_sample[:, -1, :]], axis=0)
    n_last = -(-(bp + bs) // 8) * 8
    shifts = _rwkv_unlayout(_inproj(_pad_rows(x_last, n_last), row(norm_mix_g[l]), w_in_p, F32,
                                    RWKV_PAD, n_last, RWKV_PAD // 2))
    new_shifts = (shifts[:bp], shifts[bp:bp + bs])

    mixed = []
    for (x, b, t, wkv_chunk, ret_chunk, nb, pos0, shift0, wkv0, ret0), new_shift in zip(groups, new_shifts):
        proj = _inproj(x, row(norm_mix_g[l]), w_in_p, BF16, PROJ_PAD, 1024, 1536)
        y_a, new_wkv = _rwkv(proj, shift0, wkv0, rwkv_wts, b, t, wkv_chunk, nb,
                             WKV_SUB if nb == 1 else 1)
        y_b, new_ret = _retention(proj, ret0, b, t, ret_chunk, nb, pos0)
        x1, h2, ri, rg = _outproj(x, y_a, y_b, w_out_a, w_out_b, row(norm_ffn_g[l]), w_router)
        mixed.append((x1, h2, ri, rg, new_shift, new_wkv, new_ret))

    n_tok = [m[0].shape[0] for m in mixed]
    n_all = sum(n_tok)
    n_blk = -(-2 * n_all // MOE_BLK) + N_EXPERTS
    dest, info = _sort(jnp.concatenate([m[2] for m in mixed], axis=0))
    n_blocks, first_block = info[0, :N_EXPERTS], info[1, :N_EXPERTS]
    xs = jnp.zeros((n_blk * MOE_BLK, d // 2), jnp.uint32)
    off = 0
    dests = []
    for m, n in zip(mixed, n_tok):
        d1, d2 = dest[off:off + n, 0], dest[off:off + n, 1]
        dests.append((d1, d2))
        xs = _dispatch(d1, d2, m[1], xs)
        off += n
    yb = _experts(n_blocks, first_block, xs, exp_w_gate[l], exp_w_up[l], exp_w_down[l])
    outs = [_combine(d1, d2, m[0], m[3], row(norm_final_g), yb)
            for m, (d1, d2) in zip(mixed, dests)]

    (_, _, _, _, p_shift, p_wkv, p_ret), (_, _, _, _, s_shift, s_wkv, s_ret) = mixed
    return (outs[0].reshape(bp, tp, d), outs[1].reshape(bs, ts, d),
            p_shift[None], p_wkv[None], p_ret[None], s_shift[None], s_wkv[None], s_ret[None])
```

```python
import functools
import math

import jax
import jax.numpy as jnp
from jax import lax
from jax.experimental import pallas as pl
from jax.experimental.pallas import tpu as pltpu

F32 = jnp.float32
BF16 = jnp.bfloat16

D_MODEL = 2048
A_W = 1024
A_HD = 64
A_HEADS = 16
W_LORA = 64
A_LORA = 64
G_LORA = 160
RWKV_PROJ = 3 * A_W + W_LORA + A_LORA + G_LORA
R_W = 1024
R_HEADS = 8
R_HD = 128
RET_CHUNK = 128
N_GROUPS = 4
E_PER_GROUP = 8
N_EXPERTS = 32
D_EXPERT = 512
NORM_EPS = 1e-6
LNX_EPS = 64e-5
DECAY_SCALE = math.exp(-0.5)
PAST_LEN = 16384

LANES = 128
OFF_WD = 3 * A_W
OFF_AD = OFF_WD + LANES
OFF_GD = OFF_AD + LANES
RWKV_PAD = OFF_GD + 2 * LANES
PROJ_PAD = RWKV_PAD + 4 * R_W
RET_BLK = 512
WKV_CHUNK = 64
WKV_SUB = 4
MOE_BLK = 256
W_AHEAD = 2
W_DMA_PRIORITY = 1
VMEM_LIMIT = 56 * 1024 * 1024


def _cparams(sem):
    return pltpu.CompilerParams(dimension_semantics=sem, vmem_limit_bytes=VMEM_LIMIT)


def _bdot(a, b):
    return jnp.dot(a.astype(BF16), b.astype(BF16), preferred_element_type=F32)


def _bdot_nt(a, b):
    return lax.dot_general(a.astype(BF16), b.astype(BF16), (((1,), (1,)), ((), ())),
                           preferred_element_type=F32)


def _bdot_tn(a, b):
    return lax.dot_general(a.astype(BF16), b.astype(BF16), (((0,), (0,)), ((), ())),
                           preferred_element_type=F32)


def _split3(x):
    hi = x.astype(BF16)
    r1 = x - hi.astype(F32)
    mid = r1.astype(BF16)
    lo = (r1 - mid.astype(F32)).astype(BF16)
    return hi, mid, lo


def _dot3(x, w_bf16):
    hi, mid, lo = _split3(x)
    d = lambda p: jnp.dot(p, w_bf16, preferred_element_type=F32)
    return d(hi) + d(mid) + d(lo)


def _sigmoid(x):
    return 1.0 / (1.0 + jnp.exp(-x))


def _pack_bf16_pairs(x):
    w = x.shape[1] // 2
    lo = lax.bitcast_convert_type(x[:, :w].astype(BF16).astype(F32), jnp.uint32)
    hi = lax.bitcast_convert_type(x[:, w:].astype(BF16).astype(F32), jnp.uint32)
    return hi | (lo >> 16)


def _unpack_bf16_pairs(p):
    lo = lax.bitcast_convert_type(p << 16, F32)
    hi = lax.bitcast_convert_type(p & jnp.uint32(0xFFFF0000), F32)
    return lo, hi


def _w_in_layout_kernel(w_ref, o_ref):
    w = w_ref[...]
    rows = w.shape[0]
    zeros = lambda n: jnp.zeros((rows, n), F32)
    ad0 = 3 * A_W + W_LORA
    gd0 = ad0 + A_LORA
    o_ref[...] = jnp.concatenate(
        [w[:, :ad0], zeros(LANES - W_LORA),
         w[:, ad0:gd0], zeros(LANES - A_LORA),
         w[:, gd0:RWKV_PROJ], zeros(2 * LANES - G_LORA),
         w[:, RWKV_PROJ:]], axis=1).astype(o_ref.dtype)


def _w_in_layout(w, tr=256):
    d, cols = w.shape
    return pl.pallas_call(
        _w_in_layout_kernel,
        grid=(d // tr,),
        in_specs=[pl.BlockSpec((tr, cols), lambda i: (i, 0))],
        out_specs=pl.BlockSpec((tr, PROJ_PAD), lambda i: (i, 0)),
        out_shape=jax.ShapeDtypeStruct((d, PROJ_PAD), BF16),
        compiler_params=_cparams(("parallel",)),
        name="w_in_layout",
    )(w)


def _inproj_kernel(x_ref, g_ref, w_ref, o_ref, h_ref):
    @pl.when(pl.program_id(1) == 0)
    def _():
        x = x_ref[...]
        ms = jnp.mean(x * x, axis=-1, keepdims=True)
        h_ref[...] = (x * lax.rsqrt(ms + NORM_EPS) * g_ref[...]).astype(BF16)

    o_ref[...] = jnp.dot(h_ref[...], w_ref[...], preferred_element_type=F32).astype(o_ref.dtype)


def _inproj(x, g, w_bf16, out_dtype, n_cols, tm, tn):
    n, d = x.shape
    return pl.pallas_call(
        _inproj_kernel,
        grid=(n // tm, n_cols // tn),
        in_specs=[pl.BlockSpec((tm, d), lambda i, j: (i, 0)),
                  pl.BlockSpec((1, d), lambda i, j: (0, 0)),
                  pl.BlockSpec((d, tn), lambda i, j: (0, j))],
        out_specs=pl.BlockSpec((tm, tn), lambda i, j: (i, j)),
        out_shape=jax.ShapeDtypeStruct((n, n_cols), out_dtype),
        scratch_shapes=[pltpu.VMEM((tm, d), BF16)],
        compiler_params=_cparams(("parallel", "arbitrary")),
        name="inproj",
    )(x, g, w_bf16)


def _seg_sum(x):
    ones_bd = (lax.broadcasted_iota(jnp.int32, (LANES, LANES), 0) // A_HD ==
               lax.broadcasted_iota(jnp.int32, (LANES, LANES), 1) // A_HD).astype(BF16)
    nslab = x.shape[1] // LANES
    stacked = jnp.concatenate([x[:, s * LANES:(s + 1) * LANES] for s in range(nslab)], axis=0)
    red = _dot3(stacked, ones_bd)
    c = x.shape[0]
    return jnp.concatenate([red[s * c:(s + 1) * c] for s in range(nslab)], axis=1)


def _rwkv_pre(p, before, mu_ref, w0_ref, wup_ref, a0_ref, aup_ref, gup_ref, kk_ref, ka_ref, rk_ref,
              *, chunk, nb):
    C = chunk
    R = nb * C
    row = lax.broadcasted_iota(jnp.int32, (R, 1), 0)
    pos = row % C
    first = jnp.broadcast_to(before, (nb, C, RWKV_PAD)).reshape(R, RWKV_PAD)
    prev = jnp.where(pos == 0, first, pltpu.roll(p, 1, axis=0))
    xm = p + (prev - p) * mu_ref[...]

    r = xm[:, 0:A_W]
    k = xm[:, A_W:2 * A_W]
    v = xm[:, 2 * A_W:3 * A_W]
    wd = xm[:, OFF_WD:OFF_AD]
    ad = xm[:, OFF_AD:OFF_GD]
    gd = xm[:, OFF_GD:RWKV_PAD]

    logw = -DECAY_SCALE * _sigmoid(w0_ref[...] + _bdot(jnp.tanh(wd), wup_ref[...]))
    a = _sigmoid(a0_ref[...] + _bdot(ad, aup_ref[...]))
    g = _bdot(_sigmoid(gd), gup_ref[...])

    kk = k * kk_ref[...]
    k2 = k * (1.0 + (a - 1.0) * ka_ref[...])
    sums = _seg_sum(jnp.concatenate([kk * kk, r * k2 * rk_ref[...]], axis=0))
    kk = kk / jnp.maximum(jnp.sqrt(sums[0:R]), 1e-12)
    bonus = sums[R:2 * R]
    b = kk * a

    cum = logw
    suf = logw
    d = 1
    while d < C:
        cum = cum + jnp.where(pos >= d, pltpu.roll(cum, d, axis=0), 0.0)
        if nb > 1:
            suf = suf + jnp.where(pos < C - d, pltpu.roll(suf, R - d, axis=0), 0.0)
        d *= 2
    after = suf - logw if nb > 1 else cum[C - 1:C, :] - cum
    w_incl = jnp.exp(cum)
    w_excl = jnp.exp(cum - logw)
    w_inv = jnp.exp(-cum)
    w_end = jnp.exp(after)
    w_all = jnp.exp(cum + after)

    kk_d = kk * w_excl
    r_d = r * w_incl
    b_h = b * w_inv
    k_h = k2 * w_inv
    b_e = b * w_end
    k_e = k2 * w_end
    return kk_d, r_d, b_h, k_h, b_e, k_e, v, w_all, bonus, g


def _rwkv_mix(pre, sbd_ref, lng_ref, lnb_ref, *, chunk, nb):
    kk_d, r_d, b_h, k_h, b_e, k_e, v, w_all, bonus, g = pre
    C = chunk
    R = nb * C
    pairs = range(A_HEADS // 2)
    seqs = range(nb)
    in_a = lax.broadcasted_iota(jnp.int32, (1, LANES), 1) < A_HD
    R2 = 2 * R
    ri = lax.broadcasted_iota(jnp.int32, (R2, R2), 0)
    ci = lax.broadcasted_iota(jnp.int32, (R2, R2), 1)
    same = (ri // C) == (ci // C)
    strict = same & (ci < ri)
    incl = same & (ci <= ri)
    eye = (ri == ci).astype(F32)

    def expand(x):
        return jnp.concatenate([jnp.where(in_a, x, 0.0), jnp.where(in_a, 0.0, x)], axis=0)

    def seq_rows(x, s):
        if nb == 1:
            return x
        return jnp.concatenate([x[s * C:(s + 1) * C], x[R + s * C:R + (s + 1) * C]], axis=0)

    def from_seqs(parts):
        if nb == 1:
            return parts[0]
        return jnp.concatenate([q[0:C] for q in parts] + [q[C:2 * C] for q in parts], axis=0)

    sl = [slice(pr * LANES, (pr + 1) * LANES) for pr in pairs]
    ex = lambda x: [expand(x[:, sl[pr]]) for pr in pairs]
    e_kk, e_r, e_bh, e_kh, e_be, e_ke = ex(kk_d), ex(r_d), ex(b_h), ex(k_h), ex(b_e), ex(k_e)
    v_e = [t.astype(BF16) for t in ex(v)]
    kk_bf = [t.astype(BF16) for t in e_kk]
    r_bf = [t.astype(BF16) for t in e_r]
    gram = [_bdot_nt(jnp.concatenate([kk_bf[pr], r_bf[pr]], axis=0),
                     jnp.concatenate([e_bh[pr], e_kh[pr]], axis=0)) for pr in pairs]
    l_b = [jnp.where(strict, gram[pr][0:R2, 0:R2], 0.0) for pr in pairs]
    l_k = [jnp.where(strict, gram[pr][0:R2, R2:2 * R2], 0.0) for pr in pairs]
    a_bk = [jnp.concatenate([jnp.where(incl, gram[pr][R2:2 * R2, 0:R2], 0.0),
                             jnp.where(incl, gram[pr][R2:2 * R2, R2:2 * R2], 0.0)],
                            axis=1).astype(BF16) for pr in pairs]
    t_inv = [eye - l_b[pr] for pr in pairs]
    pw = [t.astype(BF16) for t in l_b]
    e = 2
    while e < C:
        pw = [_bdot(pw[pr], pw[pr]).astype(BF16) for pr in pairs]
        t_inv = [t_inv[pr] + _bdot(t_inv[pr], pw[pr]) for pr in pairs]
        e *= 2
    lkv = [_bdot(l_k[pr], v_e[pr]) for pr in pairs]
    xs_kk, xs_r = [], []
    for pr in pairs:
        parts = [_bdot_nt(jnp.concatenate([seq_rows(kk_bf[pr], s), seq_rows(r_bf[pr], s)], axis=0),
                          sbd_ref[s, pr]) for s in seqs]
        xs_kk.append(from_seqs([q[0:2 * C] for q in parts]))
        xs_r.append(from_seqs([q[2 * C:4 * C] for q in parts]))
    u_e = [-_bdot(t_inv[pr], xs_kk[pr] + lkv[pr]).astype(BF16) for pr in pairs]
    uv = [jnp.concatenate([u_e[pr], v_e[pr]], axis=0) for pr in pairs]
    y_e = [xs_r[pr] + _bdot(a_bk[pr], uv[pr]) for pr in pairs]
    y = jnp.concatenate([y_e[pr][0:R] + y_e[pr][R:R2] for pr in pairs], axis=1)
    for pr in pairs:
        be_bf, ke_bf = e_be[pr].astype(BF16), e_ke[pr].astype(BF16)
        for s in seqs:
            upd = _bdot_tn(jnp.concatenate([seq_rows(u_e[pr], s), seq_rows(v_e[pr], s)], axis=0),
                           jnp.concatenate([seq_rows(be_bf, s), seq_rows(ke_bf, s)], axis=0))
            sbd_ref[s, pr] = sbd_ref[s, pr] * w_all[s * C:s * C + 1, sl[pr]] + upd

    inv_hd = 1.0 / A_HD
    mean = _seg_sum(y) * inv_hd
    yc = y - mean
    var = _seg_sum(yc * yc) * inv_hd
    yn = yc * lax.rsqrt(var + LNX_EPS) * lng_ref[...] + lnb_ref[...]
    yn = yn + bonus * v
    return yn * g


def _rwkv_kernel(p_ref, shift_ref, s0_ref, *rest, chunk, nb, sub):
    wrefs, (y_ref, sout_ref, sbd_ref, carry_ref) = rest[:-4], rest[-4:]
    c_idx = pl.program_id(1)
    pairs = range(A_HEADS // 2)
    rows = nb * chunk

    @pl.when(c_idx == 0)
    def _():
        carry_ref[...] = shift_ref[...]
        z = jnp.zeros((A_HD, A_HD), F32)
        for s in range(nb):
            for pr in pairs:
                sa = s0_ref[s, 2 * pr]
                sb = s0_ref[s, 2 * pr + 1]
                sbd_ref[s, pr] = jnp.concatenate(
                    [jnp.concatenate([sa, z], axis=1), jnp.concatenate([z, sb], axis=1)], axis=0)

    p_all = p_ref[...].astype(F32)
    chunks = [p_all[i * rows:(i + 1) * rows] for i in range(sub)]
    befores = [carry_ref[...]] + [c[rows - 1:rows, :].reshape(1, 1, RWKV_PAD) for c in chunks]
    if nb == 1:
        carry_ref[...] = befores[sub]
    pre = [_rwkv_pre(chunks[i], befores[i], *wrefs[:-2], chunk=chunk, nb=nb) for i in range(sub)]
    for i in range(sub):
        y = _rwkv_mix(pre[i], sbd_ref, *wrefs[-2:], chunk=chunk, nb=nb)
        y_ref[i * rows:(i + 1) * rows, :] = y.astype(y_ref.dtype)

    @pl.when(c_idx == pl.num_programs(1) - 1)
    def _():
        for s in range(nb):
            for pr in pairs:
                s_bd = sbd_ref[s, pr]
                sout_ref[s, 2 * pr] = s_bd[0:A_HD, 0:A_HD]
                sout_ref[s, 2 * pr + 1] = s_bd[A_HD:2 * A_HD, A_HD:2 * A_HD]


def _rwkv(proj, shift0, s0, wts, batch, seq, chunk, nb, sub):
    n_chunks = seq // (chunk * sub)
    assert nb == 1 or (n_chunks == 1 and sub == 1)
    rows = nb * chunk * sub
    vec = lambda w: pl.BlockSpec(w.shape, lambda b, c: (0,) * w.ndim)
    st = pl.BlockSpec((nb, A_HEADS, A_HD, A_HD), lambda b, c: (b, 0, 0, 0))
    return pl.pallas_call(
        functools.partial(_rwkv_kernel, chunk=chunk, nb=nb, sub=sub),
        grid=(batch // nb, n_chunks),
        in_specs=[pl.BlockSpec((rows, RWKV_PAD), lambda b, c: (b * n_chunks + c, 0)),
                  pl.BlockSpec((nb, 1, RWKV_PAD), lambda b, c: (b, 0, 0)), st]
                 + [vec(w) for w in wts],
        out_specs=[pl.BlockSpec((rows, A_W), lambda b, c: (b * n_chunks + c, 0)), st],
        out_shape=[jax.ShapeDtypeStruct((batch * seq, A_W), BF16),
                   jax.ShapeDtypeStruct((batch, A_HEADS, A_HD, A_HD), F32)],
        scratch_shapes=[pltpu.VMEM((nb, A_HEADS // 2, LANES, LANES), F32),
                        pltpu.VMEM((nb, 1, RWKV_PAD), F32)],
        compiler_params=_cparams(("parallel", "arbitrary")),
        name="rwkv7",
    )(proj, shift0.reshape(batch, 1, RWKV_PAD), s0, *wts)


def _ret_kernel(q0_ref, q1_ref, k0_ref, k1_ref, v0_ref, v1_ref, g0_ref, g1_ref, cos_ref, sin_ref,
                din_ref, dq_ref, dk_ref, dc_ref, r0_ref, y_ref, rout_ref, r_scr, *, chunk, nb):
    C = chunk
    half = RET_BLK // R_HD
    c_idx = pl.program_id(1)

    @pl.when(c_idx == 0)
    def _():
        r_scr[...] = r0_ref[...]

    even = lax.broadcasted_iota(jnp.int32, (1, R_HD), 1) % 2 == 0
    cos = cos_ref[...]
    sin = sin_ref[...]

    def rot(t):
        partner = jnp.where(even, pltpu.roll(t, R_HD - 1, axis=1), pltpu.roll(t, 1, axis=1))
        return t * cos + partner * sin

    def load(halves, s, h):
        return halves[h // half][s * C:(s + 1) * C, (h % half) * R_HD:(h % half + 1) * R_HD]

    f32 = lambda *refs: tuple(r[...].astype(F32) for r in refs)
    items = [(s, h) for s in range(nb) for h in range(R_HEADS)]
    q_in, k_in, v_in, g_in = f32(q0_ref, q1_ref), f32(k0_ref, k1_ref), f32(v0_ref, v1_ref), f32(g0_ref, g1_ref)
    q = [rot(load(q_in, s, h)) for s, h in items]
    k = [rot(load(k_in, s, h)) * (R_HD ** -0.5) for s, h in items]
    v = [load(v_in, s, h).astype(BF16) for s, h in items]
    state = [r_scr[s, h] for s, h in items]
    scores = [_bdot_nt(q[i], k[i]) * din_ref[h] for i, (s, h) in enumerate(items)]
    o = [_bdot(jnp.concatenate([scores[i], q[i] * dq_ref[h]], axis=1),
               jnp.concatenate([v[i], state[i].astype(BF16)], axis=0))
         for i, (s, h) in enumerate(items)]
    for i, (s, h) in enumerate(items):
        r_scr[s, h] = state[i] * dc_ref[h, 0:1, :] + _bdot_tn(k[i] * dk_ref[h], v[i])
    for i, (s, h) in enumerate(items):
        on = o[i] * lax.rsqrt(jnp.mean(o[i] * o[i], axis=-1, keepdims=True) + NORM_EPS)
        gate = load(g_in, s, h)
        y_ref[s * C:(s + 1) * C, h * R_HD:(h + 1) * R_HD] = (
            gate * _sigmoid(gate) * on).astype(y_ref.dtype)

    @pl.when(c_idx == pl.num_programs(1) - 1)
    def _():
        rout_ref[...] = r_scr[...]


def _ret_tables(chunk, pos0, seq):
    pos = pos0 + jnp.arange(seq, dtype=F32)
    theta = 1.0 / (10000.0 ** jnp.linspace(0.0, 1.0, R_HD // 2, dtype=F32))
    ang = pos[:, None] * theta[None, :]
    cos2 = jnp.repeat(jnp.cos(ang), 2, axis=1)
    sin2 = jnp.stack([-jnp.sin(ang), jnp.sin(ang)], axis=-1).reshape(seq, R_HD)
    lg = jnp.log1p(-(2.0 ** (-5.0 - jnp.arange(R_HEADS, dtype=F32))))
    idx = jnp.arange(chunk, dtype=F32)
    diff = idx[:, None] - idx[None, :]
    causal = diff >= 0
    din = jnp.where(causal, jnp.exp(jnp.where(causal, diff, 0.0)[None] * lg[:, None, None]), 0.0)
    bc = lambda t: jnp.broadcast_to(t[:, :, None], (R_HEADS, t.shape[1], R_HD))
    dq = bc(jnp.exp((idx + 1.0)[None, :] * lg[:, None]))
    dk = bc(jnp.exp((chunk - 1.0 - idx)[None, :] * lg[:, None]))
    dc = jnp.broadcast_to(jnp.exp(chunk * lg)[:, None, None], (R_HEADS, 8, R_HD))
    return cos2, sin2, din, dq, dk, dc


def _retention(proj, r0, batch, seq, chunk, nb, pos0):
    n_chunks = seq // chunk
    cos2, sin2, din, dq, dk, dc = _ret_tables(chunk, pos0, seq)
    col0 = RWKV_PAD // RET_BLK
    rows = nb * chunk

    def sec(i):
        return pl.BlockSpec((rows, RET_BLK), lambda b, c: (b * n_chunks + c, col0 + i))

    tab = lambda t: pl.BlockSpec(t.shape, lambda b, c: (0, 0, 0))
    st = pl.BlockSpec((nb, R_HEADS, R_HD, R_HD), lambda b, c: (b, 0, 0, 0))
    return pl.pallas_call(
        functools.partial(_ret_kernel, chunk=chunk, nb=nb),
        grid=(batch // nb, n_chunks),
        in_specs=[sec(i) for i in range(8)]
                 + [pl.BlockSpec((chunk, R_HD), lambda b, c: (c, 0)),
                    pl.BlockSpec((chunk, R_HD), lambda b, c: (c, 0)),
                    tab(din), tab(dq), tab(dk), tab(dc), st],
        out_specs=[pl.BlockSpec((rows, R_W), lambda b, c: (b * n_chunks + c, 0)), st],
        out_shape=[jax.ShapeDtypeStruct((batch * seq, R_W), BF16),
                   jax.ShapeDtypeStruct((batch, R_HEADS, R_HD, R_HD), F32)],
        scratch_shapes=[pltpu.VMEM((nb, R_HEADS, R_HD, R_HD), F32)],
        compiler_params=_cparams(("parallel", "arbitrary")),
        name="retention",
    )(*([proj] * 8), cos2, sin2, din, dq, dk, dc, r0)


def _outproj_kernel(x_ref, ya_ref, yb_ref, wa_ref, wb_ref, g_ref, wr_ref,
                    x1_ref, h2_ref, ri_ref, rg_ref):
    x1 = (x_ref[...] + jnp.dot(ya_ref[...], wa_ref[...], preferred_element_type=F32)
          + jnp.dot(yb_ref[...], wb_ref[...], preferred_element_type=F32))
    x1_ref[...] = x1
    ms = jnp.mean(x1 * x1, axis=-1, keepdims=True)
    h2 = x1 * lax.rsqrt(ms + NORM_EPS) * g_ref[...]
    h2_ref[...] = _pack_bf16_pairs(h2)
    logits = _bdot(h2, wr_ref[...])
    lane = lax.broadcasted_iota(jnp.int32, (1, LANES), 1)
    neg = -1e30
    big = LANES
    rmax = lambda t: jnp.max(t, axis=-1, keepdims=True)
    rmin = lambda t: jnp.min(t, axis=-1, keepdims=True)
    is_c = lane < N_GROUPS
    lc = jnp.where(is_c, logits, neg)
    mc = rmax(lc)
    grp = rmin(jnp.where(lc == mc, lane, big))
    p_grp = 1.0 / jnp.sum(jnp.where(is_c, jnp.exp(lc - mc), 0.0), axis=-1, keepdims=True)
    fine = lane - N_GROUPS
    in_g = (fine >= 0) & (fine < N_EXPERTS) & ((fine // E_PER_GROUP) == grp)
    lf = jnp.where(in_g, logits, neg)
    m1 = rmax(lf)
    i1 = rmin(jnp.where(lf == m1, lane, big))
    lf2 = jnp.where(lane == i1, neg, lf)
    m2 = rmax(lf2)
    i2 = rmin(jnp.where(lf2 == m2, lane, big))
    e2 = jnp.exp(m2 - m1)
    g1 = p_grp / (1.0 + e2)
    g2 = p_grp * e2 / (1.0 + e2)
    ri_ref[...] = jnp.where(lane == 0, i1 - N_GROUPS, jnp.where(lane == 1, i2 - N_GROUPS, 0))
    rg_ref[...] = jnp.where(lane == 0, g1, jnp.where(lane == 1, g2, 0.0))


def _outproj(x, ya, yb, wa, wb, g, wr, tm=512):
    n, d = x.shape
    rowblk = lambda w: pl.BlockSpec((tm, w), lambda i: (i, 0))
    full = lambda w: pl.BlockSpec(w.shape, lambda i: (0, 0))
    return pl.pallas_call(
        _outproj_kernel,
        grid=(n // tm,),
        in_specs=[rowblk(d), rowblk(A_W), rowblk(R_W), full(wa), full(wb), full(g), full(wr)],
        out_specs=[rowblk(d), rowblk(d // 2), rowblk(LANES), rowblk(LANES)],
        out_shape=[jax.ShapeDtypeStruct((n, d), F32), jax.ShapeDtypeStruct((n, d // 2), jnp.uint32),
                   jax.ShapeDtypeStruct((n, LANES), jnp.int32),
                   jax.ShapeDtypeStruct((n, LANES), F32)],
        compiler_params=_cparams(("parallel",)),
        name="outproj_router",
    )(x, ya, yb, wa, wb, g, wr)


def _sort_kernel(ri_ref, tri_ref, dest_ref, info_ref, cnt_ref, base_ref):
    ph = pl.program_id(0)
    i = pl.program_id(1)
    lane = lax.broadcasted_iota(jnp.int32, (1, LANES), 1)
    ri = ri_ref[...]
    oh1 = (lane == ri[:, 0:1]).astype(F32)
    oh2 = (lane == ri[:, 1:2]).astype(F32)
    tot1 = jnp.sum(oh1, axis=0, keepdims=True)
    tot2 = jnp.sum(oh2, axis=0, keepdims=True)

    @pl.when(ph == 0)
    def _():
        @pl.when(i == 0)
        def _():
            cnt_ref[...] = jnp.zeros_like(cnt_ref)
        cnt_ref[...] += tot1 + tot2

    @pl.when(ph == 1)
    def _():
        @pl.when(i == 0)
        def _():
            nblk = jnp.floor((cnt_ref[...] + (MOE_BLK - 1)) * (1.0 / MOE_BLK))
            upper = (lax.broadcasted_iota(jnp.int32, (LANES, LANES), 0) <=
                     lax.broadcasted_iota(jnp.int32, (LANES, LANES), 1))
            pend = _bdot(jnp.broadcast_to(nblk, (8, LANES)), upper.astype(F32))[0:1]
            first = pend - nblk
            base_ref[...] = first * MOE_BLK
            srow = lax.broadcasted_iota(jnp.int32, (8, LANES), 0)
            info_ref[...] = jnp.where(srow == 0, nblk, jnp.where(srow == 1, first, 0.0)).astype(jnp.int32)

        base = base_ref[...]
        tri = tri_ref[...]
        pre1 = jnp.dot(tri, oh1.astype(BF16), preferred_element_type=F32)
        pre2 = jnp.dot(tri, oh2.astype(BF16), preferred_element_type=F32) + tot1
        d1 = jnp.sum(oh1 * (base + pre1), axis=1, keepdims=True)
        d2 = jnp.sum(oh2 * (base + pre2), axis=1, keepdims=True)
        dest_ref[...] = jnp.where(lane == 0, d1, jnp.where(lane == 1, d2, 0.0)).astype(jnp.int32)
        base_ref[...] = base + tot1 + tot2


def _sort(ri, tt=512):
    n = ri.shape[0]
    tri = (jnp.arange(tt)[:, None] > jnp.arange(tt)[None, :]).astype(BF16)
    return pl.pallas_call(
        _sort_kernel,
        grid=(2, n // tt),
        in_specs=[pl.BlockSpec((tt, LANES), lambda ph, i: (i, 0)),
                  pl.BlockSpec((tt, tt), lambda ph, i: (0, 0))],
        out_specs=[pl.BlockSpec((tt, LANES), lambda ph, i: (i * ph, 0)),
                   pl.BlockSpec((8, LANES), lambda ph, i: (0, 0))],
        out_shape=[jax.ShapeDtypeStruct((n, LANES), jnp.int32),
                   jax.ShapeDtypeStruct((8, LANES), jnp.int32)],
        scratch_shapes=[pltpu.VMEM((1, LANES), F32), pltpu.VMEM((1, LANES), F32)],
        compiler_params=_cparams(("arbitrary", "arbitrary")),
        name="moe_sort",
    )(ri, tri)


def _row_copy(src_ref, src_row, dst_ref, dst_row, sem):
    return pltpu.make_async_copy(src_ref.at[pl.ds(src_row, 1)], dst_ref.at[pl.ds(dst_row, 1)], sem)


def _dispatch_kernel(d1_ref, d2_ref, h_ref, xs_in_ref, xs_ref, sem, *, tt):
    del xs_in_ref

    def issue(t, carry):
        _row_copy(h_ref, t, xs_ref, d1_ref[t], sem).start()
        _row_copy(h_ref, t, xs_ref, d2_ref[t], sem).start(priority=1)
        return carry

    def drain(t, carry):
        _row_copy(h_ref, t, xs_ref, d1_ref[t], sem).wait()
        _row_copy(h_ref, t, xs_ref, d2_ref[t], sem).wait()
        return carry

    lax.fori_loop(0, tt, issue, 0, unroll=8)
    lax.fori_loop(0, tt, drain, 0, unroll=8)


def _dispatch(d1, d2, h2, xs, tt=512):
    n, d = h2.shape
    smem = pl.BlockSpec((tt,), lambda i: (i,), memory_space=pltpu.SMEM)
    anyspec = pl.BlockSpec(memory_space=pl.ANY)
    return pl.pallas_call(
        functools.partial(_dispatch_kernel, tt=tt),
        grid=(n // tt,),
        in_specs=[smem, smem, pl.BlockSpec((tt, d), lambda i: (i, 0)), anyspec],
        out_specs=anyspec,
        out_shape=jax.ShapeDtypeStruct(xs.shape, xs.dtype),
        scratch_shapes=[pltpu.SemaphoreType.DMA(())],
        input_output_aliases={3: 0},
        compiler_params=_cparams(("arbitrary",)),
        name="moe_dispatch",
    )(d1, d2, h2, xs)


def _expert_kernel(nb_ref, fb_ref, xs_hbm, wg_hbm, wu_hbm, wd_hbm, y_hbm,
                   wg_r, wu_r, wd_r, wg_s, wu_s, wd_s, xbuf, ybuf, sem_w, sem_in, sem_out, *, n_blk):
    e = pl.program_id(0)
    last = pl.num_programs(0) - 1
    n = nb_ref[e]
    b0 = fb_ref[e]
    n_used = fb_ref[last] + nb_ref[last]
    half = xbuf.shape[2]
    ring = wg_r.shape[0]

    def fetch(ex):
        slot = ex % ring
        return [pltpu.make_async_copy(src.at[ex], dst.at[slot], sem_w.at[slot])
                for src, dst in ((wg_hbm, wg_r), (wu_hbm, wu_r), (wd_hbm, wd_r))]

    @pl.when(e == 0)
    def _():
        for ex in range(ring - 1):
            for cp in fetch(ex):
                cp.start(priority=W_DMA_PRIORITY)

    @pl.when(e + ring - 1 <= last)
    def _():
        for cp in fetch(e + ring - 1):
            cp.start(priority=W_DMA_PRIORITY)

    for cp in fetch(e):
        cp.wait()
    w_slot = e % ring

    def load(blk, slot):
        return pltpu.make_async_copy(xs_hbm.at[pl.ds(blk * MOE_BLK, MOE_BLK)],
                                     xbuf.at[slot], sem_in.at[slot])

    def store(blk, slot):
        return pltpu.make_async_copy(ybuf.at[slot], y_hbm.at[pl.ds(blk * MOE_BLK, MOE_BLK)],
                                     sem_out.at[slot])

    @pl.when(n > 0)
    def _():
        @pl.when(b0 == 0)
        def _():
            load(0, 0).start()

        wg_s[...] = wg_r[w_slot].astype(BF16)
        wu_s[...] = wu_r[w_slot].astype(BF16)
        wd_s[...] = wd_r[w_slot].astype(BF16)

        def body(blk, carry):
            slot = blk % 2
            load(blk, slot).wait()

            @pl.when(blk + 1 < n_used)
            def _():
                load(blk + 1, 1 - slot).start()

            @pl.when(blk >= 2)
            def _():
                store(blk - 2, slot).wait()

            x_lo, x_hi = _unpack_bf16_pairs(xbuf[slot])
            x_lo, x_hi = x_lo.astype(BF16), x_hi.astype(BF16)
            mm = lambda w_s: (jnp.dot(x_lo, w_s[0:half], preferred_element_type=F32)
                              + jnp.dot(x_hi, w_s[half:2 * half], preferred_element_type=F32))
            hg = mm(wg_s)
            hu = mm(wu_s)
            act = (hg * _sigmoid(hg) * hu).astype(BF16)
            ybuf[slot] = _pack_bf16_pairs(jnp.dot(act, wd_s[...], preferred_element_type=F32))
            store(blk, slot).start()
            return carry

        lax.fori_loop(b0, b0 + n, body, 0)

    @pl.when(e == last)
    def _():
        @pl.when(n_used >= 2)
        def _():
            store(n_used - 2, n_used % 2).wait()

        @pl.when(n_used >= 1)
        def _():
            store(n_used - 1, (n_used - 1) % 2).wait()

        ybuf[0] = jnp.zeros(ybuf.shape[1:], ybuf.dtype)

        def fill(blk, carry):
            cp = store(blk, 0)
            cp.start()
            cp.wait()
            return carry

        lax.fori_loop(n_used, n_blk, fill, 0)


def _experts(n_blocks, first_block, xs, wg, wu, wd):
    p, half = xs.shape
    d = 2 * half
    anyspec = pl.BlockSpec(memory_space=pl.ANY)
    ring = W_AHEAD + 1
    grid_spec = pltpu.PrefetchScalarGridSpec(
        num_scalar_prefetch=2,
        grid=(N_EXPERTS,),
        in_specs=[anyspec] * 4,
        out_specs=anyspec,
        scratch_shapes=[pltpu.VMEM((ring, d, D_EXPERT), F32), pltpu.VMEM((ring, d, D_EXPERT), F32),
                        pltpu.VMEM((ring, D_EXPERT, d), F32),
                        pltpu.VMEM((d, D_EXPERT), BF16), pltpu.VMEM((d, D_EXPERT), BF16),
                        pltpu.VMEM((D_EXPERT, d), BF16),
                        pltpu.VMEM((2, MOE_BLK, half), jnp.uint32),
                        pltpu.VMEM((2, MOE_BLK, half), jnp.uint32),
                        pltpu.SemaphoreType.DMA((ring,)),
                        pltpu.SemaphoreType.DMA((2,)), pltpu.SemaphoreType.DMA((2,))])
    return pl.pallas_call(
        functools.partial(_expert_kernel, n_blk=p // MOE_BLK),
        grid_spec=grid_spec,
        out_shape=jax.ShapeDtypeStruct((p, half), jnp.uint32),
        compiler_params=_cparams(("arbitrary",)),
        name="moe_experts",
    )(n_blocks, first_block, xs, wg, wu, wd)


def _combine_kernel(d1_ref, d2_ref, d1n_ref, d2n_ref, x1_ref, rg_ref, g_ref, yb_ref, o_ref, buf, sem,
                    *, tt):
    i = pl.program_id(0)
    slot = i % 2

    def gather(a_ref, b_ref, s, wait):
        def body(t, carry):
            for j, idx in enumerate((a_ref, b_ref)):
                cp = _row_copy(yb_ref, idx[t], buf.at[s, j], t, sem.at[s])
                cp.wait() if wait else cp.start(priority=j)
            return carry
        lax.fori_loop(0, tt, body, 0, unroll=8)

    @pl.when(i == 0)
    def _():
        gather(d1_ref, d2_ref, 0, False)

    @pl.when(i + 1 < pl.num_programs(0))
    def _():
        gather(d1n_ref, d2n_ref, 1 - slot, False)

    gather(d1_ref, d2_ref, slot, True)
    rg = rg_ref[...]
    lo1, hi1 = _unpack_bf16_pairs(buf[slot, 0])
    lo2, hi2 = _unpack_bf16_pairs(buf[slot, 1])
    g1, g2 = rg[:, 0:1], rg[:, 1:2]
    x = x1_ref[...] + jnp.concatenate([g1 * lo1 + g2 * lo2, g1 * hi1 + g2 * hi2], axis=1)
    ms = jnp.mean(x * x, axis=-1, keepdims=True)
    o_ref[...] = x * lax.rsqrt(ms + NORM_EPS) * g_ref[...]


def _combine(d1, d2, x1, rg, g, yb, tt=256):
    n, d = x1.shape
    steps = n // tt
    smem = pl.BlockSpec((tt,), lambda i: (i,), memory_space=pltpu.SMEM)
    smem_next = pl.BlockSpec((tt,), lambda i: (jnp.minimum(i + 1, steps - 1),),
                             memory_space=pltpu.SMEM)
    rowblk = lambda w: pl.BlockSpec((tt, w), lambda i: (i, 0))
    return pl.pallas_call(
        functools.partial(_combine_kernel, tt=tt),
        grid=(steps,),
        in_specs=[smem, smem, smem_next, smem_next, rowblk(d), rowblk(LANES),
                  pl.BlockSpec((1, d), lambda i: (0, 0)), pl.BlockSpec(memory_space=pl.ANY)],
        out_specs=rowblk(d),
        out_shape=jax.ShapeDtypeStruct((n, d), F32),
        scratch_shapes=[pltpu.VMEM((2, 2, tt, d // 2), jnp.uint32), pltpu.SemaphoreType.DMA((2,))],
        compiler_params=_cparams(("arbitrary",)),
        name="moe_combine",
    )(d1, d2, d1, d2, x1, rg, g, yb)


def _pad_cols(x, width):
    return jnp.pad(x, ((0, 0), (0, width - x.shape[1])))


def _rwkv_layout(x):
    return jnp.concatenate([
        x[:, :OFF_WD],
        _pad_cols(x[:, 3 * A_W:3 * A_W + W_LORA], LANES),
        _pad_cols(x[:, 3 * A_W + W_LORA:3 * A_W + W_LORA + A_LORA], LANES),
        _pad_cols(x[:, 3 * A_W + W_LORA + A_LORA:RWKV_PROJ], 2 * LANES)], axis=1)


def _rwkv_unlayout(x):
    return jnp.concatenate([x[:, :OFF_WD + W_LORA], x[:, OFF_AD:OFF_AD + A_LORA],
                            x[:, OFF_GD:OFF_GD + G_LORA]], axis=1)


def _pad_rows(w, rows):
    return jnp.pad(w, ((0, rows - w.shape[0]), (0, 0)))


def kernel(x_prompt, x_sample, state_rwkv_shift, state_rwkv_wkv, state_retention, norm_mix_g, w_in, mu_shift, decay_w0, decay_up, iclr_a0, iclr_up, gate_up, k_k, k_a, r_k, lnx_g, lnx_b, w_out, norm_ffn_g, router_coarse, router_fine, exp_w_gate, exp_w_up, exp_w_down, norm_final_g):
    bp, tp, d = x_prompt.shape
    bs, ts, _ = x_sample.shape
    row = lambda t: t.reshape(1, -1)
    l = 0

    w_in_p = _w_in_layout(w_in[l])
    rwkv_wts = (_rwkv_layout(row(mu_shift[l])), row(decay_w0[l]),
                _pad_rows(decay_up[l], LANES).astype(BF16), row(iclr_a0[l]),
                _pad_rows(iclr_up[l], LANES).astype(BF16),
                _pad_rows(gate_up[l], 2 * LANES).astype(BF16),
                row(k_k[l]), row(k_a[l]), row(r_k[l]), row(lnx_g[l]), row(lnx_b[l]))
    w_out_a = w_out[l][:A_W].astype(BF16)
    w_out_b = w_out[l][A_W:].astype(BF16)
    w_router = _pad_cols(jnp.concatenate(
        [router_coarse[l], jnp.transpose(router_fine[l], (1, 0, 2)).reshape(d, N_EXPERTS)], axis=1),
        LANES).astype(BF16)

    groups = (
        (x_prompt.reshape(bp * tp, d), bp, tp, WKV_CHUNK, RET_CHUNK, 1, 0.0,
         jnp.zeros((bp, RWKV_PAD), F32), jnp.zeros((bp, A_HEADS, A_HD, A_HD), F32),
         jnp.zeros((bp, R_HEADS, R_HD, R_HD), F32)),
        (x_sample.reshape(bs * ts, d), bs, ts, ts, math.gcd(ts, RET_CHUNK), 8, float(PAST_LEN),
         _rwkv_layout(state_rwkv_shift[l]), state_rwkv_wkv[l], state_retention[l]),
    )
    x_last = jnp.concatenate([x_prompt[:, -1, :], x_sample[:, -1, :]], axis=0)
    n_last = -(-(bp + bs) // 8) * 8
    shifts = _rwkv_unlayout(_inproj(_pad_rows(x_last, n_last), row(norm_mix_g[l]), w_in_p, F32,
                                    RWKV_PAD, n_last, RWKV_PAD // 2))
    new_shifts = (shifts[:bp], shifts[bp:bp + bs])

    mixed = []
    for (x, b, t, wkv_chunk, ret_chunk, nb, pos0, shift0, wkv0, ret0), new_shift in zip(groups, new_shifts):
        proj = _inproj(x, row(norm_mix_g[l]), w_in_p, BF16, PROJ_PAD, 1024, 1536)
        y_a, new_wkv = _rwkv(proj, shift0, wkv0, rwkv_wts, b, t, wkv_chunk, nb,
                             WKV_SUB if nb == 1 else 1)
        y_b, new_ret = _retention(proj, ret0, b, t, ret_chunk, nb, pos0)
        x1, h2, ri, rg = _outproj(x, y_a, y_b, w_out_a, w_out_b, row(norm_ffn_g[l]), w_router)
        mixed.append((x1, h2, ri, rg, new_shift, new_wkv, new_ret))

    n_tok = [m[0].shape[0] for m in mixed]
    n_all = sum(n_tok)
    n_blk = -(-2 * n_all // MOE_BLK) + N_EXPERTS
    dest, info = _sort(jnp.concatenate([m[2] for m in mixed], axis=0))
    n_blocks, first_block = info[0, :N_EXPERTS], info[1, :N_EXPERTS]
    xs = jnp.zeros((n_blk * MOE_BLK, d // 2), jnp.uint32)
    off = 0
    dests = []
    for m, n in zip(mixed, n_tok):
        d1, d2 = dest[off:off + n, 0], dest[off:off + n, 1]
        dests.append((d1, d2))
        xs = _dispatch(d1, d2, m[1], xs)
        off += n
    yb = _experts(n_blocks, first_block, xs, exp_w_gate[l], exp_w_up[l], exp_w_down[l])
    outs = [_combine(d1, d2, m[0], m[3], row(norm_final_g), yb)
            for m, (d1, d2) in zip(mixed, dests)]

    (_, _, _, _, p_shift, p_wkv, p_ret), (_, _, _, _, s_shift, s_wkv, s_ret) = mixed
    return (outs[0].reshape(bp, tp, d), outs[1].reshape(bs, ts, d),
            p_shift[None], p_wkv[None], p_ret[None], s_shift[None], s_wkv[None], s_ret[None])
```

```python
import functools
import math

import jax
import jax.numpy as jnp
from jax import lax
from jax.experimental import pallas as pl
from jax.experimental.pallas import tpu as pltpu

F32 = jnp.float32
BF16 = jnp.bfloat16

D_MODEL = 2048
A_W = 1024
A_HD = 64
A_HEADS = 16
W_LORA = 64
A_LORA = 64
G_LORA = 160
RWKV_PROJ = 3 * A_W + W_LORA + A_LORA + G_LORA
R_W = 1024
R_HEADS = 8
R_HD = 128
RET_CHUNK = 128
N_GROUPS = 4
E_PER_GROUP = 8
N_EXPERTS = 32
D_EXPERT = 512
NORM_EPS = 1e-6
LNX_EPS = 64e-5
DECAY_SCALE = math.exp(-0.5)
PAST_LEN = 16384

LANES = 128
OFF_WD = 3 * A_W
OFF_AD = OFF_WD + LANES
OFF_GD = OFF_AD + LANES
RWKV_PAD = OFF_GD + 2 * LANES
PROJ_PAD = RWKV_PAD + 4 * R_W
RET_BLK = 512
WKV_CHUNK = 64
WKV_SUB = 4
MOE_BLK = 256
W_AHEAD = 2
W_DMA_PRIORITY = 1
VMEM_LIMIT = 56 * 1024 * 1024


def _cparams(sem):
    return pltpu.CompilerParams(dimension_semantics=sem, vmem_limit_bytes=VMEM_LIMIT)


def _bdot(a, b):
    return jnp.dot(a.astype(BF16), b.astype(BF16), preferred_element_type=F32)


def _bdot_nt(a, b):
    return lax.dot_general(a.astype(BF16), b.astype(BF16), (((1,), (1,)), ((), ())),
                           preferred_element_type=F32)


def _bdot_tn(a, b):
    return lax.dot_general(a.astype(BF16), b.astype(BF16), (((0,), (0,)), ((), ())),
                           preferred_element_type=F32)


def _split3(x):
    hi = x.astype(BF16)
    r1 = x - hi.astype(F32)
    mid = r1.astype(BF16)
    lo = (r1 - mid.astype(F32)).astype(BF16)
    return hi, mid, lo


def _dot3(x, w_bf16):
    hi, mid, lo = _split3(x)
    d = lambda p: jnp.dot(p, w_bf16, preferred_element_type=F32)
    return d(hi) + d(mid) + d(lo)


def _sigmoid(x):
    return 1.0 / (1.0 + jnp.exp(-x))


def _pack_bf16_pairs(x):
    w = x.shape[1] // 2
    lo = lax.bitcast_convert_type(x[:, :w].astype(BF16).astype(F32), jnp.uint32)
    hi = lax.bitcast_convert_type(x[:, w:].astype(BF16).astype(F32), jnp.uint32)
    return hi | (lo >> 16)


def _unpack_bf16_pairs(p):
    lo = lax.bitcast_convert_type(p << 16, F32)
    hi = lax.bitcast_convert_type(p & jnp.uint32(0xFFFF0000), F32)
    return lo, hi


def _inproj_kernel(x_ref, g_ref, w_ref, o_ref, h_ref):
    @pl.when(pl.program_id(1) == 0)
    def _():
        x = x_ref[...]
        ms = jnp.mean(x * x, axis=-1, keepdims=True)
        h_ref[...] = (x * lax.rsqrt(ms + NORM_EPS) * g_ref[...]).astype(BF16)

    o_ref[...] = jnp.dot(h_ref[...], w_ref[...], preferred_element_type=F32).astype(o_ref.dtype)


def _inproj(x, g, w_bf16, out_dtype, n_cols, tm, tn):
    n, d = x.shape
    return pl.pallas_call(
        _inproj_kernel,
        grid=(n // tm, n_cols // tn),
        in_specs=[pl.BlockSpec((tm, d), lambda i, j: (i, 0)),
                  pl.BlockSpec((1, d), lambda i, j: (0, 0)),
                  pl.BlockSpec((d, tn), lambda i, j: (0, j))],
        out_specs=pl.BlockSpec((tm, tn), lambda i, j: (i, j)),
        out_shape=jax.ShapeDtypeStruct((n, n_cols), out_dtype),
        scratch_shapes=[pltpu.VMEM((tm, d), BF16)],
        compiler_params=_cparams(("parallel", "arbitrary")),
        name="inproj",
    )(x, g, w_bf16)


def _seg_sum(x):
    ones_bd = (lax.broadcasted_iota(jnp.int32, (LANES, LANES), 0) // A_HD ==
               lax.broadcasted_iota(jnp.int32, (LANES, LANES), 1) // A_HD).astype(BF16)
    nslab = x.shape[1] // LANES
    stacked = jnp.concatenate([x[:, s * LANES:(s + 1) * LANES] for s in range(nslab)], axis=0)
    red = _dot3(stacked, ones_bd)
    c = x.shape[0]
    return jnp.concatenate([red[s * c:(s + 1) * c] for s in range(nslab)], axis=1)


def _rwkv_pre(p, before, mu_ref, w0_ref, wup_ref, a0_ref, aup_ref, gup_ref, kk_ref, ka_ref, rk_ref,
              *, chunk, nb):
    C = chunk
    R = nb * C
    row = lax.broadcasted_iota(jnp.int32, (R, 1), 0)
    pos = row % C
    first = jnp.broadcast_to(before, (nb, C, RWKV_PAD)).reshape(R, RWKV_PAD)
    prev = jnp.where(pos == 0, first, pltpu.roll(p, 1, axis=0))
    xm = p + (prev - p) * mu_ref[...]

    r = xm[:, 0:A_W]
    k = xm[:, A_W:2 * A_W]
    v = xm[:, 2 * A_W:3 * A_W]
    wd = xm[:, OFF_WD:OFF_AD]
    ad = xm[:, OFF_AD:OFF_GD]
    gd = xm[:, OFF_GD:RWKV_PAD]

    logw = -DECAY_SCALE * _sigmoid(w0_ref[...] + _bdot(jnp.tanh(wd), wup_ref[...]))
    a = _sigmoid(a0_ref[...] + _bdot(ad, aup_ref[...]))
    g = _bdot(_sigmoid(gd), gup_ref[...])

    kk = k * kk_ref[...]
    k2 = k * (1.0 + (a - 1.0) * ka_ref[...])
    sums = _seg_sum(jnp.concatenate([kk * kk, r * k2 * rk_ref[...]], axis=0))
    kk = kk / jnp.maximum(jnp.sqrt(sums[0:R]), 1e-12)
    bonus = sums[R:2 * R]
    b = kk * a

    cum = logw
    suf = logw
    d = 1
    while d < C:
        cum = cum + jnp.where(pos >= d, pltpu.roll(cum, d, axis=0), 0.0)
        if nb > 1:
            suf = suf + jnp.where(pos < C - d, pltpu.roll(suf, R - d, axis=0), 0.0)
        d *= 2
    after = suf - logw if nb > 1 else cum[C - 1:C, :] - cum
    w_incl = jnp.exp(cum)
    w_excl = jnp.exp(cum - logw)
    w_inv = jnp.exp(-cum)
    w_end = jnp.exp(after)
    w_all = jnp.exp(cum + after)

    kk_d = kk * w_excl
    r_d = r * w_incl
    b_h = b * w_inv
    k_h = k2 * w_inv
    b_e = b * w_end
    k_e = k2 * w_end
    return kk_d, r_d, b_h, k_h, b_e, k_e, v, w_all, bonus, g


def _rwkv_mix(pre, sbd_ref, lng_ref, lnb_ref, *, chunk, nb):
    kk_d, r_d, b_h, k_h, b_e, k_e, v, w_all, bonus, g = pre
    C = chunk
    R = nb * C
    pairs = range(A_HEADS // 2)
    seqs = range(nb)
    in_a = lax.broadcasted_iota(jnp.int32, (1, LANES), 1) < A_HD
    R2 = 2 * R
    ri = lax.broadcasted_iota(jnp.int32, (R2, R2), 0)
    ci = lax.broadcasted_iota(jnp.int32, (R2, R2), 1)
    same = (ri // C) == (ci // C)
    strict = same & (ci < ri)
    incl = same & (ci <= ri)
    eye = (ri == ci).astype(F32)

    def expand(x):
        return jnp.concatenate([jnp.where(in_a, x, 0.0), jnp.where(in_a, 0.0, x)], axis=0)

    def seq_rows(x, s):
        if nb == 1:
            return x
        return jnp.concatenate([x[s * C:(s + 1) * C], x[R + s * C:R + (s + 1) * C]], axis=0)

    def from_seqs(parts):
        if nb == 1:
            return parts[0]
        return jnp.concatenate([q[0:C] for q in parts] + [q[C:2 * C] for q in parts], axis=0)

    sl = [slice(pr * LANES, (pr + 1) * LANES) for pr in pairs]
    ex = lambda x: [expand(x[:, sl[pr]]) for pr in pairs]
    e_kk, e_r, e_bh, e_kh, e_be, e_ke = ex(kk_d), ex(r_d), ex(b_h), ex(k_h), ex(b_e), ex(k_e)
    v_e = [t.astype(BF16) for t in ex(v)]
    kk_bf = [t.astype(BF16) for t in e_kk]
    r_bf = [t.astype(BF16) for t in e_r]
    gram = [_bdot_nt(jnp.concatenate([kk_bf[pr], r_bf[pr]], axis=0),
                     jnp.concatenate([e_bh[pr], e_kh[pr]], axis=0)) for pr in pairs]
    l_b = [jnp.where(strict, gram[pr][0:R2, 0:R2], 0.0) for pr in pairs]
    l_k = [jnp.where(strict, gram[pr][0:R2, R2:2 * R2], 0.0) for pr in pairs]
    a_bk = [jnp.concatenate([jnp.where(incl, gram[pr][R2:2 * R2, 0:R2], 0.0),
                             jnp.where(incl, gram[pr][R2:2 * R2, R2:2 * R2], 0.0)],
                            axis=1).astype(BF16) for pr in pairs]
    t_inv = [eye - l_b[pr] for pr in pairs]
    pw = [t.astype(BF16) for t in l_b]
    e = 2
    while e < C:
        pw = [_bdot(pw[pr], pw[pr]).astype(BF16) for pr in pairs]
        t_inv = [t_inv[pr] + _bdot(t_inv[pr], pw[pr]) for pr in pairs]
        e *= 2
    lkv = [_bdot(l_k[pr], v_e[pr]) for pr in pairs]
    xs_kk, xs_r = [], []
    for pr in pairs:
        parts = [_bdot_nt(jnp.concatenate([seq_rows(kk_bf[pr], s), seq_rows(r_bf[pr], s)], axis=0),
                          sbd_ref[s, pr]) for s in seqs]
        xs_kk.append(from_seqs([q[0:2 * C] for q in parts]))
        xs_r.append(from_seqs([q[2 * C:4 * C] for q in parts]))
    u_e = [-_bdot(t_inv[pr], xs_kk[pr] + lkv[pr]).astype(BF16) for pr in pairs]
    uv = [jnp.concatenate([u_e[pr], v_e[pr]], axis=0) for pr in pairs]
    y_e = [xs_r[pr] + _bdot(a_bk[pr], uv[pr]) for pr in pairs]
    y = jnp.concatenate([y_e[pr][0:R] + y_e[pr][R:R2] for pr in pairs], axis=1)
    for pr in pairs:
        be_bf, ke_bf = e_be[pr].astype(BF16), e_ke[pr].astype(BF16)
        for s in seqs:
            upd = _bdot_tn(jnp.concatenate([seq_rows(u_e[pr], s), seq_rows(v_e[pr], s)], axis=0),
                           jnp.concatenate([seq_rows(be_bf, s), seq_rows(ke_bf, s)], axis=0))
            sbd_ref[s, pr] = sbd_ref[s, pr] * w_all[s * C:s * C + 1, sl[pr]] + upd

    inv_hd = 1.0 / A_HD
    mean = _seg_sum(y) * inv_hd
    yc = y - mean
    var = _seg_sum(yc * yc) * inv_hd
    yn = yc * lax.rsqrt(var + LNX_EPS) * lng_ref[...] + lnb_ref[...]
    yn = yn + bonus * v
    return yn * g


def _rwkv_kernel(p_ref, shift_ref, s0_ref, *rest, chunk, nb, sub):
    wrefs, (y_ref, sout_ref, sbd_ref, carry_ref) = rest[:-4], rest[-4:]
    c_idx = pl.program_id(1)
    pairs = range(A_HEADS // 2)
    rows = nb * chunk

    @pl.when(c_idx == 0)
    def _():
        carry_ref[...] = shift_ref[...]
        z = jnp.zeros((A_HD, A_HD), F32)
        for s in range(nb):
            for pr in pairs:
                sa = s0_ref[s, 2 * pr]
                sb = s0_ref[s, 2 * pr + 1]
                sbd_ref[s, pr] = jnp.concatenate(
                    [jnp.concatenate([sa, z], axis=1), jnp.concatenate([z, sb], axis=1)], axis=0)

    p_all = p_ref[...].astype(F32)
    chunks = [p_all[i * rows:(i + 1) * rows] for i in range(sub)]
    befores = [carry_ref[...]] + [c[rows - 1:rows, :].reshape(1, 1, RWKV_PAD) for c in chunks]
    if nb == 1:
        carry_ref[...] = befores[sub]
    pre = [_rwkv_pre(chunks[i], befores[i], *wrefs[:-2], chunk=chunk, nb=nb) for i in range(sub)]
    for i in range(sub):
        y = _rwkv_mix(pre[i], sbd_ref, *wrefs[-2:], chunk=chunk, nb=nb)
        y_ref[i * rows:(i + 1) * rows, :] = y.astype(y_ref.dtype)

    @pl.when(c_idx == pl.num_programs(1) - 1)
    def _():
        for s in range(nb):
            for pr in pairs:
                s_bd = sbd_ref[s, pr]
                sout_ref[s, 2 * pr] = s_bd[0:A_HD, 0:A_HD]
                sout_ref[s, 2 * pr + 1] = s_bd[A_HD:2 * A_HD, A_HD:2 * A_HD]


def _rwkv(proj, shift0, s0, wts, batch, seq, chunk, nb, sub):
    n_chunks = seq // (chunk * sub)
    assert nb == 1 or (n_chunks == 1 and sub == 1)
    rows = nb * chunk * sub
    vec = lambda w: pl.BlockSpec(w.shape, lambda b, c: (0,) * w.ndim)
    st = pl.BlockSpec((nb, A_HEADS, A_HD, A_HD), lambda b, c: (b, 0, 0, 0))
    return pl.pallas_call(
        functools.partial(_rwkv_kernel, chunk=chunk, nb=nb, sub=sub),
        grid=(batch // nb, n_chunks),
        in_specs=[pl.BlockSpec((rows, RWKV_PAD), lambda b, c: (b * n_chunks + c, 0)),
                  pl.BlockSpec((nb, 1, RWKV_PAD), lambda b, c: (b, 0, 0)), st]
                 + [vec(w) for w in wts],
        out_specs=[pl.BlockSpec((rows, A_W), lambda b, c: (b * n_chunks + c, 0)), st],
        out_shape=[jax.ShapeDtypeStruct((batch * seq, A_W), BF16),
                   jax.ShapeDtypeStruct((batch, A_HEADS, A_HD, A_HD), F32)],
        scratch_shapes=[pltpu.VMEM((nb, A_HEADS // 2, LANES, LANES), F32),
                        pltpu.VMEM((nb, 1, RWKV_PAD), F32)],
        compiler_params=_cparams(("parallel", "arbitrary")),
        name="rwkv7",
    )(proj, shift0.reshape(batch, 1, RWKV_PAD), s0, *wts)


def _ret_kernel(q0_ref, q1_ref, k0_ref, k1_ref, v0_ref, v1_ref, g0_ref, g1_ref, cos_ref, sin_ref,
                din_ref, dq_ref, dk_ref, dc_ref, r0_ref, y_ref, rout_ref, r_scr, *, chunk, nb):
    C = chunk
    half = RET_BLK // R_HD
    c_idx = pl.program_id(1)

    @pl.when(c_idx == 0)
    def _():
        r_scr[...] = r0_ref[...]

    even = lax.broadcasted_iota(jnp.int32, (1, R_HD), 1) % 2 == 0
    cos = cos_ref[...]
    sin = sin_ref[...]

    def rot(t):
        partner = jnp.where(even, pltpu.roll(t, R_HD - 1, axis=1), pltpu.roll(t, 1, axis=1))
        return t * cos + partner * sin

    def load(halves, s, h):
        return halves[h // half][s * C:(s + 1) * C, (h % half) * R_HD:(h % half + 1) * R_HD]

    f32 = lambda *refs: tuple(r[...].astype(F32) for r in refs)
    items = [(s, h) for s in range(nb) for h in range(R_HEADS)]
    q_in, k_in, v_in, g_in = f32(q0_ref, q1_ref), f32(k0_ref, k1_ref), f32(v0_ref, v1_ref), f32(g0_ref, g1_ref)
    q = [rot(load(q_in, s, h)) for s, h in items]
    k = [rot(load(k_in, s, h)) * (R_HD ** -0.5) for s, h in items]
    v = [load(v_in, s, h).astype(BF16) for s, h in items]
    state = [r_scr[s, h] for s, h in items]
    scores = [_bdot_nt(q[i], k[i]) * din_ref[h] for i, (s, h) in enumerate(items)]
    o = [_bdot(jnp.concatenate([scores[i], q[i] * dq_ref[h]], axis=1),
               jnp.concatenate([v[i], state[i].astype(BF16)], axis=0))
         for i, (s, h) in enumerate(items)]
    for i, (s, h) in enumerate(items):
        r_scr[s, h] = state[i] * dc_ref[h, 0:1, :] + _bdot_tn(k[i] * dk_ref[h], v[i])
    for i, (s, h) in enumerate(items):
        on = o[i] * lax.rsqrt(jnp.mean(o[i] * o[i], axis=-1, keepdims=True) + NORM_EPS)
        gate = load(g_in, s, h)
        y_ref[s * C:(s + 1) * C, h * R_HD:(h + 1) * R_HD] = (
            gate * _sigmoid(gate) * on).astype(y_ref.dtype)

    @pl.when(c_idx == pl.num_programs(1) - 1)
    def _():
        rout_ref[...] = r_scr[...]


def _ret_tables(chunk, pos0, seq):
    pos = pos0 + jnp.arange(seq, dtype=F32)
    theta = 1.0 / (10000.0 ** jnp.linspace(0.0, 1.0, R_HD // 2, dtype=F32))
    ang = pos[:, None] * theta[None, :]
    cos2 = jnp.repeat(jnp.cos(ang), 2, axis=1)
    sin2 = jnp.stack([-jnp.sin(ang), jnp.sin(ang)], axis=-1).reshape(seq, R_HD)
    lg = jnp.log1p(-(2.0 ** (-5.0 - jnp.arange(R_HEADS, dtype=F32))))
    idx = jnp.arange(chunk, dtype=F32)
    diff = idx[:, None] - idx[None, :]
    causal = diff >= 0
    din = jnp.where(causal, jnp.exp(jnp.where(causal, diff, 0.0)[None] * lg[:, None, None]), 0.0)
    bc = lambda t: jnp.broadcast_to(t[:, :, None], (R_HEADS, t.shape[1], R_HD))
    dq = bc(jnp.exp((idx + 1.0)[None, :] * lg[:, None]))
    dk = bc(jnp.exp((chunk - 1.0 - idx)[None, :] * lg[:, None]))
    dc = jnp.broadcast_to(jnp.exp(chunk * lg)[:, None, None], (R_HEADS, 8, R_HD))
    return cos2, sin2, din, dq, dk, dc


def _retention(proj, r0, batch, seq, chunk, nb, pos0):
    n_chunks = seq // chunk
    cos2, sin2, din, dq, dk, dc = _ret_tables(chunk, pos0, seq)
    col0 = RWKV_PAD // RET_BLK
    rows = nb * chunk

    def sec(i):
        return pl.BlockSpec((rows, RET_BLK), lambda b, c: (b * n_chunks + c, col0 + i))

    tab = lambda t: pl.BlockSpec(t.shape, lambda b, c: (0, 0, 0))
    st = pl.BlockSpec((nb, R_HEADS, R_HD, R_HD), lambda b, c: (b, 0, 0, 0))
    return pl.pallas_call(
        functools.partial(_ret_kernel, chunk=chunk, nb=nb),
        grid=(batch // nb, n_chunks),
        in_specs=[sec(i) for i in range(8)]
                 + [pl.BlockSpec((chunk, R_HD), lambda b, c: (c, 0)),
                    pl.BlockSpec((chunk, R_HD), lambda b, c: (c, 0)),
                    tab(din), tab(dq), tab(dk), tab(dc), st],
        out_specs=[pl.BlockSpec((rows, R_W), lambda b, c: (b * n_chunks + c, 0)), st],
        out_shape=[jax.ShapeDtypeStruct((batch * seq, R_W), BF16),
                   jax.ShapeDtypeStruct((batch, R_HEADS, R_HD, R_HD), F32)],
        scratch_shapes=[pltpu.VMEM((nb, R_HEADS, R_HD, R_HD), F32)],
        compiler_params=_cparams(("parallel", "arbitrary")),
        name="retention",
    )(*([proj] * 8), cos2, sin2, din, dq, dk, dc, r0)


def _outproj_kernel(x_ref, ya_ref, yb_ref, wa_ref, wb_ref, g_ref, wr_ref,
                    x1_ref, h2_ref, ri_ref, rg_ref, *, parts):
    rows = x_ref.shape[0] // parts
    sl = [slice(i * rows, (i + 1) * rows) for i in range(parts)]
    x1s = [x_ref[s, :] + jnp.dot(ya_ref[s, :], wa_ref[...], preferred_element_type=F32)
           + jnp.dot(yb_ref[s, :], wb_ref[...], preferred_element_type=F32) for s in sl]
    for s, x1 in zip(sl, x1s):
        x1_ref[s, :] = x1
        ms = jnp.mean(x1 * x1, axis=-1, keepdims=True)
        h2 = x1 * lax.rsqrt(ms + NORM_EPS) * g_ref[...]
        h2_ref[s, :] = _pack_bf16_pairs(h2)
        ri_ref[s, :], rg_ref[s, :] = _route(_bdot(h2, wr_ref[...]))


def _route(logits):
    lane = lax.broadcasted_iota(jnp.int32, (1, LANES), 1)
    neg = -1e30
    big = LANES
    rmax = lambda t: jnp.max(t, axis=-1, keepdims=True)
    rmin = lambda t: jnp.min(t, axis=-1, keepdims=True)
    is_c = lane < N_GROUPS
    lc = jnp.where(is_c, logits, neg)
    mc = rmax(lc)
    grp = rmin(jnp.where(lc == mc, lane, big))
    p_grp = 1.0 / jnp.sum(jnp.where(is_c, jnp.exp(lc - mc), 0.0), axis=-1, keepdims=True)
    fine = lane - N_GROUPS
    in_g = (fine >= 0) & (fine < N_EXPERTS) & ((fine // E_PER_GROUP) == grp)
    lf = jnp.where(in_g, logits, neg)
    m1 = rmax(lf)
    i1 = rmin(jnp.where(lf == m1, lane, big))
    lf2 = jnp.where(lane == i1, neg, lf)
    m2 = rmax(lf2)
    i2 = rmin(jnp.where(lf2 == m2, lane, big))
    e2 = jnp.exp(m2 - m1)
    g1 = p_grp / (1.0 + e2)
    g2 = p_grp * e2 / (1.0 + e2)
    return (jnp.where(lane == 0, i1 - N_GROUPS, jnp.where(lane == 1, i2 - N_GROUPS, 0)),
            jnp.where(lane == 0, g1, jnp.where(lane == 1, g2, 0.0)))


def _outproj(x, ya, yb, wa, wb, g, wr, tm=512, parts=2):
    n, d = x.shape
    rowblk = lambda w: pl.BlockSpec((tm, w), lambda i: (i, 0))
    full = lambda w: pl.BlockSpec(w.shape, lambda i: (0, 0))
    return pl.pallas_call(
        functools.partial(_outproj_kernel, parts=parts),
        grid=(n // tm,),
        in_specs=[rowblk(d), rowblk(A_W), rowblk(R_W), full(wa), full(wb), full(g), full(wr)],
        out_specs=[rowblk(d), rowblk(d // 2), rowblk(LANES), rowblk(LANES)],
        out_shape=[jax.ShapeDtypeStruct((n, d), F32), jax.ShapeDtypeStruct((n, d // 2), jnp.uint32),
                   jax.ShapeDtypeStruct((n, LANES), jnp.int32),
                   jax.ShapeDtypeStruct((n, LANES), F32)],
        compiler_params=_cparams(("parallel",)),
        name="outproj_router",
    )(x, ya, yb, wa, wb, g, wr)


def _sort_kernel(ri_ref, tri_ref, dest_ref, info_ref, cnt_ref, base_ref):
    ph = pl.program_id(0)
    i = pl.program_id(1)
    lane = lax.broadcasted_iota(jnp.int32, (1, LANES), 1)
    ri = ri_ref[...]
    oh1 = (lane == ri[:, 0:1]).astype(F32)
    oh2 = (lane == ri[:, 1:2]).astype(F32)
    tot1 = jnp.sum(oh1, axis=0, keepdims=True)
    tot2 = jnp.sum(oh2, axis=0, keepdims=True)

    @pl.when(ph == 0)
    def _():
        @pl.when(i == 0)
        def _():
            cnt_ref[...] = jnp.zeros_like(cnt_ref)
        cnt_ref[...] += tot1 + tot2

    @pl.when(ph == 1)
    def _():
        @pl.when(i == 0)
        def _():
            nblk = jnp.floor((cnt_ref[...] + (MOE_BLK - 1)) * (1.0 / MOE_BLK))
            upper = (lax.broadcasted_iota(jnp.int32, (LANES, LANES), 0) <=
                     lax.broadcasted_iota(jnp.int32, (LANES, LANES), 1))
            pend = _bdot(jnp.broadcast_to(nblk, (8, LANES)), upper.astype(F32))[0:1]
            first = pend - nblk
            base_ref[...] = first * MOE_BLK
            srow = lax.broadcasted_iota(jnp.int32, (8, LANES), 0)
            info_ref[...] = jnp.where(srow == 0, nblk, jnp.where(srow == 1, first, 0.0)).astype(jnp.int32)

        base = base_ref[...]
        tri = tri_ref[...]
        pre1 = jnp.dot(tri, oh1.astype(BF16), preferred_element_type=F32)
        pre2 = jnp.dot(tri, oh2.astype(BF16), preferred_element_type=F32) + tot1
        d1 = jnp.sum(oh1 * (base + pre1), axis=1, keepdims=True)
        d2 = jnp.sum(oh2 * (base + pre2), axis=1, keepdims=True)
        dest_ref[...] = jnp.where(lane == 0, d1, jnp.where(lane == 1, d2, 0.0)).astype(jnp.int32)
        base_ref[...] = base + tot1 + tot2


def _sort(ri, tt=512):
    n = ri.shape[0]
    tri = (jnp.arange(tt)[:, None] > jnp.arange(tt)[None, :]).astype(BF16)
    return pl.pallas_call(
        _sort_kernel,
        grid=(2, n // tt),
        in_specs=[pl.BlockSpec((tt, LANES), lambda ph, i: (i, 0)),
                  pl.BlockSpec((tt, tt), lambda ph, i: (0, 0))],
        out_specs=[pl.BlockSpec((tt, LANES), lambda ph, i: (i * ph, 0)),
                   pl.BlockSpec((8, LANES), lambda ph, i: (0, 0))],
        out_shape=[jax.ShapeDtypeStruct((n, LANES), jnp.int32),
                   jax.ShapeDtypeStruct((8, LANES), jnp.int32)],
        scratch_shapes=[pltpu.VMEM((1, LANES), F32), pltpu.VMEM((1, LANES), F32)],
        compiler_params=_cparams(("arbitrary", "arbitrary")),
        name="moe_sort",
    )(ri, tri)


def _row_copy(src_ref, src_row, dst_ref, dst_row, sem):
    return pltpu.make_async_copy(src_ref.at[pl.ds(src_row, 1)], dst_ref.at[pl.ds(dst_row, 1)], sem)


def _dispatch_kernel(d1_ref, d2_ref, h_ref, xs_in_ref, xs_ref, sem, *, tt):
    del xs_in_ref

    def issue(t, carry):
        _row_copy(h_ref, t, xs_ref, d1_ref[t], sem).start()
        _row_copy(h_ref, t, xs_ref, d2_ref[t], sem).start(priority=1)
        return carry

    def drain(t, carry):
        _row_copy(h_ref, t, xs_ref, d1_ref[t], sem).wait()
        _row_copy(h_ref, t, xs_ref, d2_ref[t], sem).wait()
        return carry

    lax.fori_loop(0, tt, issue, 0, unroll=8)
    lax.fori_loop(0, tt, drain, 0, unroll=8)


def _dispatch(d1, d2, h2, xs, tt=512):
    n, d = h2.shape
    smem = pl.BlockSpec((tt,), lambda i: (i,), memory_space=pltpu.SMEM)
    anyspec = pl.BlockSpec(memory_space=pl.ANY)
    return pl.pallas_call(
        functools.partial(_dispatch_kernel, tt=tt),
        grid=(n // tt,),
        in_specs=[smem, smem, pl.BlockSpec((tt, d), lambda i: (i, 0)), anyspec],
        out_specs=anyspec,
        out_shape=jax.ShapeDtypeStruct(xs.shape, xs.dtype),
        scratch_shapes=[pltpu.SemaphoreType.DMA(())],
        input_output_aliases={3: 0},
        compiler_params=_cparams(("arbitrary",)),
        name="moe_dispatch",
    )(d1, d2, h2, xs)


def _expert_kernel(nb_ref, fb_ref, xs_hbm, wg_hbm, wu_hbm, wd_hbm, y_hbm,
                   wg_r, wu_r, wd_r, wg_s, wu_s, wd_s, xbuf, ybuf, sem_w, sem_in, sem_out, *, n_blk):
    e = pl.program_id(0)
    last = pl.num_programs(0) - 1
    n = nb_ref[e]
    b0 = fb_ref[e]
    n_used = fb_ref[last] + nb_ref[last]
    half = xbuf.shape[2]
    ring = wg_r.shape[0]

    def fetch(ex):
        slot = ex % ring
        return [pltpu.make_async_copy(src.at[ex], dst.at[slot], sem_w.at[slot])
                for src, dst in ((wg_hbm, wg_r), (wu_hbm, wu_r), (wd_hbm, wd_r))]

    @pl.when(e == 0)
    def _():
        for ex in range(ring - 1):
            for cp in fetch(ex):
                cp.start(priority=W_DMA_PRIORITY)

    @pl.when(e + ring - 1 <= last)
    def _():
        for cp in fetch(e + ring - 1):
            cp.start(priority=W_DMA_PRIORITY)

    for cp in fetch(e):
        cp.wait()
    w_slot = e % ring

    def load(blk, slot):
        return pltpu.make_async_copy(xs_hbm.at[pl.ds(blk * MOE_BLK, MOE_BLK)],
                                     xbuf.at[slot], sem_in.at[slot])

    def store(blk, slot):
        return pltpu.make_async_copy(ybuf.at[slot], y_hbm.at[pl.ds(blk * MOE_BLK, MOE_BLK)],
                                     sem_out.at[slot])

    @pl.when(n > 0)
    def _():
        @pl.when(b0 == 0)
        def _():
            load(0, 0).start()

        wg_s[...] = wg_r[w_slot].astype(BF16)
        wu_s[...] = wu_r[w_slot].astype(BF16)
        wd_s[...] = wd_r[w_slot].astype(BF16)

        def body(blk, carry):
            slot = blk % 2
            load(blk, slot).wait()

            @pl.when(blk + 1 < n_used)
            def _():
                load(blk + 1, 1 - slot).start()

            @pl.when(blk >= 2)
            def _():
                store(blk - 2, slot).wait()

            x_lo, x_hi = _unpack_bf16_pairs(xbuf[slot])
            x_lo, x_hi = x_lo.astype(BF16), x_hi.astype(BF16)
            mm = lambda w_s: (jnp.dot(x_lo, w_s[0:half], preferred_element_type=F32)
                              + jnp.dot(x_hi, w_s[half:2 * half], preferred_element_type=F32))
            hg = mm(wg_s)
            hu = mm(wu_s)
            act = (hg * _sigmoid(hg) * hu).astype(BF16)
            ybuf[slot] = _pack_bf16_pairs(jnp.dot(act, wd_s[...], preferred_element_type=F32))
            store(blk, slot).start()
            return carry

        lax.fori_loop(b0, b0 + n, body, 0)

    @pl.when(e == last)
    def _():
        @pl.when(n_used >= 2)
        def _():
            store(n_used - 2, n_used % 2).wait()

        @pl.when(n_used >= 1)
        def _():
            store(n_used - 1, (n_used - 1) % 2).wait()

        ybuf[0] = jnp.zeros(ybuf.shape[1:], ybuf.dtype)

        def fill(blk, carry):
            cp = store(blk, 0)
            cp.start()
            cp.wait()
            return carry

        lax.fori_loop(n_used, n_blk, fill, 0)


def _experts(n_blocks, first_block, xs, wg, wu, wd):
    p, half = xs.shape
    d = 2 * half
    anyspec = pl.BlockSpec(memory_space=pl.ANY)
    ring = W_AHEAD + 1
    grid_spec = pltpu.PrefetchScalarGridSpec(
        num_scalar_prefetch=2,
        grid=(N_EXPERTS,),
        in_specs=[anyspec] * 4,
        out_specs=anyspec,
        scratch_shapes=[pltpu.VMEM((ring, d, D_EXPERT), F32), pltpu.VMEM((ring, d, D_EXPERT), F32),
                        pltpu.VMEM((ring, D_EXPERT, d), F32),
                        pltpu.VMEM((d, D_EXPERT), BF16), pltpu.VMEM((d, D_EXPERT), BF16),
                        pltpu.VMEM((D_EXPERT, d), BF16),
                        pltpu.VMEM((2, MOE_BLK, half), jnp.uint32),
                        pltpu.VMEM((2, MOE_BLK, half), jnp.uint32),
                        pltpu.SemaphoreType.DMA((ring,)),
                        pltpu.SemaphoreType.DMA((2,)), pltpu.SemaphoreType.DMA((2,))])
    return pl.pallas_call(
        functools.partial(_expert_kernel, n_blk=p // MOE_BLK),
        grid_spec=grid_spec,
        out_shape=jax.ShapeDtypeStruct((p, half), jnp.uint32),
        compiler_params=_cparams(("arbitrary",)),
        name="moe_experts",
    )(n_blocks, first_block, xs, wg, wu, wd)


def _combine_kernel(d1_ref, d2_ref, d1n_ref, d2n_ref, x1_ref, rg_ref, g_ref, yb_ref, o_ref, buf, sem,
                    *, tt):
    i = pl.program_id(0)
    slot = i % 2

    def gather(a_ref, b_ref, s, wait):
        def body(t, carry):
            for j, idx in enumerate((a_ref, b_ref)):
                cp = _row_copy(yb_ref, idx[t], buf.at[s, j], t, sem.at[s])
                cp.wait() if wait else cp.start(priority=j)
            return carry
        lax.fori_loop(0, tt, body, 0, unroll=8)

    @pl.when(i == 0)
    def _():
        gather(d1_ref, d2_ref, 0, False)

    @pl.when(i + 1 < pl.num_programs(0))
    def _():
        gather(d1n_ref, d2n_ref, 1 - slot, False)

    gather(d1_ref, d2_ref, slot, True)
    rg = rg_ref[...]
    lo1, hi1 = _unpack_bf16_pairs(buf[slot, 0])
    lo2, hi2 = _unpack_bf16_pairs(buf[slot, 1])
    g1, g2 = rg[:, 0:1], rg[:, 1:2]
    x = x1_ref[...] + jnp.concatenate([g1 * lo1 + g2 * lo2, g1 * hi1 + g2 * hi2], axis=1)
    ms = jnp.mean(x * x, axis=-1, keepdims=True)
    o_ref[...] = x * lax.rsqrt(ms + NORM_EPS) * g_ref[...]


def _combine(d1, d2, x1, rg, g, yb, tt=256):
    n, d = x1.shape
    steps = n // tt
    smem = pl.BlockSpec((tt,), lambda i: (i,), memory_space=pltpu.SMEM)
    smem_next = pl.BlockSpec((tt,), lambda i: (jnp.minimum(i + 1, steps - 1),),
                             memory_space=pltpu.SMEM)
    rowblk = lambda w: pl.BlockSpec((tt, w), lambda i: (i, 0))
    return pl.pallas_call(
        functools.partial(_combine_kernel, tt=tt),
        grid=(steps,),
        in_specs=[smem, smem, smem_next, smem_next, rowblk(d), rowblk(LANES),
                  pl.BlockSpec((1, d), lambda i: (0, 0)), pl.BlockSpec(memory_space=pl.ANY)],
        out_specs=rowblk(d),
        out_shape=jax.ShapeDtypeStruct((n, d), F32),
        scratch_shapes=[pltpu.VMEM((2, 2, tt, d // 2), jnp.uint32), pltpu.SemaphoreType.DMA((2,))],
        compiler_params=_cparams(("arbitrary",)),
        name="moe_combine",
    )(d1, d2, d1, d2, x1, rg, g, yb)


def _pad_cols(x, width):
    return jnp.pad(x, ((0, 0), (0, width - x.shape[1])))


def _rwkv_layout(x):
    return jnp.concatenate([
        x[:, :OFF_WD],
        _pad_cols(x[:, 3 * A_W:3 * A_W + W_LORA], LANES),
        _pad_cols(x[:, 3 * A_W + W_LORA:3 * A_W + W_LORA + A_LORA], LANES),
        _pad_cols(x[:, 3 * A_W + W_LORA + A_LORA:RWKV_PROJ], 2 * LANES)], axis=1)


def _rwkv_unlayout(x):
    return jnp.concatenate([x[:, :OFF_WD + W_LORA], x[:, OFF_AD:OFF_AD + A_LORA],
                            x[:, OFF_GD:OFF_GD + G_LORA]], axis=1)


def _pad_rows(w, rows):
    return jnp.pad(w, ((0, rows - w.shape[0]), (0, 0)))


def kernel(x_prompt, x_sample, state_rwkv_shift, state_rwkv_wkv, state_retention, norm_mix_g, w_in, mu_shift, decay_w0, decay_up, iclr_a0, iclr_up, gate_up, k_k, k_a, r_k, lnx_g, lnx_b, w_out, norm_ffn_g, router_coarse, router_fine, exp_w_gate, exp_w_up, exp_w_down, norm_final_g):
    bp, tp, d = x_prompt.shape
    bs, ts, _ = x_sample.shape
    row = lambda t: t.reshape(1, -1)
    l = 0

    w_in_p = jnp.concatenate([_rwkv_layout(w_in[l][:, :RWKV_PROJ]), w_in[l][:, RWKV_PROJ:]],
                             axis=1).astype(BF16)
    rwkv_wts = (_rwkv_layout(row(mu_shift[l])), row(decay_w0[l]),
                _pad_rows(decay_up[l], LANES).astype(BF16), row(iclr_a0[l]),
                _pad_rows(iclr_up[l], LANES).astype(BF16),
                _pad_rows(gate_up[l], 2 * LANES).astype(BF16),
                row(k_k[l]), row(k_a[l]), row(r_k[l]), row(lnx_g[l]), row(lnx_b[l]))
    w_out_a = w_out[l][:A_W].astype(BF16)
    w_out_b = w_out[l][A_W:].astype(BF16)
    w_router = _pad_cols(jnp.concatenate(
        [router_coarse[l], jnp.transpose(router_fine[l], (1, 0, 2)).reshape(d, N_EXPERTS)], axis=1),
        LANES).astype(BF16)

    groups = (
        (x_prompt.reshape(bp * tp, d), bp, tp, WKV_CHUNK, RET_CHUNK, 1, 0.0,
         jnp.zeros((bp, RWKV_PAD), F32), jnp.zeros((bp, A_HEADS, A_HD, A_HD), F32),
         jnp.zeros((bp, R_HEADS, R_HD, R_HD), F32)),
        (x_sample.reshape(bs * ts, d), bs, ts, ts, math.gcd(ts, RET_CHUNK), 8, float(PAST_LEN),
         _rwkv_layout(state_rwkv_shift[l]), state_rwkv_wkv[l], state_retention[l]),
    )
    x_last = jnp.concatenate([x_prompt[:, -1, :], x_sample[:, -1, :]], axis=0)
    n_last = -(-(bp + bs) // 8) * 8
    shifts = _rwkv_unlayout(_inproj(_pad_rows(x_last, n_last), row(norm_mix_g[l]), w_in_p, F32,
                                    RWKV_PAD, n_last, RWKV_PAD // 2))
    new_shifts = (shifts[:bp], shifts[bp:bp + bs])

    mixed = []
    for (x, b, t, wkv_chunk, ret_chunk, nb, pos0, shift0, wkv0, ret0), new_shift in zip(groups, new_shifts):
        proj = _inproj(x, row(norm_mix_g[l]), w_in_p, BF16, PROJ_PAD, 1024, 1536)
        y_a, new_wkv = _rwkv(proj, shift0, wkv0, rwkv_wts, b, t, wkv_chunk, nb,
                             WKV_SUB if nb == 1 else 1)
        y_b, new_ret = _retention(proj, ret0, b, t, ret_chunk, nb, pos0)
        x1, h2, ri, rg = _outproj(x, y_a, y_b, w_out_a, w_out_b, row(norm_ffn_g[l]), w_router)
        mixed.append((x1, h2, ri, rg, new_shift, new_wkv, new_ret))

    n_tok = [m[0].shape[0] for m in mixed]
    n_all = sum(n_tok)
    n_blk = -(-2 * n_all // MOE_BLK) + N_EXPERTS
    dest, info = _sort(jnp.concatenate([m[2] for m in mixed], axis=0))
    n_blocks, first_block = info[0, :N_EXPERTS], info[1, :N_EXPERTS]
    xs = jnp.zeros((n_blk * MOE_BLK, d // 2), jnp.uint32)
    off = 0
    dests = []
    for m, n in zip(mixed, n_tok):
        d1, d2 = dest[off:off + n, 0], dest[off:off + n, 1]
        dests.append((d1, d2))
        xs = _dispatch(d1, d2, m[1], xs)
        off += n
    yb = _experts(n_blocks, first_block, xs, exp_w_gate[l], exp_w_up[l], exp_w_down[l])
    outs = [_combine(d1, d2, m[0], m[3], row(norm_final_g), yb)
            for m, (d1, d2) in zip(mixed, dests)]

    (_, _, _, _, p_shift, p_wkv, p_ret), (_, _, _, _, s_shift, s_wkv, s_ret) = mixed
    return (outs[0].reshape(bp, tp, d), outs[1].reshape(bs, ts, d),
            p_shift[None], p_wkv[None], p_ret[None], s_shift[None], s_wkv[None], s_ret[None])
```

```python
import functools
import math

import jax
import jax.numpy as jnp
from jax import lax
from jax.experimental import pallas as pl
from jax.experimental.pallas import tpu as pltpu

F32 = jnp.float32
BF16 = jnp.bfloat16

D_MODEL = 2048
A_W = 1024
A_HD = 64
A_HEADS = 16
W_LORA = 64
A_LORA = 64
G_LORA = 160
RWKV_PROJ = 3 * A_W + W_LORA + A_LORA + G_LORA
R_W = 1024
R_HEADS = 8
R_HD = 128
RET_CHUNK = 128
N_GROUPS = 4
E_PER_GROUP = 8
N_EXPERTS = 32
D_EXPERT = 512
NORM_EPS = 1e-6
LNX_EPS = 64e-5
DECAY_SCALE = math.exp(-0.5)
PAST_LEN = 16384

LANES = 128
OFF_WD = 3 * A_W
OFF_AD = OFF_WD + LANES
OFF_GD = OFF_AD + LANES
RWKV_PAD = OFF_GD + 2 * LANES
PROJ_PAD = RWKV_PAD + 4 * R_W
RET_BLK = 512
WKV_CHUNK = 64
WKV_SUB = 4
MOE_BLK = 256
W_AHEAD = 2
W_DMA_PRIORITY = 1
VMEM_LIMIT = 56 * 1024 * 1024


def _cparams(sem):
    return pltpu.CompilerParams(dimension_semantics=sem, vmem_limit_bytes=VMEM_LIMIT)


def _bdot(a, b):
    return jnp.dot(a.astype(BF16), b.astype(BF16), preferred_element_type=F32)


def _bdot_nt(a, b):
    return lax.dot_general(a.astype(BF16), b.astype(BF16), (((1,), (1,)), ((), ())),
                           preferred_element_type=F32)


def _bdot_tn(a, b):
    return lax.dot_general(a.astype(BF16), b.astype(BF16), (((0,), (0,)), ((), ())),
                           preferred_element_type=F32)


def _split3(x):
    hi = x.astype(BF16)
    r1 = x - hi.astype(F32)
    mid = r1.astype(BF16)
    lo = (r1 - mid.astype(F32)).astype(BF16)
    return hi, mid, lo


def _dot3(x, w_bf16):
    hi, mid, lo = _split3(x)
    d = lambda p: jnp.dot(p, w_bf16, preferred_element_type=F32)
    return d(hi) + d(mid) + d(lo)


def _sigmoid(x):
    return 1.0 / (1.0 + jnp.exp(-x))


def _pack_bf16_pairs(x):
    w = x.shape[1] // 2
    lo = lax.bitcast_convert_type(x[:, :w].astype(BF16).astype(F32), jnp.uint32)
    hi = lax.bitcast_convert_type(x[:, w:].astype(BF16).astype(F32), jnp.uint32)
    return hi | (lo >> 16)


def _unpack_bf16_pairs(p):
    lo = lax.bitcast_convert_type(p << 16, F32)
    hi = lax.bitcast_convert_type(p & jnp.uint32(0xFFFF0000), F32)
    return lo, hi


def _inproj_kernel(x_ref, g_ref, w_ref, o_ref, h_ref):
    @pl.when(pl.program_id(1) == 0)
    def _():
        x = x_ref[...]
        ms = jnp.mean(x * x, axis=-1, keepdims=True)
        h_ref[...] = (x * lax.rsqrt(ms + NORM_EPS) * g_ref[...]).astype(BF16)

    o_ref[...] = jnp.dot(h_ref[...], w_ref[...], preferred_element_type=F32).astype(o_ref.dtype)


def _inproj(x, g, w_bf16, out_dtype, n_cols, tm, tn):
    n, d = x.shape
    return pl.pallas_call(
        _inproj_kernel,
        grid=(n // tm, n_cols // tn),
        in_specs=[pl.BlockSpec((tm, d), lambda i, j: (i, 0)),
                  pl.BlockSpec((1, d), lambda i, j: (0, 0)),
                  pl.BlockSpec((d, tn), lambda i, j: (0, j))],
        out_specs=pl.BlockSpec((tm, tn), lambda i, j: (i, j)),
        out_shape=jax.ShapeDtypeStruct((n, n_cols), out_dtype),
        scratch_shapes=[pltpu.VMEM((tm, d), BF16)],
        compiler_params=_cparams(("parallel", "arbitrary")),
        name="inproj",
    )(x, g, w_bf16)


def _seg_sum(x):
    ones_bd = (lax.broadcasted_iota(jnp.int32, (LANES, LANES), 0) // A_HD ==
               lax.broadcasted_iota(jnp.int32, (LANES, LANES), 1) // A_HD).astype(BF16)
    nslab = x.shape[1] // LANES
    stacked = jnp.concatenate([x[:, s * LANES:(s + 1) * LANES] for s in range(nslab)], axis=0)
    red = _dot3(stacked, ones_bd)
    c = x.shape[0]
    return jnp.concatenate([red[s * c:(s + 1) * c] for s in range(nslab)], axis=1)


def _rwkv_pre(p, before, mu_ref, w0_ref, wup_ref, a0_ref, aup_ref, gup_ref, kk_ref, ka_ref, rk_ref,
              *, chunk, nb):
    C = chunk
    R = nb * C
    row = lax.broadcasted_iota(jnp.int32, (R, 1), 0)
    pos = row % C
    first = jnp.broadcast_to(before, (nb, C, RWKV_PAD)).reshape(R, RWKV_PAD)
    prev = jnp.where(pos == 0, first, pltpu.roll(p, 1, axis=0))
    xm = p + (prev - p) * mu_ref[...]

    r = xm[:, 0:A_W]
    k = xm[:, A_W:2 * A_W]
    v = xm[:, 2 * A_W:3 * A_W]
    wd = xm[:, OFF_WD:OFF_AD]
    ad = xm[:, OFF_AD:OFF_GD]
    gd = xm[:, OFF_GD:RWKV_PAD]

    logw = -DECAY_SCALE * _sigmoid(w0_ref[...] + _bdot(jnp.tanh(wd), wup_ref[...]))
    a = _sigmoid(a0_ref[...] + _bdot(ad, aup_ref[...]))
    g = _bdot(_sigmoid(gd), gup_ref[...])

    kk = k * kk_ref[...]
    k2 = k * (1.0 + (a - 1.0) * ka_ref[...])
    sums = _seg_sum(jnp.concatenate([kk * kk, r * k2 * rk_ref[...]], axis=0))
    kk = kk / jnp.maximum(jnp.sqrt(sums[0:R]), 1e-12)
    bonus = sums[R:2 * R]
    b = kk * a

    cum = logw
    suf = logw
    d = 1
    while d < C:
        cum = cum + jnp.where(pos >= d, pltpu.roll(cum, d, axis=0), 0.0)
        if nb > 1:
            suf = suf + jnp.where(pos < C - d, pltpu.roll(suf, R - d, axis=0), 0.0)
        d *= 2
    after = suf - logw if nb > 1 else cum[C - 1:C, :] - cum
    w_incl = jnp.exp(cum)
    w_excl = jnp.exp(cum - logw)
    w_inv = jnp.exp(-cum)
    w_end = jnp.exp(after)
    w_all = jnp.exp(cum + after)

    kk_d = kk * w_excl
    r_d = r * w_incl
    b_h = b * w_inv
    k_h = k2 * w_inv
    b_e = b * w_end
    k_e = k2 * w_end
    return kk_d, r_d, b_h, k_h, b_e, k_e, v, w_all, bonus, g


def _rwkv_mix(pre, sbd_ref, lng_ref, lnb_ref, *, chunk, nb):
    kk_d, r_d, b_h, k_h, b_e, k_e, v, w_all, bonus, g = pre
    C = chunk
    R = nb * C
    pairs = range(A_HEADS // 2)
    seqs = range(nb)
    in_a = lax.broadcasted_iota(jnp.int32, (1, LANES), 1) < A_HD
    R2 = 2 * R
    ri = lax.broadcasted_iota(jnp.int32, (R2, R2), 0)
    ci = lax.broadcasted_iota(jnp.int32, (R2, R2), 1)
    same = (ri // C) == (ci // C)
    strict = same & (ci < ri)
    incl = same & (ci <= ri)
    eye = (ri == ci).astype(F32)

    def expand(x):
        return jnp.concatenate([jnp.where(in_a, x, 0.0), jnp.where(in_a, 0.0, x)], axis=0)

    def seq_rows(x, s):
        if nb == 1:
            return x
        return jnp.concatenate([x[s * C:(s + 1) * C], x[R + s * C:R + (s + 1) * C]], axis=0)

    def from_seqs(parts):
        if nb == 1:
            return parts[0]
        return jnp.concatenate([q[0:C] for q in parts] + [q[C:2 * C] for q in parts], axis=0)

    sl = [slice(pr * LANES, (pr + 1) * LANES) for pr in pairs]
    ex = lambda x: [expand(x[:, sl[pr]]) for pr in pairs]
    e_kk, e_r, e_bh, e_kh, e_be, e_ke = ex(kk_d), ex(r_d), ex(b_h), ex(k_h), ex(b_e), ex(k_e)
    v_e = [t.astype(BF16) for t in ex(v)]
    kk_bf = [t.astype(BF16) for t in e_kk]
    r_bf = [t.astype(BF16) for t in e_r]
    gram = [_bdot_nt(jnp.concatenate([kk_bf[pr], r_bf[pr]], axis=0),
                     jnp.concatenate([e_bh[pr], e_kh[pr]], axis=0)) for pr in pairs]
    l_b = [jnp.where(strict, gram[pr][0:R2, 0:R2], 0.0) for pr in pairs]
    l_k = [jnp.where(strict, gram[pr][0:R2, R2:2 * R2], 0.0) for pr in pairs]
    a_bk = [jnp.concatenate([jnp.where(incl, gram[pr][R2:2 * R2, 0:R2], 0.0),
                             jnp.where(incl, gram[pr][R2:2 * R2, R2:2 * R2], 0.0)],
                            axis=1).astype(BF16) for pr in pairs]
    t_inv = [eye - l_b[pr] for pr in pairs]
    pw = [t.astype(BF16) for t in l_b]
    e = 2
    while e < C:
        pw = [_bdot(pw[pr], pw[pr]).astype(BF16) for pr in pairs]
        t_inv = [t_inv[pr] + _bdot(t_inv[pr], pw[pr]) for pr in pairs]
        e *= 2
    lkv = [_bdot(l_k[pr], v_e[pr]) for pr in pairs]
    xs_kk, xs_r = [], []
    for pr in pairs:
        parts = [_bdot_nt(jnp.concatenate([seq_rows(kk_bf[pr], s), seq_rows(r_bf[pr], s)], axis=0),
                          sbd_ref[s, pr]) for s in seqs]
        xs_kk.append(from_seqs([q[0:2 * C] for q in parts]))
        xs_r.append(from_seqs([q[2 * C:4 * C] for q in parts]))
    u_e = [-_bdot(t_inv[pr], xs_kk[pr] + lkv[pr]).astype(BF16) for pr in pairs]
    uv = [jnp.concatenate([u_e[pr], v_e[pr]], axis=0) for pr in pairs]
    y_e = [xs_r[pr] + _bdot(a_bk[pr], uv[pr]) for pr in pairs]
    y = jnp.concatenate([y_e[pr][0:R] + y_e[pr][R:R2] for pr in pairs], axis=1)
    for pr in pairs:
        be_bf, ke_bf = e_be[pr].astype(BF16), e_ke[pr].astype(BF16)
        for s in seqs:
            upd = _bdot_tn(jnp.concatenate([seq_rows(u_e[pr], s), seq_rows(v_e[pr], s)], axis=0),
                           jnp.concatenate([seq_rows(be_bf, s), seq_rows(ke_bf, s)], axis=0))
            sbd_ref[s, pr] = sbd_ref[s, pr] * w_all[s * C:s * C + 1, sl[pr]] + upd

    inv_hd = 1.0 / A_HD
    mean = _seg_sum(y) * inv_hd
    yc = y - mean
    var = _seg_sum(yc * yc) * inv_hd
    yn = yc * lax.rsqrt(var + LNX_EPS) * lng_ref[...] + lnb_ref[...]
    yn = yn + bonus * v
    return yn * g


def _rwkv_kernel(p_ref, shift_ref, s0_ref, *rest, chunk, nb, sub):
    wrefs, (y_ref, sout_ref, sbd_ref, carry_ref) = rest[:-4], rest[-4:]
    c_idx = pl.program_id(1)
    pairs = range(A_HEADS // 2)
    rows = nb * chunk

    @pl.when(c_idx == 0)
    def _():
        carry_ref[...] = shift_ref[...]
        z = jnp.zeros((A_HD, A_HD), F32)
        for s in range(nb):
            for pr in pairs:
                sa = s0_ref[s, 2 * pr]
                sb = s0_ref[s, 2 * pr + 1]
                sbd_ref[s, pr] = jnp.concatenate(
                    [jnp.concatenate([sa, z], axis=1), jnp.concatenate([z, sb], axis=1)], axis=0)

    p_all = p_ref[...].astype(F32)
    chunks = [p_all[i * rows:(i + 1) * rows] for i in range(sub)]
    befores = [carry_ref[...]] + [c[rows - 1:rows, :].reshape(1, 1, RWKV_PAD) for c in chunks]
    if nb == 1:
        carry_ref[...] = befores[sub]
    pre = [_rwkv_pre(chunks[i], befores[i], *wrefs[:-2], chunk=chunk, nb=nb) for i in range(sub)]
    for i in range(sub):
        y = _rwkv_mix(pre[i], sbd_ref, *wrefs[-2:], chunk=chunk, nb=nb)
        y_ref[i * rows:(i + 1) * rows, :] = y.astype(y_ref.dtype)

    @pl.when(c_idx == pl.num_programs(1) - 1)
    def _():
        for s in range(nb):
            for pr in pairs:
                s_bd = sbd_ref[s, pr]
                sout_ref[s, 2 * pr] = s_bd[0:A_HD, 0:A_HD]
                sout_ref[s, 2 * pr + 1] = s_bd[A_HD:2 * A_HD, A_HD:2 * A_HD]


def _rwkv(proj, shift0, s0, wts, batch, seq, chunk, nb, sub):
    n_chunks = seq // (chunk * sub)
    assert nb == 1 or (n_chunks == 1 and sub == 1)
    rows = nb * chunk * sub
    vec = lambda w: pl.BlockSpec(w.shape, lambda b, c: (0,) * w.ndim)
    st = pl.BlockSpec((nb, A_HEADS, A_HD, A_HD), lambda b, c: (b, 0, 0, 0))
    return pl.pallas_call(
        functools.partial(_rwkv_kernel, chunk=chunk, nb=nb, sub=sub),
        grid=(batch // nb, n_chunks),
        in_specs=[pl.BlockSpec((rows, RWKV_PAD), lambda b, c: (b * n_chunks + c, 0)),
                  pl.BlockSpec((nb, 1, RWKV_PAD), lambda b, c: (b, 0, 0)), st]
                 + [vec(w) for w in wts],
        out_specs=[pl.BlockSpec((rows, A_W), lambda b, c: (b * n_chunks + c, 0)), st],
        out_shape=[jax.ShapeDtypeStruct((batch * seq, A_W), BF16),
                   jax.ShapeDtypeStruct((batch, A_HEADS, A_HD, A_HD), F32)],
        scratch_shapes=[pltpu.VMEM((nb, A_HEADS // 2, LANES, LANES), F32),
                        pltpu.VMEM((nb, 1, RWKV_PAD), F32)],
        compiler_params=_cparams(("parallel", "arbitrary")),
        name="rwkv7",
    )(proj, shift0.reshape(batch, 1, RWKV_PAD), s0, *wts)


def _ret_kernel(q0_ref, q1_ref, k0_ref, k1_ref, v0_ref, v1_ref, g0_ref, g1_ref, cos_ref, sin_ref,
                din_ref, dq_ref, dk_ref, dc_ref, r0_ref, y_ref, rout_ref, r_scr, *, chunk, nb):
    C = chunk
    half = RET_BLK // R_HD
    c_idx = pl.program_id(1)

    @pl.when(c_idx == 0)
    def _():
        r_scr[...] = r0_ref[...]

    even = lax.broadcasted_iota(jnp.int32, (1, R_HD), 1) % 2 == 0
    cos = cos_ref[...]
    sin = sin_ref[...]

    def rot(t):
        partner = jnp.where(even, pltpu.roll(t, R_HD - 1, axis=1), pltpu.roll(t, 1, axis=1))
        return t * cos + partner * sin

    def load(halves, s, h):
        return halves[h // half][s * C:(s + 1) * C, (h % half) * R_HD:(h % half + 1) * R_HD]

    f32 = lambda *refs: tuple(r[...].astype(F32) for r in refs)
    items = [(s, h) for s in range(nb) for h in range(R_HEADS)]
    q_in, k_in, v_in, g_in = f32(q0_ref, q1_ref), f32(k0_ref, k1_ref), f32(v0_ref, v1_ref), f32(g0_ref, g1_ref)
    q = [rot(load(q_in, s, h)) for s, h in items]
    k = [rot(load(k_in, s, h)) * (R_HD ** -0.5) for s, h in items]
    v = [load(v_in, s, h).astype(BF16) for s, h in items]
    state = [r_scr[s, h] for s, h in items]
    scores = [_bdot_nt(q[i], k[i]) * din_ref[h] for i, (s, h) in enumerate(items)]
    o = [_bdot(jnp.concatenate([scores[i], q[i] * dq_ref[h]], axis=1),
               jnp.concatenate([v[i], state[i].astype(BF16)], axis=0))
         for i, (s, h) in enumerate(items)]
    for i, (s, h) in enumerate(items):
        r_scr[s, h] = state[i] * dc_ref[h, 0:1, :] + _bdot_tn(k[i] * dk_ref[h], v[i])
    for i, (s, h) in enumerate(items):
        on = o[i] * lax.rsqrt(jnp.mean(o[i] * o[i], axis=-1, keepdims=True) + NORM_EPS)
        gate = load(g_in, s, h)
        y_ref[s * C:(s + 1) * C, h * R_HD:(h + 1) * R_HD] = (
            gate * _sigmoid(gate) * on).astype(y_ref.dtype)

    @pl.when(c_idx == pl.num_programs(1) - 1)
    def _():
        rout_ref[...] = r_scr[...]


def _ret_tables(chunk, pos0, seq):
    pos = pos0 + jnp.arange(seq, dtype=F32)
    theta = 1.0 / (10000.0 ** jnp.linspace(0.0, 1.0, R_HD // 2, dtype=F32))
    ang = pos[:, None] * theta[None, :]
    cos2 = jnp.repeat(jnp.cos(ang), 2, axis=1)
    sin2 = jnp.stack([-jnp.sin(ang), jnp.sin(ang)], axis=-1).reshape(seq, R_HD)
    lg = jnp.log1p(-(2.0 ** (-5.0 - jnp.arange(R_HEADS, dtype=F32))))
    idx = jnp.arange(chunk, dtype=F32)
    diff = idx[:, None] - idx[None, :]
    causal = diff >= 0
    din = jnp.where(causal, jnp.exp(jnp.where(causal, diff, 0.0)[None] * lg[:, None, None]), 0.0)
    bc = lambda t: jnp.broadcast_to(t[:, :, None], (R_HEADS, t.shape[1], R_HD))
    dq = bc(jnp.exp((idx + 1.0)[None, :] * lg[:, None]))
    dk = bc(jnp.exp((chunk - 1.0 - idx)[None, :] * lg[:, None]))
    dc = jnp.broadcast_to(jnp.exp(chunk * lg)[:, None, None], (R_HEADS, 8, R_HD))
    return cos2, sin2, din, dq, dk, dc


def _retention(proj, r0, batch, seq, chunk, nb, pos0):
    n_chunks = seq // chunk
    cos2, sin2, din, dq, dk, dc = _ret_tables(chunk, pos0, seq)
    col0 = RWKV_PAD // RET_BLK
    rows = nb * chunk

    def sec(i):
        return pl.BlockSpec((rows, RET_BLK), lambda b, c: (b * n_chunks + c, col0 + i))

    tab = lambda t: pl.BlockSpec(t.shape, lambda b, c: (0, 0, 0))
    st = pl.BlockSpec((nb, R_HEADS, R_HD, R_HD), lambda b, c: (b, 0, 0, 0))
    return pl.pallas_call(
        functools.partial(_ret_kernel, chunk=chunk, nb=nb),
        grid=(batch // nb, n_chunks),
        in_specs=[sec(i) for i in range(8)]
                 + [pl.BlockSpec((chunk, R_HD), lambda b, c: (c, 0)),
                    pl.BlockSpec((chunk, R_HD), lambda b, c: (c, 0)),
                    tab(din), tab(dq), tab(dk), tab(dc), st],
        out_specs=[pl.BlockSpec((rows, R_W), lambda b, c: (b * n_chunks + c, 0)), st],
        out_shape=[jax.ShapeDtypeStruct((batch * seq, R_W), BF16),
                   jax.ShapeDtypeStruct((batch, R_HEADS, R_HD, R_HD), F32)],
        scratch_shapes=[pltpu.VMEM((nb, R_HEADS, R_HD, R_HD), F32)],
        compiler_params=_cparams(("parallel", "arbitrary")),
        name="retention",
    )(*([proj] * 8), cos2, sin2, din, dq, dk, dc, r0)


def _outproj_kernel(x_ref, ya_ref, yb_ref, wa_ref, wb_ref, g_ref, wr_ref,
                    x1_ref, h2_ref, ri_ref, rg_ref, *, parts):
    rows = x_ref.shape[0] // parts
    sl = [slice(i * rows, (i + 1) * rows) for i in range(parts)]
    x1s = [x_ref[s, :] + jnp.dot(ya_ref[s, :], wa_ref[...], preferred_element_type=F32)
           + jnp.dot(yb_ref[s, :], wb_ref[...], preferred_element_type=F32) for s in sl]
    for s, x1 in zip(sl, x1s):
        x1_ref[s, :] = x1
        ms = jnp.mean(x1 * x1, axis=-1, keepdims=True)
        h2 = x1 * lax.rsqrt(ms + NORM_EPS) * g_ref[...]
        h2_ref[s, :] = _pack_bf16_pairs(h2)
        ri_ref[s, :], rg_ref[s, :] = _route(_bdot(h2, wr_ref[...]))


def _route(logits):
    lane = lax.broadcasted_iota(jnp.int32, (1, LANES), 1)
    neg = -1e30
    big = LANES
    rmax = lambda t: jnp.max(t, axis=-1, keepdims=True)
    rmin = lambda t: jnp.min(t, axis=-1, keepdims=True)
    is_c = lane < N_GROUPS
    lc = jnp.where(is_c, logits, neg)
    mc = rmax(lc)
    grp = rmin(jnp.where(lc == mc, lane, big))
    p_grp = 1.0 / jnp.sum(jnp.where(is_c, jnp.exp(lc - mc), 0.0), axis=-1, keepdims=True)
    fine = lane - N_GROUPS
    in_g = (fine >= 0) & (fine < N_EXPERTS) & ((fine // E_PER_GROUP) == grp)
    lf = jnp.where(in_g, logits, neg)
    m1 = rmax(lf)
    i1 = rmin(jnp.where(lf == m1, lane, big))
    lf2 = jnp.where(lane == i1, neg, lf)
    m2 = rmax(lf2)
    i2 = rmin(jnp.where(lf2 == m2, lane, big))
    e2 = jnp.exp(m2 - m1)
    g1 = p_grp / (1.0 + e2)
    g2 = p_grp * e2 / (1.0 + e2)
    return (jnp.where(lane == 0, i1 - N_GROUPS, jnp.where(lane == 1, i2 - N_GROUPS, 0)),
            jnp.where(lane == 0, g1, jnp.where(lane == 1, g2, 0.0)))


def _outproj(x, ya, yb, wa, wb, g, wr, tm=512, parts=2):
    n, d = x.shape
    rowblk = lambda w: pl.BlockSpec((tm, w), lambda i: (i, 0))
    full = lambda w: pl.BlockSpec(w.shape, lambda i: (0, 0))
    return pl.pallas_call(
        functools.partial(_outproj_kernel, parts=parts),
        grid=(n // tm,),
        in_specs=[rowblk(d), rowblk(A_W), rowblk(R_W), full(wa), full(wb), full(g), full(wr)],
        out_specs=[rowblk(d), rowblk(d // 2), rowblk(LANES), rowblk(LANES)],
        out_shape=[jax.ShapeDtypeStruct((n, d), F32), jax.ShapeDtypeStruct((n, d // 2), jnp.uint32),
                   jax.ShapeDtypeStruct((n, LANES), jnp.int32),
                   jax.ShapeDtypeStruct((n, LANES), F32)],
        compiler_params=_cparams(("parallel",)),
        name="outproj_router",
    )(x, ya, yb, wa, wb, g, wr)


def _sort_kernel(ri_ref, tri_ref, dest_ref, info_ref, cnt_ref, base_ref):
    ph = pl.program_id(0)
    i = pl.program_id(1)
    lane = lax.broadcasted_iota(jnp.int32, (1, LANES), 1)
    ri = ri_ref[...]
    oh1 = (lane == ri[:, 0:1]).astype(F32)
    oh2 = (lane == ri[:, 1:2]).astype(F32)
    tot1 = jnp.sum(oh1, axis=0, keepdims=True)
    tot2 = jnp.sum(oh2, axis=0, keepdims=True)

    @pl.when(ph == 0)
    def _():
        @pl.when(i == 0)
        def _():
            cnt_ref[...] = jnp.zeros_like(cnt_ref)
        cnt_ref[...] += tot1 + tot2

    @pl.when(ph == 1)
    def _():
        @pl.when(i == 0)
        def _():
            nblk = jnp.floor((cnt_ref[...] + (MOE_BLK - 1)) * (1.0 / MOE_BLK))
            upper = (lax.broadcasted_iota(jnp.int32, (LANES, LANES), 0) <=
                     lax.broadcasted_iota(jnp.int32, (LANES, LANES), 1))
            pend = _bdot(jnp.broadcast_to(nblk, (8, LANES)), upper.astype(F32))[0:1]
            first = pend - nblk
            base_ref[...] = first * MOE_BLK
            srow = lax.broadcasted_iota(jnp.int32, (8, LANES), 0)
            info_ref[...] = jnp.where(srow == 0, nblk, jnp.where(srow == 1, first, 0.0)).astype(jnp.int32)

        base = base_ref[...]
        tri = tri_ref[...]
        pre1 = jnp.dot(tri, oh1.astype(BF16), preferred_element_type=F32)
        pre2 = jnp.dot(tri, oh2.astype(BF16), preferred_element_type=F32) + tot1
        d1 = jnp.sum(oh1 * (base + pre1), axis=1, keepdims=True)
        d2 = jnp.sum(oh2 * (base + pre2), axis=1, keepdims=True)
        dest_ref[...] = jnp.where(lane == 0, d1, jnp.where(lane == 1, d2, 0.0)).astype(jnp.int32)
        base_ref[...] = base + tot1 + tot2


def _sort(ri, tt=1024):
    n = ri.shape[0]
    tri = (jnp.arange(tt)[:, None] > jnp.arange(tt)[None, :]).astype(BF16)
    return pl.pallas_call(
        _sort_kernel,
        grid=(2, n // tt),
        in_specs=[pl.BlockSpec((tt, LANES), lambda ph, i: (i, 0)),
                  pl.BlockSpec((tt, tt), lambda ph, i: (0, 0))],
        out_specs=[pl.BlockSpec((tt, LANES), lambda ph, i: (i * ph, 0)),
                   pl.BlockSpec((8, LANES), lambda ph, i: (0, 0))],
        out_shape=[jax.ShapeDtypeStruct((n, LANES), jnp.int32),
                   jax.ShapeDtypeStruct((8, LANES), jnp.int32)],
        scratch_shapes=[pltpu.VMEM((1, LANES), F32), pltpu.VMEM((1, LANES), F32)],
        compiler_params=_cparams(("arbitrary", "arbitrary")),
        name="moe_sort",
    )(ri, tri)


def _row_copy(src_ref, src_row, dst_ref, dst_row, sem):
    return pltpu.make_async_copy(src_ref.at[pl.ds(src_row, 1)], dst_ref.at[pl.ds(dst_row, 1)], sem)


def _dispatch_kernel(d1_ref, d2_ref, h_ref, xs_in_ref, xs_ref, sem, *, tt):
    del xs_in_ref

    def issue(t, carry):
        _row_copy(h_ref, t, xs_ref, d1_ref[t], sem).start()
        _row_copy(h_ref, t, xs_ref, d2_ref[t], sem).start(priority=1)
        return carry

    def drain(t, carry):
        _row_copy(h_ref, t, xs_ref, d1_ref[t], sem).wait()
        _row_copy(h_ref, t, xs_ref, d2_ref[t], sem).wait()
        return carry

    lax.fori_loop(0, tt, issue, 0, unroll=8)
    lax.fori_loop(0, tt, drain, 0, unroll=8)


def _dispatch(d1, d2, h2, xs, tt=1024):
    n, d = h2.shape
    smem = pl.BlockSpec((tt,), lambda i: (i,), memory_space=pltpu.SMEM)
    anyspec = pl.BlockSpec(memory_space=pl.ANY)
    return pl.pallas_call(
        functools.partial(_dispatch_kernel, tt=tt),
        grid=(n // tt,),
        in_specs=[smem, smem, pl.BlockSpec((tt, d), lambda i: (i, 0)), anyspec],
        out_specs=anyspec,
        out_shape=jax.ShapeDtypeStruct(xs.shape, xs.dtype),
        scratch_shapes=[pltpu.SemaphoreType.DMA(())],
        input_output_aliases={3: 0},
        compiler_params=_cparams(("arbitrary",)),
        name="moe_dispatch",
    )(d1, d2, h2, xs)


def _expert_kernel(nb_ref, fb_ref, xs_hbm, wg_hbm, wu_hbm, wd_hbm, y_hbm,
                   wg_r, wu_r, wd_r, wg_s, wu_s, wd_s, xbuf, ybuf, sem_w, sem_in, sem_out, *, n_blk):
    e = pl.program_id(0)
    last = pl.num_programs(0) - 1
    n = nb_ref[e]
    b0 = fb_ref[e]
    n_used = fb_ref[last] + nb_ref[last]
    half = xbuf.shape[2]
    ring = wg_r.shape[0]

    def fetch(ex):
        slot = ex % ring
        return [pltpu.make_async_copy(src.at[ex], dst.at[slot], sem_w.at[slot])
                for src, dst in ((wg_hbm, wg_r), (wu_hbm, wu_r), (wd_hbm, wd_r))]

    @pl.when(e == 0)
    def _():
        for ex in range(ring - 1):
            for cp in fetch(ex):
                cp.start(priority=W_DMA_PRIORITY)

    @pl.when(e + ring - 1 <= last)
    def _():
        for cp in fetch(e + ring - 1):
            cp.start(priority=W_DMA_PRIORITY)

    for cp in fetch(e):
        cp.wait()
    w_slot = e % ring

    def load(blk, slot):
        return pltpu.make_async_copy(xs_hbm.at[pl.ds(blk * MOE_BLK, MOE_BLK)],
                                     xbuf.at[slot], sem_in.at[slot])

    def store(blk, slot):
        return pltpu.make_async_copy(ybuf.at[slot], y_hbm.at[pl.ds(blk * MOE_BLK, MOE_BLK)],
                                     sem_out.at[slot])

    @pl.when(n > 0)
    def _():
        @pl.when(b0 == 0)
        def _():
            load(0, 0).start()

        wg_s[...] = wg_r[w_slot].astype(BF16)
        wu_s[...] = wu_r[w_slot].astype(BF16)
        wd_s[...] = wd_r[w_slot].astype(BF16)

        def body(blk, carry):
            slot = blk % 2
            load(blk, slot).wait()

            @pl.when(blk + 1 < n_used)
            def _():
                load(blk + 1, 1 - slot).start()

            @pl.when(blk >= 2)
            def _():
                store(blk - 2, slot).wait()

            x_lo, x_hi = _unpack_bf16_pairs(xbuf[slot])
            x_lo, x_hi = x_lo.astype(BF16), x_hi.astype(BF16)
            mm = lambda w_s: (jnp.dot(x_lo, w_s[0:half], preferred_element_type=F32)
                              + jnp.dot(x_hi, w_s[half:2 * half], preferred_element_type=F32))
            hg = mm(wg_s)
            hu = mm(wu_s)
            act = (hg * _sigmoid(hg) * hu).astype(BF16)
            ybuf[slot] = _pack_bf16_pairs(jnp.dot(act, wd_s[...], preferred_element_type=F32))
            store(blk, slot).start()
            return carry

        lax.fori_loop(b0, b0 + n, body, 0)

    @pl.when(e == last)
    def _():
        @pl.when(n_used >= 2)
        def _():
            store(n_used - 2, n_used % 2).wait()

        @pl.when(n_used >= 1)
        def _():
            store(n_used - 1, (n_used - 1) % 2).wait()

        ybuf[0] = jnp.zeros(ybuf.shape[1:], ybuf.dtype)

        def fill(blk, carry):
            cp = store(blk, 0)
            cp.start()
            cp.wait()
            return carry

        lax.fori_loop(n_used, n_blk, fill, 0)


def _experts(n_blocks, first_block, xs, wg, wu, wd):
    p, half = xs.shape
    d = 2 * half
    anyspec = pl.BlockSpec(memory_space=pl.ANY)
    ring = W_AHEAD + 1
    grid_spec = pltpu.PrefetchScalarGridSpec(
        num_scalar_prefetch=2,
        grid=(N_EXPERTS,),
        in_specs=[anyspec] * 4,
        out_specs=anyspec,
        scratch_shapes=[pltpu.VMEM((ring, d, D_EXPERT), F32), pltpu.VMEM((ring, d, D_EXPERT), F32),
                        pltpu.VMEM((ring, D_EXPERT, d), F32),
                        pltpu.VMEM((d, D_EXPERT), BF16), pltpu.VMEM((d, D_EXPERT), BF16),
                        pltpu.VMEM((D_EXPERT, d), BF16),
                        pltpu.VMEM((2, MOE_BLK, half), jnp.uint32),
                        pltpu.VMEM((2, MOE_BLK, half), jnp.uint32),
                        pltpu.SemaphoreType.DMA((ring,)),
                        pltpu.SemaphoreType.DMA((2,)), pltpu.SemaphoreType.DMA((2,))])
    return pl.pallas_call(
        functools.partial(_expert_kernel, n_blk=p // MOE_BLK),
        grid_spec=grid_spec,
        out_shape=jax.ShapeDtypeStruct((p, half), jnp.uint32),
        compiler_params=_cparams(("arbitrary",)),
        name="moe_experts",
    )(n_blocks, first_block, xs, wg, wu, wd)


def _combine_kernel(d1_ref, d2_ref, d1n_ref, d2n_ref, x1_ref, rg_ref, g_ref, yb_ref, o_ref, buf, sem,
                    *, tt):
    i = pl.program_id(0)
    slot = i % 2

    def gather(a_ref, b_ref, s, wait):
        def body(t, carry):
            for j, idx in enumerate((a_ref, b_ref)):
                cp = _row_copy(yb_ref, idx[t], buf.at[s, j], t, sem.at[s])
                cp.wait() if wait else cp.start(priority=j)
            return carry
        lax.fori_loop(0, tt, body, 0, unroll=8)

    @pl.when(i == 0)
    def _():
        gather(d1_ref, d2_ref, 0, False)

    @pl.when(i + 1 < pl.num_programs(0))
    def _():
        gather(d1n_ref, d2n_ref, 1 - slot, False)

    gather(d1_ref, d2_ref, slot, True)
    rg = rg_ref[...]
    lo1, hi1 = _unpack_bf16_pairs(buf[slot, 0])
    lo2, hi2 = _unpack_bf16_pairs(buf[slot, 1])
    g1, g2 = rg[:, 0:1], rg[:, 1:2]
    x = x1_ref[...] + jnp.concatenate([g1 * lo1 + g2 * lo2, g1 * hi1 + g2 * hi2], axis=1)
    ms = jnp.mean(x * x, axis=-1, keepdims=True)
    o_ref[...] = x * lax.rsqrt(ms + NORM_EPS) * g_ref[...]


def _combine(d1, d2, x1, rg, g, yb, tt=512):
    n, d = x1.shape
    steps = n // tt
    smem = pl.BlockSpec((tt,), lambda i: (i,), memory_space=pltpu.SMEM)
    smem_next = pl.BlockSpec((tt,), lambda i: (jnp.minimum(i + 1, steps - 1),),
                             memory_space=pltpu.SMEM)
    rowblk = lambda w: pl.BlockSpec((tt, w), lambda i: (i, 0))
    return pl.pallas_call(
        functools.partial(_combine_kernel, tt=tt),
        grid=(steps,),
        in_specs=[smem, smem, smem_next, smem_next, rowblk(d), rowblk(LANES),
                  pl.BlockSpec((1, d), lambda i: (0, 0)), pl.BlockSpec(memory_space=pl.ANY)],
        out_specs=rowblk(d),
        out_shape=jax.ShapeDtypeStruct((n, d), F32),
        scratch_shapes=[pltpu.VMEM((2, 2, tt, d // 2), jnp.uint32), pltpu.SemaphoreType.DMA((2,))],
        compiler_params=_cparams(("arbitrary",)),
        name="moe_combine",
    )(d1, d2, d1, d2, x1, rg, g, yb)


def _pad_cols(x, width):
    return jnp.pad(x, ((0, 0), (0, width - x.shape[1])))


def _rwkv_layout(x):
    return jnp.concatenate([
        x[:, :OFF_WD],
        _pad_cols(x[:, 3 * A_W:3 * A_W + W_LORA], LANES),
        _pad_cols(x[:, 3 * A_W + W_LORA:3 * A_W + W_LORA + A_LORA], LANES),
        _pad_cols(x[:, 3 * A_W + W_LORA + A_LORA:RWKV_PROJ], 2 * LANES)], axis=1)


def _rwkv_unlayout(x):
    return jnp.concatenate([x[:, :OFF_WD + W_LORA], x[:, OFF_AD:OFF_AD + A_LORA],
                            x[:, OFF_GD:OFF_GD + G_LORA]], axis=1)


def _pad_rows(w, rows):
    return jnp.pad(w, ((0, rows - w.shape[0]), (0, 0)))


def kernel(x_prompt, x_sample, state_rwkv_shift, state_rwkv_wkv, state_retention, norm_mix_g, w_in, mu_shift, decay_w0, decay_up, iclr_a0, iclr_up, gate_up, k_k, k_a, r_k, lnx_g, lnx_b, w_out, norm_ffn_g, router_coarse, router_fine, exp_w_gate, exp_w_up, exp_w_down, norm_final_g):
    bp, tp, d = x_prompt.shape
    bs, ts, _ = x_sample.shape
    row = lambda t: t.reshape(1, -1)
    l = 0

    w_in_p = jnp.concatenate([_rwkv_layout(w_in[l][:, :RWKV_PROJ]), w_in[l][:, RWKV_PROJ:]],
                             axis=1).astype(BF16)
    rwkv_wts = (_rwkv_layout(row(mu_shift[l])), row(decay_w0[l]),
                _pad_rows(decay_up[l], LANES).astype(BF16), row(iclr_a0[l]),
                _pad_rows(iclr_up[l], LANES).astype(BF16),
                _pad_rows(gate_up[l], 2 * LANES).astype(BF16),
                row(k_k[l]), row(k_a[l]), row(r_k[l]), row(lnx_g[l]), row(lnx_b[l]))
    w_out_a = w_out[l][:A_W].astype(BF16)
    w_out_b = w_out[l][A_W:].astype(BF16)
    w_router = _pad_cols(jnp.concatenate(
        [router_coarse[l], jnp.transpose(router_fine[l], (1, 0, 2)).reshape(d, N_EXPERTS)], axis=1),
        LANES).astype(BF16)

    groups = (
        (x_prompt.reshape(bp * tp, d), bp, tp, WKV_CHUNK, RET_CHUNK, 1, 0.0,
         jnp.zeros((bp, RWKV_PAD), F32), jnp.zeros((bp, A_HEADS, A_HD, A_HD), F32),
         jnp.zeros((bp, R_HEADS, R_HD, R_HD), F32)),
        (x_sample.reshape(bs * ts, d), bs, ts, ts, math.gcd(ts, RET_CHUNK), 8, float(PAST_LEN),
         _rwkv_layout(state_rwkv_shift[l]), state_rwkv_wkv[l], state_retention[l]),
    )
    x_last = jnp.concatenate([x_prompt[:, -1, :], x_sample[:, -1, :]], axis=0)
    n_last = -(-(bp + bs) // 8) * 8
    shifts = _rwkv_unlayout(_inproj(_pad_rows(x_last, n_last), row(norm_mix_g[l]), w_in_p, F32,
                                    RWKV_PAD, n_last, RWKV_PAD // 2))
    new_shifts = (shifts[:bp], shifts[bp:bp + bs])

    mixed = []
    for (x, b, t, wkv_chunk, ret_chunk, nb, pos0, shift0, wkv0, ret0), new_shift in zip(groups, new_shifts):
        proj = _inproj(x, row(norm_mix_g[l]), w_in_p, BF16, PROJ_PAD, 1024, 1920)
        y_a, new_wkv = _rwkv(proj, shift0, wkv0, rwkv_wts, b, t, wkv_chunk, nb,
                             WKV_SUB if nb == 1 else 1)
        y_b, new_ret = _retention(proj, ret0, b, t, ret_chunk, nb, pos0)
        x1, h2, ri, rg = _outproj(x, y_a, y_b, w_out_a, w_out_b, row(norm_ffn_g[l]), w_router)
        mixed.append((x1, h2, ri, rg, new_shift, new_wkv, new_ret))

    n_tok = [m[0].shape[0] for m in mixed]
    n_all = sum(n_tok)
    n_blk = -(-2 * n_all // MOE_BLK) + N_EXPERTS
    dest, info = _sort(jnp.concatenate([m[2] for m in mixed], axis=0))
    n_blocks, first_block = info[0, :N_EXPERTS], info[1, :N_EXPERTS]
    xs = jnp.zeros((n_blk * MOE_BLK, d // 2), jnp.uint32)
    off = 0
    dests = []
    for m, n in zip(mixed, n_tok):
        d1, d2 = dest[off:off + n, 0], dest[off:off + n, 1]
        dests.append((d1, d2))
        xs = _dispatch(d1, d2, m[1], xs)
        off += n
    yb = _experts(n_blocks, first_block, xs, exp_w_gate[l], exp_w_up[l], exp_w_down[l])
    outs = [_combine(d1, d2, m[0], m[3], row(norm_final_g), yb)
            for m, (d1, d2) in zip(mixed, dests)]

    (_, _, _, _, p_shift, p_wkv, p_ret), (_, _, _, _, s_shift, s_wkv, s_ret) = mixed
    return (outs[0].reshape(bp, tp, d), outs[1].reshape(bs, ts, d),
            p_shift[None], p_wkv[None], p_ret[None], s_shift[None], s_wkv[None], s_ret[None])
```

```python
import functools
import math

import jax
import jax.numpy as jnp
import numpy as np
from jax import lax
from jax.experimental import pallas as pl
from jax.experimental.pallas import tpu as pltpu

F32 = jnp.float32
BF16 = jnp.bfloat16

D_MODEL = 2048
A_W = 1024
A_HD = 64
A_HEADS = 16
W_LORA = 64
A_LORA = 64
G_LORA = 160
RWKV_PROJ = 3 * A_W + W_LORA + A_LORA + G_LORA
R_W = 1024
R_HEADS = 8
R_HD = 128
RET_CHUNK = 128
N_GROUPS = 4
E_PER_GROUP = 8
N_EXPERTS = 32
D_EXPERT = 512
NORM_EPS = 1e-6
LNX_EPS = 64e-5
DECAY_SCALE = math.exp(-0.5)
PAST_LEN = 16384

LANES = 128
OFF_WD = 3 * A_W
OFF_AD = OFF_WD + LANES
OFF_GD = OFF_AD + LANES
RWKV_PAD = OFF_GD + 2 * LANES
PROJ_PAD = RWKV_PAD + 4 * R_W
RET_BLK = 512
WKV_CHUNK = 64
WKV_SUB = 4
WKV_GROUPS = 2
MOE_BLK = 256
W_AHEAD = 2
W_DMA_PRIORITY = 1
VMEM_LIMIT = 56 * 1024 * 1024


def _cparams(sem):
    return pltpu.CompilerParams(dimension_semantics=sem, vmem_limit_bytes=VMEM_LIMIT)


def _bdot(a, b):
    return jnp.dot(a.astype(BF16), b.astype(BF16), preferred_element_type=F32)


def _bdot_nt(a, b):
    return lax.dot_general(a.astype(BF16), b.astype(BF16), (((1,), (1,)), ((), ())),
                           preferred_element_type=F32)


def _bdot_tn(a, b):
    return lax.dot_general(a.astype(BF16), b.astype(BF16), (((0,), (0,)), ((), ())),
                           preferred_element_type=F32)


def _split3(x):
    hi = x.astype(BF16)
    r1 = x - hi.astype(F32)
    mid = r1.astype(BF16)
    lo = (r1 - mid.astype(F32)).astype(BF16)
    return hi, mid, lo


def _dot3(x, w_bf16):
    hi, mid, lo = _split3(x)
    d = lambda p: jnp.dot(p, w_bf16, preferred_element_type=F32)
    return d(hi) + d(mid) + d(lo)


def _sigmoid(x):
    return 1.0 / (1.0 + jnp.exp(-x))


def _pack_bf16_pairs(x):
    w = x.shape[1] // 2
    lo = lax.bitcast_convert_type(x[:, :w].astype(BF16).astype(F32), jnp.uint32)
    hi = lax.bitcast_convert_type(x[:, w:].astype(BF16).astype(F32), jnp.uint32)
    return hi | (lo >> 16)


def _unpack_bf16_pairs(p):
    lo = lax.bitcast_convert_type(p << 16, F32)
    hi = lax.bitcast_convert_type(p & jnp.uint32(0xFFFF0000), F32)
    return lo, hi


def _inproj_kernel(x_ref, g_ref, w_ref, o_ref, h_ref):
    @pl.when(pl.program_id(1) == 0)
    def _():
        x = x_ref[...]
        ms = jnp.mean(x * x, axis=-1, keepdims=True)
        h_ref[...] = (x * lax.rsqrt(ms + NORM_EPS) * g_ref[...]).astype(BF16)

    o_ref[...] = jnp.dot(h_ref[...], w_ref[...], preferred_element_type=F32).astype(o_ref.dtype)


def _inproj(x, g, w_bf16, out_dtype, n_cols, tm, tn):
    n, d = x.shape
    return pl.pallas_call(
        _inproj_kernel,
        grid=(n // tm, n_cols // tn),
        in_specs=[pl.BlockSpec((tm, d), lambda i, j: (i, 0)),
                  pl.BlockSpec((1, d), lambda i, j: (0, 0)),
                  pl.BlockSpec((d, tn), lambda i, j: (0, j))],
        out_specs=pl.BlockSpec((tm, tn), lambda i, j: (i, j)),
        out_shape=jax.ShapeDtypeStruct((n, n_cols), out_dtype),
        scratch_shapes=[pltpu.VMEM((tm, d), BF16)],
        compiler_params=_cparams(("parallel", "arbitrary")),
        name="inproj",
    )(x, g, w_bf16)


def _seg_sum(x):
    ones_bd = (lax.broadcasted_iota(jnp.int32, (LANES, LANES), 0) // A_HD ==
               lax.broadcasted_iota(jnp.int32, (LANES, LANES), 1) // A_HD).astype(BF16)
    nslab = x.shape[1] // LANES
    stacked = jnp.concatenate([x[:, s * LANES:(s + 1) * LANES] for s in range(nslab)], axis=0)
    red = _dot3(stacked, ones_bd)
    c = x.shape[0]
    return jnp.concatenate([red[s * c:(s + 1) * c] for s in range(nslab)], axis=1)


def _rwkv_pre(p, before, mu_ref, w0_ref, wup_ref, a0_ref, aup_ref, gup_ref, kk_ref, ka_ref, rk_ref,
              *, chunk, nb):
    C = chunk
    R = nb * C
    row = lax.broadcasted_iota(jnp.int32, (R, 1), 0)
    pos = row % C
    first = jnp.broadcast_to(before, (nb, C, RWKV_PAD)).reshape(R, RWKV_PAD)
    prev = jnp.where(pos == 0, first, pltpu.roll(p, 1, axis=0))
    xm = p + (prev - p) * mu_ref[...]

    r = xm[:, 0:A_W]
    k = xm[:, A_W:2 * A_W]
    v = xm[:, 2 * A_W:3 * A_W]
    wd = xm[:, OFF_WD:OFF_AD]
    ad = xm[:, OFF_AD:OFF_GD]
    gd = xm[:, OFF_GD:RWKV_PAD]

    logw = -DECAY_SCALE * _sigmoid(w0_ref[...] + _bdot(jnp.tanh(wd), wup_ref[...]))
    a = _sigmoid(a0_ref[...] + _bdot(ad, aup_ref[...]))
    g = _bdot(_sigmoid(gd), gup_ref[...])

    kk = k * kk_ref[...]
    k2 = k * (1.0 + (a - 1.0) * ka_ref[...])
    sums = _seg_sum(jnp.concatenate([kk * kk, r * k2 * rk_ref[...]], axis=0))
    kk = kk / jnp.maximum(jnp.sqrt(sums[0:R]), 1e-12)
    bonus = sums[R:2 * R]
    b = kk * a

    cum = logw
    suf = logw
    d = 1
    while d < C:
        cum = cum + jnp.where(pos >= d, pltpu.roll(cum, d, axis=0), 0.0)
        if nb > 1:
            suf = suf + jnp.where(pos < C - d, pltpu.roll(suf, R - d, axis=0), 0.0)
        d *= 2
    after = suf - logw if nb > 1 else cum[C - 1:C, :] - cum
    w_incl = jnp.exp(cum)
    w_excl = jnp.exp(cum - logw)
    w_inv = jnp.exp(-cum)
    w_end = jnp.exp(after)
    w_all = jnp.exp(cum + after)

    kk_d = kk * w_excl
    r_d = r * w_incl
    b_h = b * w_inv
    k_h = k2 * w_inv
    b_e = b * w_end
    k_e = k2 * w_end
    return kk_d, r_d, b_h, k_h, b_e, k_e, v, w_all, bonus, g


def _rwkv_mix(pre, sbd_ref, lng_ref, lnb_ref, *, chunk, nb):
    kk_d, r_d, b_h, k_h, b_e, k_e, v, w_all, bonus, g = pre
    C = chunk
    R = nb * C
    pairs = range(A_HEADS // 2)
    seqs = range(nb)
    in_a = lax.broadcasted_iota(jnp.int32, (1, LANES), 1) < A_HD
    R2 = 2 * R
    ri = lax.broadcasted_iota(jnp.int32, (R2, R2), 0)
    ci = lax.broadcasted_iota(jnp.int32, (R2, R2), 1)
    same = (ri // C) == (ci // C)
    strict = same & (ci < ri)
    incl = same & (ci <= ri)
    eye = (ri == ci).astype(F32)

    def expand(x):
        return jnp.concatenate([jnp.where(in_a, x, 0.0), jnp.where(in_a, 0.0, x)], axis=0)

    def seq_rows(x, s):
        if nb == 1:
            return x
        return jnp.concatenate([x[s * C:(s + 1) * C], x[R + s * C:R + (s + 1) * C]], axis=0)

    def from_seqs(parts):
        if nb == 1:
            return parts[0]
        return jnp.concatenate([q[0:C] for q in parts] + [q[C:2 * C] for q in parts], axis=0)

    sl = [slice(pr * LANES, (pr + 1) * LANES) for pr in pairs]
    ex = lambda x: [expand(x[:, sl[pr]]) for pr in pairs]
    e_kk, e_r, e_bh, e_kh, e_be, e_ke = ex(kk_d), ex(r_d), ex(b_h), ex(k_h), ex(b_e), ex(k_e)
    v_e = [t.astype(BF16) for t in ex(v)]
    kk_bf = [t.astype(BF16) for t in e_kk]
    r_bf = [t.astype(BF16) for t in e_r]
    gram = [_bdot_nt(jnp.concatenate([kk_bf[pr], r_bf[pr]], axis=0),
                     jnp.concatenate([e_bh[pr], e_kh[pr]], axis=0)) for pr in pairs]
    l_b = [jnp.where(strict, gram[pr][0:R2, 0:R2], 0.0) for pr in pairs]
    l_k = [jnp.where(strict, gram[pr][0:R2, R2:2 * R2], 0.0) for pr in pairs]
    a_bk = [jnp.concatenate([jnp.where(incl, gram[pr][R2:2 * R2, 0:R2], 0.0),
                             jnp.where(incl, gram[pr][R2:2 * R2, R2:2 * R2], 0.0)],
                            axis=1).astype(BF16) for pr in pairs]
    t_inv = [eye - l_b[pr] for pr in pairs]
    pw = [t.astype(BF16) for t in l_b]
    e = 2
    while e < C:
        pw = [_bdot(pw[pr], pw[pr]).astype(BF16) for pr in pairs]
        t_inv = [t_inv[pr] + _bdot(t_inv[pr], pw[pr]) for pr in pairs]
        e *= 2
    lkv = [_bdot(l_k[pr], v_e[pr]) for pr in pairs]
    xs_kk, xs_r = [], []
    for pr in pairs:
        parts = [_bdot_nt(jnp.concatenate([seq_rows(kk_bf[pr], s), seq_rows(r_bf[pr], s)], axis=0),
                          sbd_ref[s, pr]) for s in seqs]
        xs_kk.append(from_seqs([q[0:2 * C] for q in parts]))
        xs_r.append(from_seqs([q[2 * C:4 * C] for q in parts]))
    u_e = [-_bdot(t_inv[pr], xs_kk[pr] + lkv[pr]).astype(BF16) for pr in pairs]
    uv = [jnp.concatenate([u_e[pr], v_e[pr]], axis=0) for pr in pairs]
    y_e = [xs_r[pr] + _bdot(a_bk[pr], uv[pr]) for pr in pairs]
    y = jnp.concatenate([y_e[pr][0:R] + y_e[pr][R:R2] for pr in pairs], axis=1)
    for pr in pairs:
        be_bf, ke_bf = e_be[pr].astype(BF16), e_ke[pr].astype(BF16)
        for s in seqs:
            upd = _bdot_tn(jnp.concatenate([seq_rows(u_e[pr], s), seq_rows(v_e[pr], s)], axis=0),
                           jnp.concatenate([seq_rows(be_bf, s), seq_rows(ke_bf, s)], axis=0))
            sbd_ref[s, pr] = sbd_ref[s, pr] * w_all[s * C:s * C + 1, sl[pr]] + upd

    inv_hd = 1.0 / A_HD
    mean = _seg_sum(y) * inv_hd
    yc = y - mean
    var = _seg_sum(yc * yc) * inv_hd
    yn = yc * lax.rsqrt(var + LNX_EPS) * lng_ref[...] + lnb_ref[...]
    yn = yn + bonus * v
    return yn * g


def _rwkv_kernel(p_ref, shift_ref, s0_ref, *rest, chunk, nb, sub):
    wrefs, (y_ref, sout_ref, sbd_ref, carry_ref) = rest[:-4], rest[-4:]
    c_idx = pl.program_id(1)
    pairs = range(A_HEADS // 2)
    rows = nb * chunk
    n_seq = sbd_ref.shape[0]

    @pl.when(c_idx == 0)
    def _():
        carry_ref[...] = shift_ref[...]
        z = jnp.zeros((A_HD, A_HD), F32)
        for s in range(n_seq):
            for pr in pairs:
                sa = s0_ref[s, 2 * pr]
                sb = s0_ref[s, 2 * pr + 1]
                sbd_ref[s, pr] = jnp.concatenate(
                    [jnp.concatenate([sa, z], axis=1), jnp.concatenate([z, sb], axis=1)], axis=0)

    p_all = p_ref[...].astype(F32)
    chunks = [p_all[i * rows:(i + 1) * rows] for i in range(sub)]
    if nb == 1:
        befores = [carry_ref[...]] + [c[rows - 1:rows, :].reshape(1, 1, RWKV_PAD) for c in chunks]
        carry_ref[...] = befores[sub]
        states = [sbd_ref] * sub
    else:
        befores = [carry_ref[i * nb:(i + 1) * nb] for i in range(sub)]
        states = [sbd_ref.at[pl.ds(i * nb, nb)] for i in range(sub)]
    pre = [_rwkv_pre(chunks[i], befores[i], *wrefs[:-2], chunk=chunk, nb=nb) for i in range(sub)]
    for i in range(sub):
        y = _rwkv_mix(pre[i], states[i], *wrefs[-2:], chunk=chunk, nb=nb)
        y_ref[i * rows:(i + 1) * rows, :] = y.astype(y_ref.dtype)

    @pl.when(c_idx == pl.num_programs(1) - 1)
    def _():
        for s in range(n_seq):
            for pr in pairs:
                s_bd = sbd_ref[s, pr]
                sout_ref[s, 2 * pr] = s_bd[0:A_HD, 0:A_HD]
                sout_ref[s, 2 * pr + 1] = s_bd[A_HD:2 * A_HD, A_HD:2 * A_HD]


def _rwkv(proj, shift0, s0, wts, batch, seq, chunk, nb, sub):
    seqs = 1 if nb == 1 else nb * sub
    n_chunks = seq // (chunk * sub) if nb == 1 else 1
    assert nb == 1 or seq == chunk
    rows = nb * chunk * sub
    vec = lambda w: pl.BlockSpec(w.shape, lambda b, c: (0,) * w.ndim)
    st = pl.BlockSpec((seqs, A_HEADS, A_HD, A_HD), lambda b, c: (b, 0, 0, 0))
    return pl.pallas_call(
        functools.partial(_rwkv_kernel, chunk=chunk, nb=nb, sub=sub),
        grid=(batch // seqs, n_chunks),
        in_specs=[pl.BlockSpec((rows, RWKV_PAD), lambda b, c: (b * n_chunks + c, 0)),
                  pl.BlockSpec((seqs, 1, RWKV_PAD), lambda b, c: (b, 0, 0)), st]
                 + [vec(w) for w in wts],
        out_specs=[pl.BlockSpec((rows, A_W), lambda b, c: (b * n_chunks + c, 0)), st],
        out_shape=[jax.ShapeDtypeStruct((batch * seq, A_W), BF16),
                   jax.ShapeDtypeStruct((batch, A_HEADS, A_HD, A_HD), F32)],
        scratch_shapes=[pltpu.VMEM((seqs, A_HEADS // 2, LANES, LANES), F32),
                        pltpu.VMEM((seqs, 1, RWKV_PAD), F32)],
        compiler_params=_cparams(("parallel", "arbitrary")),
        name="rwkv7",
    )(proj, shift0.reshape(batch, 1, RWKV_PAD), s0, *wts)


def _ret_kernel(q0_ref, q1_ref, k0_ref, k1_ref, v0_ref, v1_ref, g0_ref, g1_ref, cos_ref, sin_ref,
                din_ref, dq_ref, dk_ref, dc_ref, r0_ref, y_ref, rout_ref, r_scr, *, chunk, nb):
    C = chunk
    half = RET_BLK // R_HD
    c_idx = pl.program_id(1)

    @pl.when(c_idx == 0)
    def _():
        r_scr[...] = r0_ref[...]

    even = lax.broadcasted_iota(jnp.int32, (1, R_HD), 1) % 2 == 0
    cos = cos_ref[...]
    sin = sin_ref[...]

    def rot(t):
        partner = jnp.where(even, pltpu.roll(t, R_HD - 1, axis=1), pltpu.roll(t, 1, axis=1))
        return t * cos + partner * sin

    def load(halves, s, h):
        return halves[h // half][s * C:(s + 1) * C, (h % half) * R_HD:(h % half + 1) * R_HD]

    f32 = lambda *refs: tuple(r[...].astype(F32) for r in refs)
    items = [(s, h) for s in range(nb) for h in range(R_HEADS)]
    q_in, k_in, v_in, g_in = f32(q0_ref, q1_ref), f32(k0_ref, k1_ref), f32(v0_ref, v1_ref), f32(g0_ref, g1_ref)
    q = [rot(load(q_in, s, h)) for s, h in items]
    k = [rot(load(k_in, s, h)) * (R_HD ** -0.5) for s, h in items]
    v = [load(v_in, s, h).astype(BF16) for s, h in items]
    state = [r_scr[s, h] for s, h in items]
    scores = [_bdot_nt(q[i], k[i]) * din_ref[h] for i, (s, h) in enumerate(items)]
    o = [_bdot(jnp.concatenate([scores[i], q[i] * dq_ref[h]], axis=1),
               jnp.concatenate([v[i], state[i].astype(BF16)], axis=0))
         for i, (s, h) in enumerate(items)]
    for i, (s, h) in enumerate(items):
        r_scr[s, h] = state[i] * dc_ref[h, 0:1, :] + _bdot_tn(k[i] * dk_ref[h], v[i])
    for i, (s, h) in enumerate(items):
        on = o[i] * lax.rsqrt(jnp.mean(o[i] * o[i], axis=-1, keepdims=True) + NORM_EPS)
        gate = load(g_in, s, h)
        y_ref[s * C:(s + 1) * C, h * R_HD:(h + 1) * R_HD] = (
            gate * _sigmoid(gate) * on).astype(y_ref.dtype)

    @pl.when(c_idx == pl.num_programs(1) - 1)
    def _():
        rout_ref[...] = r_scr[...]


def _ret_tables(chunk, pos0, seq):
    f32 = np.float32
    pos = f32(pos0) + np.arange(seq, dtype=f32)
    theta = f32(1.0) / (f32(10000.0) ** np.linspace(0.0, 1.0, R_HD // 2, dtype=f32))
    ang = pos[:, None] * theta[None, :]
    cos2 = np.repeat(np.cos(ang), 2, axis=1)
    sin2 = np.stack([-np.sin(ang), np.sin(ang)], axis=-1).reshape(seq, R_HD)
    lg = np.log1p(-(f32(2.0) ** (f32(-5.0) - np.arange(R_HEADS, dtype=f32))))
    idx = np.arange(chunk, dtype=f32)
    diff = idx[:, None] - idx[None, :]
    causal = diff >= 0
    din = np.where(causal, np.exp(np.where(causal, diff, f32(0.0))[None] * lg[:, None, None]), f32(0.0))
    bc = lambda t: np.broadcast_to(t[:, :, None], (R_HEADS, t.shape[1], R_HD))
    dq = bc(np.exp((idx + f32(1.0))[None, :] * lg[:, None]))
    dk = bc(np.exp((f32(chunk - 1.0) - idx)[None, :] * lg[:, None]))
    dc = np.broadcast_to(np.exp(f32(chunk) * lg)[:, None, None], (R_HEADS, 8, R_HD))
    return tuple(jnp.asarray(np.ascontiguousarray(t, dtype=f32))
                 for t in (cos2, sin2, din, dq, dk, dc))


def _retention(proj, r0, batch, seq, chunk, nb, pos0):
    n_chunks = seq // chunk
    cos2, sin2, din, dq, dk, dc = _ret_tables(chunk, pos0, seq)
    col0 = RWKV_PAD // RET_BLK
    rows = nb * chunk

    def sec(i):
        return pl.BlockSpec((rows, RET_BLK), lambda b, c: (b * n_chunks + c, col0 + i))

    tab = lambda t: pl.BlockSpec(t.shape, lambda b, c: (0, 0, 0))
    st = pl.BlockSpec((nb, R_HEADS, R_HD, R_HD), lambda b, c: (b, 0, 0, 0))
    return pl.pallas_call(
        functools.partial(_ret_kernel, chunk=chunk, nb=nb),
        grid=(batch // nb, n_chunks),
        in_specs=[sec(i) for i in range(8)]
                 + [pl.BlockSpec((chunk, R_HD), lambda b, c: (c, 0)),
                    pl.BlockSpec((chunk, R_HD), lambda b, c: (c, 0)),
                    tab(din), tab(dq), tab(dk), tab(dc), st],
        out_specs=[pl.BlockSpec((rows, R_W), lambda b, c: (b * n_chunks + c, 0)), st],
        out_shape=[jax.ShapeDtypeStruct((batch * seq, R_W), BF16),
                   jax.ShapeDtypeStruct((batch, R_HEADS, R_HD, R_HD), F32)],
        scratch_shapes=[pltpu.VMEM((nb, R_HEADS, R_HD, R_HD), F32)],
        compiler_params=_cparams(("parallel", "arbitrary")),
        name="retention",
    )(*([proj] * 8), cos2, sin2, din, dq, dk, dc, r0)


def _outproj_kernel(x_ref, ya_ref, yb_ref, wa_ref, wb_ref, g_ref, wr_ref,
                    x1_ref, h2_ref, ri_ref, rg_ref, *, parts):
    rows = x_ref.shape[0] // parts
    sl = [slice(i * rows, (i + 1) * rows) for i in range(parts)]
    x1s = [x_ref[s, :] + jnp.dot(ya_ref[s, :], wa_ref[...], preferred_element_type=F32)
           + jnp.dot(yb_ref[s, :], wb_ref[...], preferred_element_type=F32) for s in sl]
    for s, x1 in zip(sl, x1s):
        x1_ref[s, :] = x1
        ms = jnp.mean(x1 * x1, axis=-1, keepdims=True)
        h2 = x1 * lax.rsqrt(ms + NORM_EPS) * g_ref[...]
        h2_ref[s, :] = _pack_bf16_pairs(h2)
        ri_ref[s, :], rg_ref[s, :] = _route(_bdot(h2, wr_ref[...]))


def _route(logits):
    lane = lax.broadcasted_iota(jnp.int32, (1, LANES), 1)
    neg = -1e30
    big = LANES
    rmax = lambda t: jnp.max(t, axis=-1, keepdims=True)
    rmin = lambda t: jnp.min(t, axis=-1, keepdims=True)
    is_c = lane < N_GROUPS
    lc = jnp.where(is_c, logits, neg)
    mc = rmax(lc)
    grp = rmin(jnp.where(lc == mc, lane, big))
    p_grp = 1.0 / jnp.sum(jnp.where(is_c, jnp.exp(lc - mc), 0.0), axis=-1, keepdims=True)
    fine = lane - N_GROUPS
    in_g = (fine >= 0) & (fine < N_EXPERTS) & ((fine // E_PER_GROUP) == grp)
    lf = jnp.where(in_g, logits, neg)
    m1 = rmax(lf)
    i1 = rmin(jnp.where(lf == m1, lane, big))
    lf2 = jnp.where(lane == i1, neg, lf)
    m2 = rmax(lf2)
    i2 = rmin(jnp.where(lf2 == m2, lane, big))
    e2 = jnp.exp(m2 - m1)
    g1 = p_grp / (1.0 + e2)
    g2 = p_grp * e2 / (1.0 + e2)
    return (jnp.where(lane == 0, i1 - N_GROUPS, jnp.where(lane == 1, i2 - N_GROUPS, 0)),
            jnp.where(lane == 0, g1, jnp.where(lane == 1, g2, 0.0)))


def _outproj(x, ya, yb, wa, wb, g, wr, tm=512, parts=2):
    n, d = x.shape
    rowblk = lambda w: pl.BlockSpec((tm, w), lambda i: (i, 0))
    full = lambda w: pl.BlockSpec(w.shape, lambda i: (0, 0))
    return pl.pallas_call(
        functools.partial(_outproj_kernel, parts=parts),
        grid=(n // tm,),
        in_specs=[rowblk(d), rowblk(A_W), rowblk(R_W), full(wa), full(wb), full(g), full(wr)],
        out_specs=[rowblk(d), rowblk(d // 2), rowblk(LANES), rowblk(LANES)],
        out_shape=[jax.ShapeDtypeStruct((n, d), F32), jax.ShapeDtypeStruct((n, d // 2), jnp.uint32),
                   jax.ShapeDtypeStruct((n, LANES), jnp.int32),
                   jax.ShapeDtypeStruct((n, LANES), F32)],
        compiler_params=_cparams(("parallel",)),
        name="outproj_router",
    )(x, ya, yb, wa, wb, g, wr)


def _sort_kernel(ri_ref, tri_ref, dest_ref, info_ref, cnt_ref, base_ref):
    ph = pl.program_id(0)
    i = pl.program_id(1)
    lane = lax.broadcasted_iota(jnp.int32, (1, LANES), 1)
    ri = ri_ref[...]
    oh1 = (lane == ri[:, 0:1]).astype(F32)
    oh2 = (lane == ri[:, 1:2]).astype(F32)
    tot1 = jnp.sum(oh1, axis=0, keepdims=True)
    tot2 = jnp.sum(oh2, axis=0, keepdims=True)

    @pl.when(ph == 0)
    def _():
        @pl.when(i == 0)
        def _():
            cnt_ref[...] = jnp.zeros_like(cnt_ref)
        cnt_ref[...] += tot1 + tot2

    @pl.when(ph == 1)
    def _():
        @pl.when(i == 0)
        def _():
            nblk = jnp.floor((cnt_ref[...] + (MOE_BLK - 1)) * (1.0 / MOE_BLK))
            upper = (lax.broadcasted_iota(jnp.int32, (LANES, LANES), 0) <=
                     lax.broadcasted_iota(jnp.int32, (LANES, LANES), 1))
            pend = _bdot(jnp.broadcast_to(nblk, (8, LANES)), upper.astype(F32))[0:1]
            first = pend - nblk
            base_ref[...] = first * MOE_BLK
            srow = lax.broadcasted_iota(jnp.int32, (8, LANES), 0)
            info_ref[...] = jnp.where(srow == 0, nblk, jnp.where(srow == 1, first, 0.0)).astype(jnp.int32)

        base = base_ref[...]
        tri = tri_ref[...]
        pre1 = jnp.dot(tri, oh1.astype(BF16), preferred_element_type=F32)
        pre2 = jnp.dot(tri, oh2.astype(BF16), preferred_element_type=F32) + tot1
        d1 = jnp.sum(oh1 * (base + pre1), axis=1, keepdims=True)
        d2 = jnp.sum(oh2 * (base + pre2), axis=1, keepdims=True)
        dest_ref[...] = jnp.where(lane == 0, d1, jnp.where(lane == 1, d2, 0.0)).astype(jnp.int32)
        base_ref[...] = base + tot1 + tot2


def _sort(ri, tt=1024):
    n = ri.shape[0]
    tri = (jnp.arange(tt)[:, None] > jnp.arange(tt)[None, :]).astype(BF16)
    return pl.pallas_call(
        _sort_kernel,
        grid=(2, n // tt),
        in_specs=[pl.BlockSpec((tt, LANES), lambda ph, i: (i, 0)),
                  pl.BlockSpec((tt, tt), lambda ph, i: (0, 0))],
        out_specs=[pl.BlockSpec((tt, LANES), lambda ph, i: (i * ph, 0)),
                   pl.BlockSpec((8, LANES), lambda ph, i: (0, 0))],
        out_shape=[jax.ShapeDtypeStruct((n, LANES), jnp.int32),
                   jax.ShapeDtypeStruct((8, LANES), jnp.int32)],
        scratch_shapes=[pltpu.VMEM((1, LANES), F32), pltpu.VMEM((1, LANES), F32)],
        compiler_params=_cparams(("arbitrary", "arbitrary")),
        name="moe_sort",
    )(ri, tri)


def _row_copy(src_ref, src_row, dst_ref, dst_row, sem):
    return pltpu.make_async_copy(src_ref.at[pl.ds(src_row, 1)], dst_ref.at[pl.ds(dst_row, 1)], sem)


def _dispatch_kernel(d1_ref, d2_ref, h_ref, xs_in_ref, xs_ref, sem, *, tt):
    del xs_in_ref

    def issue(t, carry):
        _row_copy(h_ref, t, xs_ref, d1_ref[t], sem).start()
        _row_copy(h_ref, t, xs_ref, d2_ref[t], sem).start(priority=1)
        return carry

    def drain(t, carry):
        _row_copy(h_ref, t, xs_ref, d1_ref[t], sem).wait()
        _row_copy(h_ref, t, xs_ref, d2_ref[t], sem).wait()
        return carry

    lax.fori_loop(0, tt, issue, 0, unroll=8)
    lax.fori_loop(0, tt, drain, 0, unroll=8)


def _dispatch(d1, d2, h2, xs, tt=1024):
    n, d = h2.shape
    smem = pl.BlockSpec((tt,), lambda i: (i,), memory_space=pltpu.SMEM)
    anyspec = pl.BlockSpec(memory_space=pl.ANY)
    return pl.pallas_call(
        functools.partial(_dispatch_kernel, tt=tt),
        grid=(n // tt,),
        in_specs=[smem, smem, pl.BlockSpec((tt, d), lambda i: (i, 0)), anyspec],
        out_specs=anyspec,
        out_shape=jax.ShapeDtypeStruct(xs.shape, xs.dtype),
        scratch_shapes=[pltpu.SemaphoreType.DMA(())],
        input_output_aliases={3: 0},
        compiler_params=_cparams(("arbitrary",)),
        name="moe_dispatch",
    )(d1, d2, h2, xs)


def _expert_kernel(nb_ref, fb_ref, xs_hbm, wg_hbm, wu_hbm, wd_hbm, y_hbm,
                   wg_r, wu_r, wd_r, wg_s, wu_s, wd_s, xbuf, ybuf, sem_w, sem_in, sem_out, *, n_blk):
    e = pl.program_id(0)
    last = pl.num_programs(0) - 1
    n = nb_ref[e]
    b0 = fb_ref[e]
    n_used = fb_ref[last] + nb_ref[last]
    half = xbuf.shape[2]
    ring = wg_r.shape[0]

    def fetch(ex):
        slot = ex % ring
        return [pltpu.make_async_copy(src.at[ex], dst.at[slot], sem_w.at[slot])
                for src, dst in ((wg_hbm, wg_r), (wu_hbm, wu_r), (wd_hbm, wd_r))]

    @pl.when(e == 0)
    def _():
        for ex in range(ring - 1):
            for cp in fetch(ex):
                cp.start(priority=W_DMA_PRIORITY)

    @pl.when(e + ring - 1 <= last)
    def _():
        for cp in fetch(e + ring - 1):
            cp.start(priority=W_DMA_PRIORITY)

    for cp in fetch(e):
        cp.wait()
    w_slot = e % ring

    def load(blk, slot):
        return pltpu.make_async_copy(xs_hbm.at[pl.ds(blk * MOE_BLK, MOE_BLK)],
                                     xbuf.at[slot], sem_in.at[slot])

    def store(blk, slot):
        return pltpu.make_async_copy(ybuf.at[slot], y_hbm.at[pl.ds(blk * MOE_BLK, MOE_BLK)],
                                     sem_out.at[slot])

    @pl.when(n > 0)
    def _():
        @pl.when(b0 == 0)
        def _():
            load(0, 0).start()

        wg_s[...] = wg_r[w_slot].astype(BF16)
        wu_s[...] = wu_r[w_slot].astype(BF16)
        wd_s[...] = wd_r[w_slot].astype(BF16)

        def body(blk, carry):
            slot = blk % 2
            load(blk, slot).wait()

            @pl.when(blk + 1 < n_used)
            def _():
                load(blk + 1, 1 - slot).start()

            @pl.when(blk >= 2)
            def _():
                store(blk - 2, slot).wait()

            x_lo, x_hi = _unpack_bf16_pairs(xbuf[slot])
            x_lo, x_hi = x_lo.astype(BF16), x_hi.astype(BF16)
            mm = lambda w_s: (jnp.dot(x_lo, w_s[0:half], preferred_element_type=F32)
                              + jnp.dot(x_hi, w_s[half:2 * half], preferred_element_type=F32))
            hg = mm(wg_s)
            hu = mm(wu_s)
            act = (hg * _sigmoid(hg) * hu).astype(BF16)
            ybuf[slot] = _pack_bf16_pairs(jnp.dot(act, wd_s[...], preferred_element_type=F32))
            store(blk, slot).start()
            return carry

        lax.fori_loop(b0, b0 + n, body, 0)

    @pl.when(e == last)
    def _():
        @pl.when(n_used >= 2)
        def _():
            store(n_used - 2, n_used % 2).wait()

        @pl.when(n_used >= 1)
        def _():
            store(n_used - 1, (n_used - 1) % 2).wait()

        ybuf[0] = jnp.zeros(ybuf.shape[1:], ybuf.dtype)

        def fill(blk, carry):
            cp = store(blk, 0)
            cp.start()
            cp.wait()
            return carry

        lax.fori_loop(n_used, n_blk, fill, 0)


def _experts(n_blocks, first_block, xs, wg, wu, wd):
    p, half = xs.shape
    d = 2 * half
    anyspec = pl.BlockSpec(memory_space=pl.ANY)
    ring = W_AHEAD + 1
    grid_spec = pltpu.PrefetchScalarGridSpec(
        num_scalar_prefetch=2,
        grid=(N_EXPERTS,),
        in_specs=[anyspec] * 4,
        out_specs=anyspec,
        scratch_shapes=[pltpu.VMEM((ring, d, D_EXPERT), F32), pltpu.VMEM((ring, d, D_EXPERT), F32),
                        pltpu.VMEM((ring, D_EXPERT, d), F32),
                        pltpu.VMEM((d, D_EXPERT), BF16), pltpu.VMEM((d, D_EXPERT), BF16),
                        pltpu.VMEM((D_EXPERT, d), BF16),
                        pltpu.VMEM((2, MOE_BLK, half), jnp.uint32),
                        pltpu.VMEM((2, MOE_BLK, half), jnp.uint32),
                        pltpu.SemaphoreType.DMA((ring,)),
                        pltpu.SemaphoreType.DMA((2,)), pltpu.SemaphoreType.DMA((2,))])
    return pl.pallas_call(
        functools.partial(_expert_kernel, n_blk=p // MOE_BLK),
        grid_spec=grid_spec,
        out_shape=jax.ShapeDtypeStruct((p, half), jnp.uint32),
        compiler_params=_cparams(("arbitrary",)),
        name="moe_experts",
    )(n_blocks, first_block, xs, wg, wu, wd)


def _combine_kernel(d1_ref, d2_ref, d1n_ref, d2n_ref, x1_ref, rg_ref, g_ref, yb_ref, o_ref, buf, sem,
                    *, tt):
    i = pl.program_id(0)
    slot = i % 2

    def gather(a_ref, b_ref, s, wait):
        def body(t, carry):
            for j, idx in enumerate((a_ref, b_ref)):
                cp = _row_copy(yb_ref, idx[t], buf.at[s, j], t, sem.at[s])
                cp.wait() if wait else cp.start(priority=j)
            return carry
        lax.fori_loop(0, tt, body, 0, unroll=8)

    @pl.when(i == 0)
    def _():
        gather(d1_ref, d2_ref, 0, False)

    @pl.when(i + 1 < pl.num_programs(0))
    def _():
        gather(d1n_ref, d2n_ref, 1 - slot, False)

    gather(d1_ref, d2_ref, slot, True)
    rg = rg_ref[...]
    lo1, hi1 = _unpack_bf16_pairs(buf[slot, 0])
    lo2, hi2 = _unpack_bf16_pairs(buf[slot, 1])
    g1, g2 = rg[:, 0:1], rg[:, 1:2]
    x = x1_ref[...] + jnp.concatenate([g1 * lo1 + g2 * lo2, g1 * hi1 + g2 * hi2], axis=1)
    ms = jnp.mean(x * x, axis=-1, keepdims=True)
    o_ref[...] = x * lax.rsqrt(ms + NORM_EPS) * g_ref[...]


def _combine(d1, d2, x1, rg, g, yb, tt=256):
    n, d = x1.shape
    steps = n // tt
    smem = pl.BlockSpec((tt,), lambda i: (i,), memory_space=pltpu.SMEM)
    smem_next = pl.BlockSpec((tt,), lambda i: (jnp.minimum(i + 1, steps - 1),),
                             memory_space=pltpu.SMEM)
    rowblk = lambda w: pl.BlockSpec((tt, w), lambda i: (i, 0))
    return pl.pallas_call(
        functools.partial(_combine_kernel, tt=tt),
        grid=(steps,),
        in_specs=[smem, smem, smem_next, smem_next, rowblk(d), rowblk(LANES),
                  pl.BlockSpec((1, d), lambda i: (0, 0)), pl.BlockSpec(memory_space=pl.ANY)],
        out_specs=rowblk(d),
        out_shape=jax.ShapeDtypeStruct((n, d), F32),
        scratch_shapes=[pltpu.VMEM((2, 2, tt, d // 2), jnp.uint32), pltpu.SemaphoreType.DMA((2,))],
        compiler_params=_cparams(("arbitrary",)),
        name="moe_combine",
    )(d1, d2, d1, d2, x1, rg, g, yb)


def _pad_cols(x, width):
    return jnp.pad(x, ((0, 0), (0, width - x.shape[1])))


def _rwkv_layout(x):
    return jnp.concatenate([
        x[:, :OFF_WD],
        _pad_cols(x[:, 3 * A_W:3 * A_W + W_LORA], LANES),
        _pad_cols(x[:, 3 * A_W + W_LORA:3 * A_W + W_LORA + A_LORA], LANES),
        _pad_cols(x[:, 3 * A_W + W_LORA + A_LORA:RWKV_PROJ], 2 * LANES)], axis=1)


def _rwkv_unlayout(x):
    return jnp.concatenate([x[:, :OFF_WD + W_LORA], x[:, OFF_AD:OFF_AD + A_LORA],
                            x[:, OFF_GD:OFF_GD + G_LORA]], axis=1)


def _pad_rows(w, rows):
    return jnp.pad(w, ((0, rows - w.shape[0]), (0, 0)))


def kernel(x_prompt, x_sample, state_rwkv_shift, state_rwkv_wkv, state_retention, norm_mix_g, w_in, mu_shift, decay_w0, decay_up, iclr_a0, iclr_up, gate_up, k_k, k_a, r_k, lnx_g, lnx_b, w_out, norm_ffn_g, router_coarse, router_fine, exp_w_gate, exp_w_up, exp_w_down, norm_final_g):
    bp, tp, d = x_prompt.shape
    bs, ts, _ = x_sample.shape
    row = lambda t: t.reshape(1, -1)
    l = 0

    w_in_p = jnp.concatenate([_rwkv_layout(w_in[l][:, :RWKV_PROJ]), w_in[l][:, RWKV_PROJ:]],
                             axis=1).astype(BF16)
    rwkv_wts = (_rwkv_layout(row(mu_shift[l])), row(decay_w0[l]),
                _pad_rows(decay_up[l], LANES).astype(BF16), row(iclr_a0[l]),
                _pad_rows(iclr_up[l], LANES).astype(BF16),
                _pad_rows(gate_up[l], 2 * LANES).astype(BF16),
                row(k_k[l]), row(k_a[l]), row(r_k[l]), row(lnx_g[l]), row(lnx_b[l]))
    w_out_a = w_out[l][:A_W].astype(BF16)
    w_out_b = w_out[l][A_W:].astype(BF16)
    w_router = _pad_cols(jnp.concatenate(
        [router_coarse[l], jnp.transpose(router_fine[l], (1, 0, 2)).reshape(d, N_EXPERTS)], axis=1),
        LANES).astype(BF16)

    groups = (
        (x_prompt.reshape(bp * tp, d), bp, tp, WKV_CHUNK, RET_CHUNK, 1, 0.0,
         jnp.zeros((bp, RWKV_PAD), F32), jnp.zeros((bp, A_HEADS, A_HD, A_HD), F32),
         jnp.zeros((bp, R_HEADS, R_HD, R_HD), F32)),
        (x_sample.reshape(bs * ts, d), bs, ts, ts, math.gcd(ts, RET_CHUNK), 8, float(PAST_LEN),
         _rwkv_layout(state_rwkv_shift[l]), state_rwkv_wkv[l], state_retention[l]),
    )
    x_last = jnp.concatenate([x_prompt[:, -1, :], x_sample[:, -1, :]], axis=0)
    n_last = -(-(bp + bs) // 8) * 8
    shifts = _rwkv_unlayout(_inproj(_pad_rows(x_last, n_last), row(norm_mix_g[l]), w_in_p, F32,
                                    RWKV_PAD, n_last, RWKV_PAD // 2))
    new_shifts = (shifts[:bp], shifts[bp:bp + bs])

    mixed = []
    for (x, b, t, wkv_chunk, ret_chunk, nb, pos0, shift0, wkv0, ret0), new_shift in zip(groups, new_shifts):
        proj = _inproj(x, row(norm_mix_g[l]), w_in_p, BF16, PROJ_PAD, 1024, 1536)
        y_a, new_wkv = _rwkv(proj, shift0, wkv0, rwkv_wts, b, t, wkv_chunk, nb,
                             WKV_SUB if nb == 1 else WKV_GROUPS)
        y_b, new_ret = _retention(proj, ret0, b, t, ret_chunk, nb, pos0)
        x1, h2, ri, rg = _outproj(x, y_a, y_b, w_out_a, w_out_b, row(norm_ffn_g[l]), w_router)
        mixed.append((x1, h2, ri, rg, new_shift, new_wkv, new_ret))

    n_tok = [m[0].shape[0] for m in mixed]
    n_all = sum(n_tok)
    n_blk = -(-2 * n_all // MOE_BLK) + N_EXPERTS
    dest, info = _sort(jnp.concatenate([m[2] for m in mixed], axis=0))
    n_blocks, first_block = info[0, :N_EXPERTS], info[1, :N_EXPERTS]
    xs = jnp.zeros((n_blk * MOE_BLK, d // 2), jnp.uint32)
    off = 0
    dests = []
    for m, n in zip(mixed, n_tok):
        d1, d2 = dest[off:off + n, 0], dest[off:off + n, 1]
        dests.append((d1, d2))
        xs = _dispatch(d1, d2, m[1], xs)
        off += n
    yb = _experts(n_blocks, first_block, xs, exp_w_gate[l], exp_w_up[l], exp_w_down[l])
    outs = [_combine(d1, d2, m[0], m[3], row(norm_final_g), yb)
            for m, (d1, d2) in zip(mixed, dests)]

    (_, _, _, _, p_shift, p_wkv, p_ret), (_, _, _, _, s_shift, s_wkv, s_ret) = mixed
    return (outs[0].reshape(bp, tp, d), outs[1].reshape(bs, ts, d),
            p_shift[None], p_wkv[None], p_ret[None], s_shift[None], s_wkv[None], s_ret[None])
```

```python
import functools
import math

import jax
import jax.numpy as jnp
import numpy as np
from jax import lax
from jax.experimental import pallas as pl
from jax.experimental.pallas import tpu as pltpu

F32 = jnp.float32
BF16 = jnp.bfloat16

D_MODEL = 2048
A_W = 1024
A_HD = 64
A_HEADS = 16
W_LORA = 64
A_LORA = 64
G_LORA = 160
RWKV_PROJ = 3 * A_W + W_LORA + A_LORA + G_LORA
R_W = 1024
R_HEADS = 8
R_HD = 128
RET_CHUNK = 128
N_GROUPS = 4
E_PER_GROUP = 8
N_EXPERTS = 32
D_EXPERT = 512
NORM_EPS = 1e-6
LNX_EPS = 64e-5
DECAY_SCALE = math.exp(-0.5)
PAST_LEN = 16384

LANES = 128
OFF_WD = 3 * A_W
OFF_AD = OFF_WD + LANES
OFF_GD = OFF_AD + LANES
RWKV_PAD = OFF_GD + 2 * LANES
PROJ_PAD = RWKV_PAD + 4 * R_W
RET_BLK = 512
WKV_CHUNK = 64
WKV_SUB = 4
WKV_GROUPS = 2
MOE_BLK = 256
W_AHEAD = 2
W_DMA_PRIORITY = 1
VMEM_LIMIT = 56 * 1024 * 1024


def _cparams(sem):
    return pltpu.CompilerParams(dimension_semantics=sem, vmem_limit_bytes=VMEM_LIMIT)


def _bdot(a, b):
    return jnp.dot(a.astype(BF16), b.astype(BF16), preferred_element_type=F32)


def _bdot_nt(a, b):
    return lax.dot_general(a.astype(BF16), b.astype(BF16), (((1,), (1,)), ((), ())),
                           preferred_element_type=F32)


def _bdot_tn(a, b):
    return lax.dot_general(a.astype(BF16), b.astype(BF16), (((0,), (0,)), ((), ())),
                           preferred_element_type=F32)


def _dot_split(x, w_bf16):
    hi = x.astype(BF16)
    lo = (x - hi.astype(F32)).astype(BF16)
    d = lambda p: jnp.dot(p, w_bf16, preferred_element_type=F32)
    return d(hi) + d(lo)


def _sigmoid(x):
    return 1.0 / (1.0 + jnp.exp(-x))


def _pack_bf16_pairs(x):
    w = x.shape[1] // 2
    lo = lax.bitcast_convert_type(x[:, :w].astype(BF16).astype(F32), jnp.uint32)
    hi = lax.bitcast_convert_type(x[:, w:].astype(BF16).astype(F32), jnp.uint32)
    return hi | (lo >> 16)


def _unpack_bf16_pairs(p):
    lo = lax.bitcast_convert_type(p << 16, F32)
    hi = lax.bitcast_convert_type(p & jnp.uint32(0xFFFF0000), F32)
    return lo, hi


def _inproj_kernel(x_ref, g_ref, w_ref, o_ref, h_ref):
    @pl.when(pl.program_id(1) == 0)
    def _():
        x = x_ref[...]
        ms = jnp.mean(x * x, axis=-1, keepdims=True)
        h_ref[...] = (x * lax.rsqrt(ms + NORM_EPS) * g_ref[...]).astype(BF16)

    o_ref[...] = jnp.dot(h_ref[...], w_ref[...], preferred_element_type=F32).astype(o_ref.dtype)


def _inproj(x, g, w_bf16, out_dtype, n_cols, tm, tn):
    n, d = x.shape
    return pl.pallas_call(
        _inproj_kernel,
        grid=(n // tm, n_cols // tn),
        in_specs=[pl.BlockSpec((tm, d), lambda i, j: (i, 0)),
                  pl.BlockSpec((1, d), lambda i, j: (0, 0)),
                  pl.BlockSpec((d, tn), lambda i, j: (0, j))],
        out_specs=pl.BlockSpec((tm, tn), lambda i, j: (i, j)),
        out_shape=jax.ShapeDtypeStruct((n, n_cols), out_dtype),
        scratch_shapes=[pltpu.VMEM((tm, d), BF16)],
        compiler_params=_cparams(("parallel", "arbitrary")),
        name="inproj",
    )(x, g, w_bf16)


def _seg_sum(x):
    ones_bd = (lax.broadcasted_iota(jnp.int32, (LANES, LANES), 0) // A_HD ==
               lax.broadcasted_iota(jnp.int32, (LANES, LANES), 1) // A_HD).astype(BF16)
    nslab = x.shape[1] // LANES
    stacked = jnp.concatenate([x[:, s * LANES:(s + 1) * LANES] for s in range(nslab)], axis=0)
    red = _dot_split(stacked, ones_bd)
    c = x.shape[0]
    return jnp.concatenate([red[s * c:(s + 1) * c] for s in range(nslab)], axis=1)


def _rwkv_pre(p, before, mu_ref, w0_ref, wup_ref, a0_ref, aup_ref, gup_ref, kk_ref, ka_ref, rk_ref,
              *, chunk, nb):
    C = chunk
    R = nb * C
    row = lax.broadcasted_iota(jnp.int32, (R, 1), 0)
    pos = row % C
    first = jnp.broadcast_to(before, (nb, C, RWKV_PAD)).reshape(R, RWKV_PAD)
    prev = jnp.where(pos == 0, first, pltpu.roll(p, 1, axis=0))
    xm = p + (prev - p) * mu_ref[...]

    r = xm[:, 0:A_W]
    k = xm[:, A_W:2 * A_W]
    v = xm[:, 2 * A_W:3 * A_W]
    wd = xm[:, OFF_WD:OFF_AD]
    ad = xm[:, OFF_AD:OFF_GD]
    gd = xm[:, OFF_GD:RWKV_PAD]

    logw = -DECAY_SCALE * _sigmoid(w0_ref[...] + _bdot(jnp.tanh(wd), wup_ref[...]))
    a = _sigmoid(a0_ref[...] + _bdot(ad, aup_ref[...]))
    g = _bdot(_sigmoid(gd), gup_ref[...])

    kk = k * kk_ref[...]
    k2 = k * (1.0 + (a - 1.0) * ka_ref[...])
    sums = _seg_sum(jnp.concatenate([kk * kk, r * k2 * rk_ref[...]], axis=0))
    kk = kk / jnp.maximum(jnp.sqrt(sums[0:R]), 1e-12)
    bonus = sums[R:2 * R]
    b = kk * a

    cum = logw
    suf = logw
    d = 1
    while d < C:
        cum = cum + jnp.where(pos >= d, pltpu.roll(cum, d, axis=0), 0.0)
        if nb > 1:
            suf = suf + jnp.where(pos < C - d, pltpu.roll(suf, R - d, axis=0), 0.0)
        d *= 2
    after = suf - logw if nb > 1 else cum[C - 1:C, :] - cum
    w_incl = jnp.exp(cum)
    w_excl = jnp.exp(cum - logw)
    w_inv = jnp.exp(-cum)
    w_end = jnp.exp(after)
    w_all = jnp.exp(cum + after)

    kk_d = kk * w_excl
    r_d = r * w_incl
    b_h = b * w_inv
    k_h = k2 * w_inv
    b_e = b * w_end
    k_e = k2 * w_end
    return kk_d, r_d, b_h, k_h, b_e, k_e, v, w_all, bonus, g


def _rwkv_mix(pre, sbd_ref, lng_ref, lnb_ref, *, chunk, nb):
    kk_d, r_d, b_h, k_h, b_e, k_e, v, w_all, bonus, g = pre
    C = chunk
    R = nb * C
    pairs = range(A_HEADS // 2)
    seqs = range(nb)
    in_a = lax.broadcasted_iota(jnp.int32, (1, LANES), 1) < A_HD
    R2 = 2 * R
    ri = lax.broadcasted_iota(jnp.int32, (R2, R2), 0)
    ci = lax.broadcasted_iota(jnp.int32, (R2, R2), 1)
    same = (ri // C) == (ci // C)
    strict = same & (ci < ri)
    incl = same & (ci <= ri)
    eye = (ri == ci).astype(F32)

    def expand(x):
        return jnp.concatenate([jnp.where(in_a, x, 0.0), jnp.where(in_a, 0.0, x)], axis=0)

    def seq_rows(x, s):
        if nb == 1:
            return x
        return jnp.concatenate([x[s * C:(s + 1) * C], x[R + s * C:R + (s + 1) * C]], axis=0)

    def from_seqs(parts):
        if nb == 1:
            return parts[0]
        return jnp.concatenate([q[0:C] for q in parts] + [q[C:2 * C] for q in parts], axis=0)

    sl = [slice(pr * LANES, (pr + 1) * LANES) for pr in pairs]
    ex = lambda x: [expand(x[:, sl[pr]]) for pr in pairs]
    e_kk, e_r, e_bh, e_kh, e_be, e_ke = ex(kk_d), ex(r_d), ex(b_h), ex(k_h), ex(b_e), ex(k_e)
    v_e = [t.astype(BF16) for t in ex(v)]
    kk_bf = [t.astype(BF16) for t in e_kk]
    r_bf = [t.astype(BF16) for t in e_r]
    gram = [_bdot_nt(jnp.concatenate([kk_bf[pr], r_bf[pr]], axis=0),
                     jnp.concatenate([e_bh[pr], e_kh[pr]], axis=0)) for pr in pairs]
    l_b = [jnp.where(strict, gram[pr][0:R2, 0:R2], 0.0) for pr in pairs]
    l_k = [jnp.where(strict, gram[pr][0:R2, R2:2 * R2], 0.0) for pr in pairs]
    a_bk = [jnp.concatenate([jnp.where(incl, gram[pr][R2:2 * R2, 0:R2], 0.0),
                             jnp.where(incl, gram[pr][R2:2 * R2, R2:2 * R2], 0.0)],
                            axis=1).astype(BF16) for pr in pairs]
    t_inv = [eye - l_b[pr] for pr in pairs]
    pw = [t.astype(BF16) for t in l_b]
    e = 2
    while e < C:
        pw = [_bdot(pw[pr], pw[pr]).astype(BF16) for pr in pairs]
        t_inv = [t_inv[pr] + _bdot(t_inv[pr], pw[pr]) for pr in pairs]
        e *= 2
    lkv = [_bdot(l_k[pr], v_e[pr]) for pr in pairs]
    xs_kk, xs_r = [], []
    for pr in pairs:
        parts = [_bdot_nt(jnp.concatenate([seq_rows(kk_bf[pr], s), seq_rows(r_bf[pr], s)], axis=0),
                          sbd_ref[s, pr]) for s in seqs]
        xs_kk.append(from_seqs([q[0:2 * C] for q in parts]))
        xs_r.append(from_seqs([q[2 * C:4 * C] for q in parts]))
    u_e = [-_bdot(t_inv[pr], xs_kk[pr] + lkv[pr]).astype(BF16) for pr in pairs]
    uv = [jnp.concatenate([u_e[pr], v_e[pr]], axis=0) for pr in pairs]
    y_e = [xs_r[pr] + _bdot(a_bk[pr], uv[pr]) for pr in pairs]
    y = jnp.concatenate([y_e[pr][0:R] + y_e[pr][R:R2] for pr in pairs], axis=1)
    for pr in pairs:
        be_bf, ke_bf = e_be[pr].astype(BF16), e_ke[pr].astype(BF16)
        for s in seqs:
            upd = _bdot_tn(jnp.concatenate([seq_rows(u_e[pr], s), seq_rows(v_e[pr], s)], axis=0),
                           jnp.concatenate([seq_rows(be_bf, s), seq_rows(ke_bf, s)], axis=0))
            sbd_ref[s, pr] = sbd_ref[s, pr] * w_all[s * C:s * C + 1, sl[pr]] + upd

    inv_hd = 1.0 / A_HD
    mean = _seg_sum(y) * inv_hd
    yc = y - mean
    var = _seg_sum(yc * yc) * inv_hd
    yn = yc * lax.rsqrt(var + LNX_EPS) * lng_ref[...] + lnb_ref[...]
    yn = yn + bonus * v
    return yn * g


def _rwkv_kernel(p_ref, shift_ref, s0_ref, *rest, chunk, nb, sub):
    wrefs, (y_ref, sout_ref, sbd_ref, carry_ref) = rest[:-4], rest[-4:]
    c_idx = pl.program_id(1)
    pairs = range(A_HEADS // 2)
    rows = nb * chunk
    n_seq = sbd_ref.shape[0]

    @pl.when(c_idx == 0)
    def _():
        carry_ref[...] = shift_ref[...]
        z = jnp.zeros((A_HD, A_HD), F32)
        for s in range(n_seq):
            for pr in pairs:
                sa = s0_ref[s, 2 * pr]
                sb = s0_ref[s, 2 * pr + 1]
                sbd_ref[s, pr] = jnp.concatenate(
                    [jnp.concatenate([sa, z], axis=1), jnp.concatenate([z, sb], axis=1)], axis=0)

    p_all = p_ref[...].astype(F32)
    chunks = [p_all[i * rows:(i + 1) * rows] for i in range(sub)]
    if nb == 1:
        befores = [carry_ref[...]] + [c[rows - 1:rows, :].reshape(1, 1, RWKV_PAD) for c in chunks]
        carry_ref[...] = befores[sub]
        states = [sbd_ref] * sub
    else:
        befores = [carry_ref[i * nb:(i + 1) * nb] for i in range(sub)]
        states = [sbd_ref.at[pl.ds(i * nb, nb)] for i in range(sub)]
    pre = [_rwkv_pre(chunks[i], befores[i], *wrefs[:-2], chunk=chunk, nb=nb) for i in range(sub)]
    for i in range(sub):
        y = _rwkv_mix(pre[i], states[i], *wrefs[-2:], chunk=chunk, nb=nb)
        y_ref[i * rows:(i + 1) * rows, :] = y.astype(y_ref.dtype)

    @pl.when(c_idx == pl.num_programs(1) - 1)
    def _():
        for s in range(n_seq):
            for pr in pairs:
                s_bd = sbd_ref[s, pr]
                sout_ref[s, 2 * pr] = s_bd[0:A_HD, 0:A_HD]
                sout_ref[s, 2 * pr + 1] = s_bd[A_HD:2 * A_HD, A_HD:2 * A_HD]


def _rwkv(proj, shift0, s0, wts, batch, seq, chunk, nb, sub):
    seqs = 1 if nb == 1 else nb * sub
    n_chunks = seq // (chunk * sub) if nb == 1 else 1
    assert nb == 1 or seq == chunk
    rows = nb * chunk * sub
    vec = lambda w: pl.BlockSpec(w.shape, lambda b, c: (0,) * w.ndim)
    st = pl.BlockSpec((seqs, A_HEADS, A_HD, A_HD), lambda b, c: (b, 0, 0, 0))
    return pl.pallas_call(
        functools.partial(_rwkv_kernel, chunk=chunk, nb=nb, sub=sub),
        grid=(batch // seqs, n_chunks),
        in_specs=[pl.BlockSpec((rows, RWKV_PAD), lambda b, c: (b * n_chunks + c, 0)),
                  pl.BlockSpec((seqs, 1, RWKV_PAD), lambda b, c: (b, 0, 0)), st]
                 + [vec(w) for w in wts],
        out_specs=[pl.BlockSpec((rows, A_W), lambda b, c: (b * n_chunks + c, 0)), st],
        out_shape=[jax.ShapeDtypeStruct((batch * seq, A_W), BF16),
                   jax.ShapeDtypeStruct((batch, A_HEADS, A_HD, A_HD), F32)],
        scratch_shapes=[pltpu.VMEM((seqs, A_HEADS // 2, LANES, LANES), F32),
                        pltpu.VMEM((seqs, 1, RWKV_PAD), F32)],
        compiler_params=_cparams(("parallel", "arbitrary")),
        name="rwkv7",
    )(proj, shift0.reshape(batch, 1, RWKV_PAD), s0, *wts)


def _ret_kernel(q0_ref, q1_ref, k0_ref, k1_ref, v0_ref, v1_ref, g0_ref, g1_ref, cos_ref, sin_ref,
                din_ref, dq_ref, dk_ref, dc_ref, r0_ref, y_ref, rout_ref, r_scr, *, chunk, nb):
    C = chunk
    half = RET_BLK // R_HD
    c_idx = pl.program_id(1)

    @pl.when(c_idx == 0)
    def _():
        r_scr[...] = r0_ref[...]

    even = lax.broadcasted_iota(jnp.int32, (1, R_HD), 1) % 2 == 0
    cos = cos_ref[...]
    sin = sin_ref[...]

    def rot(t):
        partner = jnp.where(even, pltpu.roll(t, R_HD - 1, axis=1), pltpu.roll(t, 1, axis=1))
        return t * cos + partner * sin

    def load(halves, s, h):
        return halves[h // half][s * C:(s + 1) * C, (h % half) * R_HD:(h % half + 1) * R_HD]

    f32 = lambda *refs: tuple(r[...].astype(F32) for r in refs)
    items = [(s, h) for s in range(nb) for h in range(R_HEADS)]
    q_in, k_in, v_in, g_in = f32(q0_ref, q1_ref), f32(k0_ref, k1_ref), f32(v0_ref, v1_ref), f32(g0_ref, g1_ref)
    q = [rot(load(q_in, s, h)) for s, h in items]
    k = [rot(load(k_in, s, h)) * (R_HD ** -0.5) for s, h in items]
    v = [load(v_in, s, h).astype(BF16) for s, h in items]
    state = [r_scr[s, h] for s, h in items]
    scores = [_bdot_nt(q[i], k[i]) * din_ref[h] for i, (s, h) in enumerate(items)]
    o = [_bdot(jnp.concatenate([scores[i], q[i] * dq_ref[h]], axis=1),
               jnp.concatenate([v[i], state[i].astype(BF16)], axis=0))
         for i, (s, h) in enumerate(items)]
    for i, (s, h) in enumerate(items):
        r_scr[s, h] = state[i] * dc_ref[h, 0:1, :] + _bdot_tn(k[i] * dk_ref[h], v[i])
    for i, (s, h) in enumerate(items):
        on = o[i] * lax.rsqrt(jnp.mean(o[i] * o[i], axis=-1, keepdims=True) + NORM_EPS)
        gate = load(g_in, s, h)
        y_ref[s * C:(s + 1) * C, h * R_HD:(h + 1) * R_HD] = (
            gate * _sigmoid(gate) * on).astype(y_ref.dtype)

    @pl.when(c_idx == pl.num_programs(1) - 1)
    def _():
        rout_ref[...] = r_scr[...]


def _ret_tables(chunk, pos0, seq):
    f32 = np.float32
    pos = f32(pos0) + np.arange(seq, dtype=f32)
    theta = f32(1.0) / (f32(10000.0) ** np.linspace(0.0, 1.0, R_HD // 2, dtype=f32))
    ang = pos[:, None] * theta[None, :]
    cos2 = np.repeat(np.cos(ang), 2, axis=1)
    sin2 = np.stack([-np.sin(ang), np.sin(ang)], axis=-1).reshape(seq, R_HD)
    lg = np.log1p(-(f32(2.0) ** (f32(-5.0) - np.arange(R_HEADS, dtype=f32))))
    idx = np.arange(chunk, dtype=f32)
    diff = idx[:, None] - idx[None, :]
    causal = diff >= 0
    din = np.where(causal, np.exp(np.where(causal, diff, f32(0.0))[None] * lg[:, None, None]), f32(0.0))
    bc = lambda t: np.broadcast_to(t[:, :, None], (R_HEADS, t.shape[1], R_HD))
    dq = bc(np.exp((idx + f32(1.0))[None, :] * lg[:, None]))
    dk = bc(np.exp((f32(chunk - 1.0) - idx)[None, :] * lg[:, None]))
    dc = np.broadcast_to(np.exp(f32(chunk) * lg)[:, None, None], (R_HEADS, 8, R_HD))
    return tuple(jnp.asarray(np.ascontiguousarray(t, dtype=f32))
                 for t in (cos2, sin2, din, dq, dk, dc))


def _retention(proj, r0, batch, seq, chunk, nb, pos0):
    n_chunks = seq // chunk
    cos2, sin2, din, dq, dk, dc = _ret_tables(chunk, pos0, seq)
    col0 = RWKV_PAD // RET_BLK
    rows = nb * chunk

    def sec(i):
        return pl.BlockSpec((rows, RET_BLK), lambda b, c: (b * n_chunks + c, col0 + i))

    tab = lambda t: pl.BlockSpec(t.shape, lambda b, c: (0, 0, 0))
    st = pl.BlockSpec((nb, R_HEADS, R_HD, R_HD), lambda b, c: (b, 0, 0, 0))
    return pl.pallas_call(
        functools.partial(_ret_kernel, chunk=chunk, nb=nb),
        grid=(batch // nb, n_chunks),
        in_specs=[sec(i) for i in range(8)]
                 + [pl.BlockSpec((chunk, R_HD), lambda b, c: (c, 0)),
                    pl.BlockSpec((chunk, R_HD), lambda b, c: (c, 0)),
                    tab(din), tab(dq), tab(dk), tab(dc), st],
        out_specs=[pl.BlockSpec((rows, R_W), lambda b, c: (b * n_chunks + c, 0)), st],
        out_shape=[jax.ShapeDtypeStruct((batch * seq, R_W), BF16),
                   jax.ShapeDtypeStruct((batch, R_HEADS, R_HD, R_HD), F32)],
        scratch_shapes=[pltpu.VMEM((nb, R_HEADS, R_HD, R_HD), F32)],
        compiler_params=_cparams(("parallel", "arbitrary")),
        name="retention",
    )(*([proj] * 8), cos2, sin2, din, dq, dk, dc, r0)


def _outproj_kernel(x_ref, ya_ref, yb_ref, wa_ref, wb_ref, g_ref, wr_ref,
                    x1_ref, h2_ref, ri_ref, rg_ref, *, parts):
    rows = x_ref.shape[0] // parts
    sl = [slice(i * rows, (i + 1) * rows) for i in range(parts)]
    x1s = [x_ref[s, :] + jnp.dot(ya_ref[s, :], wa_ref[...], preferred_element_type=F32)
           + jnp.dot(yb_ref[s, :], wb_ref[...], preferred_element_type=F32) for s in sl]
    for s, x1 in zip(sl, x1s):
        x1_ref[s, :] = x1
        ms = jnp.mean(x1 * x1, axis=-1, keepdims=True)
        h2 = x1 * lax.rsqrt(ms + NORM_EPS) * g_ref[...]
        h2_ref[s, :] = _pack_bf16_pairs(h2)
        ri_ref[s, :], rg_ref[s, :] = _route(_bdot(h2, wr_ref[...]))


def _route(logits):
    lane = lax.broadcasted_iota(jnp.int32, (1, LANES), 1)
    neg = -1e30
    big = LANES
    rmax = lambda t: jnp.max(t, axis=-1, keepdims=True)
    rmin = lambda t: jnp.min(t, axis=-1, keepdims=True)
    is_c = lane < N_GROUPS
    lc = jnp.where(is_c, logits, neg)
    mc = rmax(lc)
    grp = rmin(jnp.where(lc == mc, lane, big))
    p_grp = 1.0 / jnp.sum(jnp.where(is_c, jnp.exp(lc - mc), 0.0), axis=-1, keepdims=True)
    fine = lane - N_GROUPS
    in_g = (fine >= 0) & (fine < N_EXPERTS) & ((fine // E_PER_GROUP) == grp)
    lf = jnp.where(in_g, logits, neg)
    m1 = rmax(lf)
    i1 = rmin(jnp.where(lf == m1, lane, big))
    lf2 = jnp.where(lane == i1, neg, lf)
    m2 = rmax(lf2)
    i2 = rmin(jnp.where(lf2 == m2, lane, big))
    e2 = jnp.exp(m2 - m1)
    g1 = p_grp / (1.0 + e2)
    g2 = p_grp * e2 / (1.0 + e2)
    return (jnp.where(lane == 0, i1 - N_GROUPS, jnp.where(lane == 1, i2 - N_GROUPS, 0)),
            jnp.where(lane == 0, g1, jnp.where(lane == 1, g2, 0.0)))


def _outproj(x, ya, yb, wa, wb, g, wr, tm=512, parts=2):
    n, d = x.shape
    rowblk = lambda w: pl.BlockSpec((tm, w), lambda i: (i, 0))
    full = lambda w: pl.BlockSpec(w.shape, lambda i: (0, 0))
    return pl.pallas_call(
        functools.partial(_outproj_kernel, parts=parts),
        grid=(n // tm,),
        in_specs=[rowblk(d), rowblk(A_W), rowblk(R_W), full(wa), full(wb), full(g), full(wr)],
        out_specs=[rowblk(d), rowblk(d // 2), rowblk(LANES), rowblk(LANES)],
        out_shape=[jax.ShapeDtypeStruct((n, d), F32), jax.ShapeDtypeStruct((n, d // 2), jnp.uint32),
                   jax.ShapeDtypeStruct((n, LANES), jnp.int32),
                   jax.ShapeDtypeStruct((n, LANES), F32)],
        compiler_params=_cparams(("parallel",)),
        name="outproj_router",
    )(x, ya, yb, wa, wb, g, wr)


def _sort_kernel(ri_ref, tri_ref, dest_ref, info_ref, cnt_ref, base_ref):
    ph = pl.program_id(0)
    i = pl.program_id(1)
    lane = lax.broadcasted_iota(jnp.int32, (1, LANES), 1)
    ri = ri_ref[...]
    oh1 = (lane == ri[:, 0:1]).astype(F32)
    oh2 = (lane == ri[:, 1:2]).astype(F32)
    tot1 = jnp.sum(oh1, axis=0, keepdims=True)
    tot2 = jnp.sum(oh2, axis=0, keepdims=True)

    @pl.when(ph == 0)
    def _():
        @pl.when(i == 0)
        def _():
            cnt_ref[...] = jnp.zeros_like(cnt_ref)
        cnt_ref[...] += tot1 + tot2

    @pl.when(ph == 1)
    def _():
        @pl.when(i == 0)
        def _():
            nblk = jnp.floor((cnt_ref[...] + (MOE_BLK - 1)) * (1.0 / MOE_BLK))
            upper = (lax.broadcasted_iota(jnp.int32, (LANES, LANES), 0) <=
                     lax.broadcasted_iota(jnp.int32, (LANES, LANES), 1))
            pend = _bdot(jnp.broadcast_to(nblk, (8, LANES)), upper.astype(F32))[0:1]
            first = pend - nblk
            base_ref[...] = first * MOE_BLK
            srow = lax.broadcasted_iota(jnp.int32, (8, LANES), 0)
            info_ref[...] = jnp.where(srow == 0, nblk, jnp.where(srow == 1, first, 0.0)).astype(jnp.int32)

        base = base_ref[...]
        tri = tri_ref[...]
        pre1 = jnp.dot(tri, oh1.astype(BF16), preferred_element_type=F32)
        pre2 = jnp.dot(tri, oh2.astype(BF16), preferred_element_type=F32) + tot1
        d1 = jnp.sum(oh1 * (base + pre1), axis=1, keepdims=True)
        d2 = jnp.sum(oh2 * (base + pre2), axis=1, keepdims=True)
        dest_ref[...] = jnp.where(lane == 0, d1, jnp.where(lane == 1, d2, 0.0)).astype(jnp.int32)
        base_ref[...] = base + tot1 + tot2


def _sort(ri, tt=1024):
    n = ri.shape[0]
    tri = (jnp.arange(tt)[:, None] > jnp.arange(tt)[None, :]).astype(BF16)
    return pl.pallas_call(
        _sort_kernel,
        grid=(2, n // tt),
        in_specs=[pl.BlockSpec((tt, LANES), lambda ph, i: (i, 0)),
                  pl.BlockSpec((tt, tt), lambda ph, i: (0, 0))],
        out_specs=[pl.BlockSpec((tt, LANES), lambda ph, i: (i * ph, 0)),
                   pl.BlockSpec((8, LANES), lambda ph, i: (0, 0))],
        out_shape=[jax.ShapeDtypeStruct((n, LANES), jnp.int32),
                   jax.ShapeDtypeStruct((8, LANES), jnp.int32)],
        scratch_shapes=[pltpu.VMEM((1, LANES), F32), pltpu.VMEM((1, LANES), F32)],
        compiler_params=_cparams(("arbitrary", "arbitrary")),
        name="moe_sort",
    )(ri, tri)


def _row_copy(src_ref, src_row, dst_ref, dst_row, sem):
    return pltpu.make_async_copy(src_ref.at[pl.ds(src_row, 1)], dst_ref.at[pl.ds(dst_row, 1)], sem)


def _dispatch_kernel(d1_ref, d2_ref, h_ref, xs_in_ref, xs_ref, sem, *, tt):
    del xs_in_ref

    def issue(t, carry):
        _row_copy(h_ref, t, xs_ref, d1_ref[t], sem).start()
        _row_copy(h_ref, t, xs_ref, d2_ref[t], sem).start(priority=1)
        return carry

    def drain(t, carry):
        _row_copy(h_ref, t, xs_ref, d1_ref[t], sem).wait()
        _row_copy(h_ref, t, xs_ref, d2_ref[t], sem).wait()
        return carry

    lax.fori_loop(0, tt, issue, 0, unroll=8)
    lax.fori_loop(0, tt, drain, 0, unroll=8)


def _dispatch(d1, d2, h2, xs, tt=1024):
    n, d = h2.shape
    smem = pl.BlockSpec((tt,), lambda i: (i,), memory_space=pltpu.SMEM)
    anyspec = pl.BlockSpec(memory_space=pl.ANY)
    return pl.pallas_call(
        functools.partial(_dispatch_kernel, tt=tt),
        grid=(n // tt,),
        in_specs=[smem, smem, pl.BlockSpec((tt, d), lambda i: (i, 0)), anyspec],
        out_specs=anyspec,
        out_shape=jax.ShapeDtypeStruct(xs.shape, xs.dtype),
        scratch_shapes=[pltpu.SemaphoreType.DMA(())],
        input_output_aliases={3: 0},
        compiler_params=_cparams(("arbitrary",)),
        name="moe_dispatch",
    )(d1, d2, h2, xs)


def _expert_kernel(nb_ref, fb_ref, xs_hbm, wg_hbm, wu_hbm, wd_hbm, y_hbm,
                   wg_r, wu_r, wd_r, wg_s, wu_s, wd_s, xbuf, ybuf, sem_w, sem_in, sem_out, *, n_blk):
    e = pl.program_id(0)
    last = pl.num_programs(0) - 1
    n = nb_ref[e]
    b0 = fb_ref[e]
    n_used = fb_ref[last] + nb_ref[last]
    half = xbuf.shape[2]
    ring = wg_r.shape[0]

    def fetch(ex):
        slot = ex % ring
        return [pltpu.make_async_copy(src.at[ex], dst.at[slot], sem_w.at[slot])
                for src, dst in ((wg_hbm, wg_r), (wu_hbm, wu_r), (wd_hbm, wd_r))]

    @pl.when(e == 0)
    def _():
        for ex in range(ring - 1):
            for cp in fetch(ex):
                cp.start(priority=W_DMA_PRIORITY)

    @pl.when(e + ring - 1 <= last)
    def _():
        for cp in fetch(e + ring - 1):
            cp.start(priority=W_DMA_PRIORITY)

    for cp in fetch(e):
        cp.wait()
    w_slot = e % ring

    def load(blk, slot):
        return pltpu.make_async_copy(xs_hbm.at[pl.ds(blk * MOE_BLK, MOE_BLK)],
                                     xbuf.at[slot], sem_in.at[slot])

    def store(blk, slot):
        return pltpu.make_async_copy(ybuf.at[slot], y_hbm.at[pl.ds(blk * MOE_BLK, MOE_BLK)],
                                     sem_out.at[slot])

    @pl.when(n > 0)
    def _():
        @pl.when(b0 == 0)
        def _():
            load(0, 0).start()

        wg_s[...] = wg_r[w_slot].astype(BF16)
        wu_s[...] = wu_r[w_slot].astype(BF16)
        wd_s[...] = wd_r[w_slot].astype(BF16)

        def body(blk, carry):
            slot = blk % 2
            load(blk, slot).wait()

            @pl.when(blk + 1 < n_used)
            def _():
                load(blk + 1, 1 - slot).start()

            @pl.when(blk >= 2)
            def _():
                store(blk - 2, slot).wait()

            x_lo, x_hi = _unpack_bf16_pairs(xbuf[slot])
            x_lo, x_hi = x_lo.astype(BF16), x_hi.astype(BF16)
            mm = lambda w_s: (jnp.dot(x_lo, w_s[0:half], preferred_element_type=F32)
                              + jnp.dot(x_hi, w_s[half:2 * half], preferred_element_type=F32))
            hg = mm(wg_s)
            hu = mm(wu_s)
            act = (hg * _sigmoid(hg) * hu).astype(BF16)
            ybuf[slot] = _pack_bf16_pairs(jnp.dot(act, wd_s[...], preferred_element_type=F32))
            store(blk, slot).start()
            return carry

        lax.fori_loop(b0, b0 + n, body, 0)

    @pl.when(e == last)
    def _():
        @pl.when(n_used >= 2)
        def _():
            store(n_used - 2, n_used % 2).wait()

        @pl.when(n_used >= 1)
        def _():
            store(n_used - 1, (n_used - 1) % 2).wait()

        ybuf[0] = jnp.zeros(ybuf.shape[1:], ybuf.dtype)

        def fill(blk, carry):
            cp = store(blk, 0)
            cp.start()
            cp.wait()
            return carry

        lax.fori_loop(n_used, n_blk, fill, 0)


def _experts(n_blocks, first_block, xs, wg, wu, wd):
    p, half = xs.shape
    d = 2 * half
    anyspec = pl.BlockSpec(memory_space=pl.ANY)
    ring = W_AHEAD + 1
    grid_spec = pltpu.PrefetchScalarGridSpec(
        num_scalar_prefetch=2,
        grid=(N_EXPERTS,),
        in_specs=[anyspec] * 4,
        out_specs=anyspec,
        scratch_shapes=[pltpu.VMEM((ring, d, D_EXPERT), F32), pltpu.VMEM((ring, d, D_EXPERT), F32),
                        pltpu.VMEM((ring, D_EXPERT, d), F32),
                        pltpu.VMEM((d, D_EXPERT), BF16), pltpu.VMEM((d, D_EXPERT), BF16),
                        pltpu.VMEM((D_EXPERT, d), BF16),
                        pltpu.VMEM((2, MOE_BLK, half), jnp.uint32),
                        pltpu.VMEM((2, MOE_BLK, half), jnp.uint32),
                        pltpu.SemaphoreType.DMA((ring,)),
                        pltpu.SemaphoreType.DMA((2,)), pltpu.SemaphoreType.DMA((2,))])
    return pl.pallas_call(
        functools.partial(_expert_kernel, n_blk=p // MOE_BLK),
        grid_spec=grid_spec,
        out_shape=jax.ShapeDtypeStruct((p, half), jnp.uint32),
        compiler_params=_cparams(("arbitrary",)),
        name="moe_experts",
    )(n_blocks, first_block, xs, wg, wu, wd)


def _combine_kernel(d1_ref, d2_ref, d1n_ref, d2n_ref, x1_ref, rg_ref, g_ref, yb_ref, o_ref, buf, sem,
                    *, tt):
    i = pl.program_id(0)
    slot = i % 2

    def gather(a_ref, b_ref, s, wait):
        def body(t, carry):
            for j, idx in enumerate((a_ref, b_ref)):
                cp = _row_copy(yb_ref, idx[t], buf.at[s, j], t, sem.at[s])
                cp.wait() if wait else cp.start(priority=j)
            return carry
        lax.fori_loop(0, tt, body, 0, unroll=8)

    @pl.when(i == 0)
    def _():
        gather(d1_ref, d2_ref, 0, False)

    @pl.when(i + 1 < pl.num_programs(0))
    def _():
        gather(d1n_ref, d2n_ref, 1 - slot, False)

    gather(d1_ref, d2_ref, slot, True)
    rg = rg_ref[...]
    lo1, hi1 = _unpack_bf16_pairs(buf[slot, 0])
    lo2, hi2 = _unpack_bf16_pairs(buf[slot, 1])
    g1, g2 = rg[:, 0:1], rg[:, 1:2]
    x = x1_ref[...] + jnp.concatenate([g1 * lo1 + g2 * lo2, g1 * hi1 + g2 * hi2], axis=1)
    ms = jnp.mean(x * x, axis=-1, keepdims=True)
    o_ref[...] = x * lax.rsqrt(ms + NORM_EPS) * g_ref[...]


def _combine(d1, d2, x1, rg, g, yb, tt=256):
    n, d = x1.shape
    steps = n // tt
    smem = pl.BlockSpec((tt,), lambda i: (i,), memory_space=pltpu.SMEM)
    smem_next = pl.BlockSpec((tt,), lambda i: (jnp.minimum(i + 1, steps - 1),),
                             memory_space=pltpu.SMEM)
    rowblk = lambda w: pl.BlockSpec((tt, w), lambda i: (i, 0))
    return pl.pallas_call(
        functools.partial(_combine_kernel, tt=tt),
        grid=(steps,),
        in_specs=[smem, smem, smem_next, smem_next, rowblk(d), rowblk(LANES),
                  pl.BlockSpec((1, d), lambda i: (0, 0)), pl.BlockSpec(memory_space=pl.ANY)],
        out_specs=rowblk(d),
        out_shape=jax.ShapeDtypeStruct((n, d), F32),
        scratch_shapes=[pltpu.VMEM((2, 2, tt, d // 2), jnp.uint32), pltpu.SemaphoreType.DMA((2,))],
        compiler_params=_cparams(("arbitrary",)),
        name="moe_combine",
    )(d1, d2, d1, d2, x1, rg, g, yb)


def _pad_cols(x, width):
    return jnp.pad(x, ((0, 0), (0, width - x.shape[1])))


def _rwkv_layout(x):
    return jnp.concatenate([
        x[:, :OFF_WD],
        _pad_cols(x[:, 3 * A_W:3 * A_W + W_LORA], LANES),
        _pad_cols(x[:, 3 * A_W + W_LORA:3 * A_W + W_LORA + A_LORA], LANES),
        _pad_cols(x[:, 3 * A_W + W_LORA + A_LORA:RWKV_PROJ], 2 * LANES)], axis=1)


def _rwkv_unlayout(x):
    return jnp.concatenate([x[:, :OFF_WD + W_LORA], x[:, OFF_AD:OFF_AD + A_LORA],
                            x[:, OFF_GD:OFF_GD + G_LORA]], axis=1)


def _pad_rows(w, rows):
    return jnp.pad(w, ((0, rows - w.shape[0]), (0, 0)))


def kernel(x_prompt, x_sample, state_rwkv_shift, state_rwkv_wkv, state_retention, norm_mix_g, w_in, mu_shift, decay_w0, decay_up, iclr_a0, iclr_up, gate_up, k_k, k_a, r_k, lnx_g, lnx_b, w_out, norm_ffn_g, router_coarse, router_fine, exp_w_gate, exp_w_up, exp_w_down, norm_final_g):
    bp, tp, d = x_prompt.shape
    bs, ts, _ = x_sample.shape
    row = lambda t: t.reshape(1, -1)
    l = 0

    w_in_p = jnp.concatenate([_rwkv_layout(w_in[l][:, :RWKV_PROJ]), w_in[l][:, RWKV_PROJ:]],
                             axis=1).astype(BF16)
    rwkv_wts = (_rwkv_layout(row(mu_shift[l])), row(decay_w0[l]),
                _pad_rows(decay_up[l], LANES).astype(BF16), row(iclr_a0[l]),
                _pad_rows(iclr_up[l], LANES).astype(BF16),
                _pad_rows(gate_up[l], 2 * LANES).astype(BF16),
                row(k_k[l]), row(k_a[l]), row(r_k[l]), row(lnx_g[l]), row(lnx_b[l]))
    w_out_a = w_out[l][:A_W].astype(BF16)
    w_out_b = w_out[l][A_W:].astype(BF16)
    w_router = _pad_cols(jnp.concatenate(
        [router_coarse[l], jnp.transpose(router_fine[l], (1, 0, 2)).reshape(d, N_EXPERTS)], axis=1),
        LANES).astype(BF16)

    groups = (
        (x_prompt.reshape(bp * tp, d), bp, tp, WKV_CHUNK, RET_CHUNK, 1, 0.0,
         jnp.zeros((bp, RWKV_PAD), F32), jnp.zeros((bp, A_HEADS, A_HD, A_HD), F32),
         jnp.zeros((bp, R_HEADS, R_HD, R_HD), F32)),
        (x_sample.reshape(bs * ts, d), bs, ts, ts, math.gcd(ts, RET_CHUNK), 8, float(PAST_LEN),
         _rwkv_layout(state_rwkv_shift[l]), state_rwkv_wkv[l], state_retention[l]),
    )
    x_last = jnp.concatenate([x_prompt[:, -1, :], x_sample[:, -1, :]], axis=0)
    n_last = -(-(bp + bs) // 8) * 8
    shifts = _rwkv_unlayout(_inproj(_pad_rows(x_last, n_last), row(norm_mix_g[l]), w_in_p, F32,
                                    RWKV_PAD, n_last, RWKV_PAD // 2))
    new_shifts = (shifts[:bp], shifts[bp:bp + bs])

    mixed = []
    for (x, b, t, wkv_chunk, ret_chunk, nb, pos0, shift0, wkv0, ret0), new_shift in zip(groups, new_shifts):
        proj = _inproj(x, row(norm_mix_g[l]), w_in_p, BF16, PROJ_PAD, 1024, 1536)
        y_a, new_wkv = _rwkv(proj, shift0, wkv0, rwkv_wts, b, t, wkv_chunk, nb,
                             WKV_SUB if nb == 1 else WKV_GROUPS)
        y_b, new_ret = _retention(proj, ret0, b, t, ret_chunk, nb, pos0)
        x1, h2, ri, rg = _outproj(x, y_a, y_b, w_out_a, w_out_b, row(norm_ffn_g[l]), w_router)
        mixed.append((x1, h2, ri, rg, new_shift, new_wkv, new_ret))

    n_tok = [m[0].shape[0] for m in mixed]
    n_all = sum(n_tok)
    n_blk = -(-2 * n_all // MOE_BLK) + N_EXPERTS
    dest, info = _sort(jnp.concatenate([m[2] for m in mixed], axis=0))
    n_blocks, first_block = info[0, :N_EXPERTS], info[1, :N_EXPERTS]
    xs = jnp.zeros((n_blk * MOE_BLK, d // 2), jnp.uint32)
    off = 0
    dests = []
    for m, n in zip(mixed, n_tok):
        d1, d2 = dest[off:off + n, 0], dest[off:off + n, 1]
        dests.append((d1, d2))
        xs = _dispatch(d1, d2, m[1], xs)
        off += n
    yb = _experts(n_blocks, first_block, xs, exp_w_gate[l], exp_w_up[l], exp_w_down[l])
    outs = [_combine(d1, d2, m[0], m[3], row(norm_final_g), yb)
            for m, (d1, d2) in zip(mixed, dests)]

    (_, _, _, _, p_shift, p_wkv, p_ret), (_, _, _, _, s_shift, s_wkv, s_ret) = mixed
    return (outs[0].reshape(bp, tp, d), outs[1].reshape(bs, ts, d),
            p_shift[None], p_wkv[None], p_ret[None], s_shift[None], s_wkv[None], s_ret[None])
```

```python
import functools
import math

import jax
import jax.numpy as jnp
import numpy as np
from jax import lax
from jax.experimental import pallas as pl
from jax.experimental.pallas import tpu as pltpu

F32 = jnp.float32
BF16 = jnp.bfloat16

D_MODEL = 2048
A_W = 1024
A_HD = 64
A_HEADS = 16
W_LORA = 64
A_LORA = 64
G_LORA = 160
RWKV_PROJ = 3 * A_W + W_LORA + A_LORA + G_LORA
R_W = 1024
R_HEADS = 8
R_HD = 128
RET_CHUNK = 128
N_GROUPS = 4
E_PER_GROUP = 8
N_EXPERTS = 32
D_EXPERT = 512
NORM_EPS = 1e-6
LNX_EPS = 64e-5
DECAY_SCALE = math.exp(-0.5)
PAST_LEN = 16384

LANES = 128
OFF_WD = 3 * A_W
OFF_AD = OFF_WD + LANES
OFF_GD = OFF_AD + LANES
RWKV_PAD = OFF_GD + 2 * LANES
PROJ_PAD = RWKV_PAD + 4 * R_W
RET_BLK = 512
WKV_CHUNK = 64
WKV_SUB = 8
WKV_GROUPS = 2
MOE_BLK = 256
W_AHEAD = 2
W_DMA_PRIORITY = 1
VMEM_LIMIT = 56 * 1024 * 1024


def _cparams(sem):
    return pltpu.CompilerParams(dimension_semantics=sem, vmem_limit_bytes=VMEM_LIMIT)


def _bdot(a, b):
    return jnp.dot(a.astype(BF16), b.astype(BF16), preferred_element_type=F32)


def _bdot_nt(a, b):
    return lax.dot_general(a.astype(BF16), b.astype(BF16), (((1,), (1,)), ((), ())),
                           preferred_element_type=F32)


def _bdot_tn(a, b):
    return lax.dot_general(a.astype(BF16), b.astype(BF16), (((0,), (0,)), ((), ())),
                           preferred_element_type=F32)


def _dot_split(x, w_bf16):
    hi = x.astype(BF16)
    lo = (x - hi.astype(F32)).astype(BF16)
    d = lambda p: jnp.dot(p, w_bf16, preferred_element_type=F32)
    return d(hi) + d(lo)


def _sigmoid(x):
    return 1.0 / (1.0 + jnp.exp(-x))


def _pack_bf16_pairs(x):
    w = x.shape[1] // 2
    lo = lax.bitcast_convert_type(x[:, :w].astype(BF16).astype(F32), jnp.uint32)
    hi = lax.bitcast_convert_type(x[:, w:].astype(BF16).astype(F32), jnp.uint32)
    return hi | (lo >> 16)


def _unpack_bf16_pairs(p):
    lo = lax.bitcast_convert_type(p << 16, F32)
    hi = lax.bitcast_convert_type(p & jnp.uint32(0xFFFF0000), F32)
    return lo, hi


def _inproj_kernel(x_ref, g_ref, w_ref, o_ref, h_ref):
    @pl.when(pl.program_id(1) == 0)
    def _():
        x = x_ref[...]
        ms = jnp.mean(x * x, axis=-1, keepdims=True)
        h_ref[...] = (x * lax.rsqrt(ms + NORM_EPS) * g_ref[...]).astype(BF16)

    o_ref[...] = jnp.dot(h_ref[...], w_ref[...], preferred_element_type=F32).astype(o_ref.dtype)


def _inproj(x, g, w_bf16, out_dtype, n_cols, tm, tn):
    n, d = x.shape
    return pl.pallas_call(
        _inproj_kernel,
        grid=(n // tm, n_cols // tn),
        in_specs=[pl.BlockSpec((tm, d), lambda i, j: (i, 0)),
                  pl.BlockSpec((1, d), lambda i, j: (0, 0)),
                  pl.BlockSpec((d, tn), lambda i, j: (0, j))],
        out_specs=pl.BlockSpec((tm, tn), lambda i, j: (i, j)),
        out_shape=jax.ShapeDtypeStruct((n, n_cols), out_dtype),
        scratch_shapes=[pltpu.VMEM((tm, d), BF16)],
        compiler_params=_cparams(("parallel", "arbitrary")),
        name="inproj",
    )(x, g, w_bf16)


def _seg_sum(x):
    ones_bd = (lax.broadcasted_iota(jnp.int32, (LANES, LANES), 0) // A_HD ==
               lax.broadcasted_iota(jnp.int32, (LANES, LANES), 1) // A_HD).astype(BF16)
    nslab = x.shape[1] // LANES
    stacked = jnp.concatenate([x[:, s * LANES:(s + 1) * LANES] for s in range(nslab)], axis=0)
    red = _dot_split(stacked, ones_bd)
    c = x.shape[0]
    return jnp.concatenate([red[s * c:(s + 1) * c] for s in range(nslab)], axis=1)


def _rwkv_pre(p, before, mu_ref, w0_ref, wup_ref, a0_ref, aup_ref, gup_ref, kk_ref, ka_ref, rk_ref,
              *, chunk, nb):
    C = chunk
    R = nb * C
    row = lax.broadcasted_iota(jnp.int32, (R, 1), 0)
    pos = row % C
    first = jnp.broadcast_to(before, (nb, C, RWKV_PAD)).reshape(R, RWKV_PAD)
    prev = jnp.where(pos == 0, first, pltpu.roll(p, 1, axis=0))
    xm = p + (prev - p) * mu_ref[...]

    r = xm[:, 0:A_W]
    k = xm[:, A_W:2 * A_W]
    v = xm[:, 2 * A_W:3 * A_W]
    wd = xm[:, OFF_WD:OFF_AD]
    ad = xm[:, OFF_AD:OFF_GD]
    gd = xm[:, OFF_GD:RWKV_PAD]

    logw = -DECAY_SCALE * _sigmoid(w0_ref[...] + _bdot(jnp.tanh(wd), wup_ref[...]))
    a = _sigmoid(a0_ref[...] + _bdot(ad, aup_ref[...]))
    g = _bdot(_sigmoid(gd), gup_ref[...])

    kk = k * kk_ref[...]
    k2 = k * (1.0 + (a - 1.0) * ka_ref[...])
    sums = _seg_sum(jnp.concatenate([kk * kk, r * k2 * rk_ref[...]], axis=0))
    kk = kk / jnp.maximum(jnp.sqrt(sums[0:R]), 1e-12)
    bonus = sums[R:2 * R]
    b = kk * a

    cum = logw
    suf = logw
    d = 1
    while d < C:
        cum = cum + jnp.where(pos >= d, pltpu.roll(cum, d, axis=0), 0.0)
        if nb > 1:
            suf = suf + jnp.where(pos < C - d, pltpu.roll(suf, R - d, axis=0), 0.0)
        d *= 2
    after = suf - logw if nb > 1 else cum[C - 1:C, :] - cum
    w_incl = jnp.exp(cum)
    w_excl = jnp.exp(cum - logw)
    w_inv = jnp.exp(-cum)
    w_end = jnp.exp(after)
    w_all = jnp.exp(cum + after)

    kk_d = kk * w_excl
    r_d = r * w_incl
    b_h = b * w_inv
    k_h = k2 * w_inv
    b_e = b * w_end
    k_e = k2 * w_end
    return kk_d, r_d, b_h, k_h, b_e, k_e, v, w_all, bonus, g


def _rwkv_mix(pre, sbd_ref, lng_ref, lnb_ref, *, chunk, nb):
    kk_d, r_d, b_h, k_h, b_e, k_e, v, w_all, bonus, g = pre
    C = chunk
    R = nb * C
    pairs = range(A_HEADS // 2)
    seqs = range(nb)
    in_a = lax.broadcasted_iota(jnp.int32, (1, LANES), 1) < A_HD
    R2 = 2 * R
    ri = lax.broadcasted_iota(jnp.int32, (R2, R2), 0)
    ci = lax.broadcasted_iota(jnp.int32, (R2, R2), 1)
    same = (ri // C) == (ci // C)
    strict = same & (ci < ri)
    incl = same & (ci <= ri)
    eye = (ri == ci).astype(F32)

    def expand(x):
        return jnp.concatenate([jnp.where(in_a, x, 0.0), jnp.where(in_a, 0.0, x)], axis=0)

    def seq_rows(x, s):
        if nb == 1:
            return x
        return jnp.concatenate([x[s * C:(s + 1) * C], x[R + s * C:R + (s + 1) * C]], axis=0)

    def from_seqs(parts):
        if nb == 1:
            return parts[0]
        return jnp.concatenate([q[0:C] for q in parts] + [q[C:2 * C] for q in parts], axis=0)

    sl = [slice(pr * LANES, (pr + 1) * LANES) for pr in pairs]
    ex = lambda x: [expand(x[:, sl[pr]]) for pr in pairs]
    e_kk, e_r, e_bh, e_kh, e_be, e_ke = ex(kk_d), ex(r_d), ex(b_h), ex(k_h), ex(b_e), ex(k_e)
    v_e = [t.astype(BF16) for t in ex(v)]
    kk_bf = [t.astype(BF16) for t in e_kk]
    r_bf = [t.astype(BF16) for t in e_r]
    gram = [_bdot_nt(jnp.concatenate([kk_bf[pr], r_bf[pr]], axis=0),
                     jnp.concatenate([e_bh[pr], e_kh[pr]], axis=0)) for pr in pairs]
    l_b = [jnp.where(strict, gram[pr][0:R2, 0:R2], 0.0) for pr in pairs]
    l_k = [jnp.where(strict, gram[pr][0:R2, R2:2 * R2], 0.0) for pr in pairs]
    a_bk = [jnp.concatenate([jnp.where(incl, gram[pr][R2:2 * R2, 0:R2], 0.0),
                             jnp.where(incl, gram[pr][R2:2 * R2, R2:2 * R2], 0.0)],
                            axis=1).astype(BF16) for pr in pairs]
    t_inv = [eye - l_b[pr] for pr in pairs]
    pw = [t.astype(BF16) for t in l_b]
    e = 2
    while e < C:
        pw = [_bdot(pw[pr], pw[pr]).astype(BF16) for pr in pairs]
        t_inv = [t_inv[pr] + _bdot(t_inv[pr], pw[pr]) for pr in pairs]
        e *= 2
    lkv = [_bdot(l_k[pr], v_e[pr]) for pr in pairs]
    xs_kk, xs_r = [], []
    for pr in pairs:
        parts = [_bdot_nt(jnp.concatenate([seq_rows(kk_bf[pr], s), seq_rows(r_bf[pr], s)], axis=0),
                          sbd_ref[s, pr]) for s in seqs]
        xs_kk.append(from_seqs([q[0:2 * C] for q in parts]))
        xs_r.append(from_seqs([q[2 * C:4 * C] for q in parts]))
    u_e = [-_bdot(t_inv[pr], xs_kk[pr] + lkv[pr]).astype(BF16) for pr in pairs]
    uv = [jnp.concatenate([u_e[pr], v_e[pr]], axis=0) for pr in pairs]
    y_e = [xs_r[pr] + _bdot(a_bk[pr], uv[pr]) for pr in pairs]
    y = jnp.concatenate([y_e[pr][0:R] + y_e[pr][R:R2] for pr in pairs], axis=1)
    for pr in pairs:
        be_bf, ke_bf = e_be[pr].astype(BF16), e_ke[pr].astype(BF16)
        for s in seqs:
            upd = _bdot_tn(jnp.concatenate([seq_rows(u_e[pr], s), seq_rows(v_e[pr], s)], axis=0),
                           jnp.concatenate([seq_rows(be_bf, s), seq_rows(ke_bf, s)], axis=0))
            sbd_ref[s, pr] = sbd_ref[s, pr] * w_all[s * C:s * C + 1, sl[pr]] + upd

    inv_hd = 1.0 / A_HD
    mean = _seg_sum(y) * inv_hd
    yc = y - mean
    var = _seg_sum(yc * yc) * inv_hd
    yn = yc * lax.rsqrt(var + LNX_EPS) * lng_ref[...] + lnb_ref[...]
    yn = yn + bonus * v
    return yn * g


def _rwkv_kernel(p_ref, shift_ref, s0_ref, *rest, chunk, nb, sub):
    wrefs, (y_ref, sout_ref, sbd_ref, carry_ref) = rest[:-4], rest[-4:]
    c_idx = pl.program_id(1)
    pairs = range(A_HEADS // 2)
    rows = nb * chunk
    n_seq = sbd_ref.shape[0]

    @pl.when(c_idx == 0)
    def _():
        carry_ref[...] = shift_ref[...]
        z = jnp.zeros((A_HD, A_HD), F32)
        for s in range(n_seq):
            for pr in pairs:
                sa = s0_ref[s, 2 * pr]
                sb = s0_ref[s, 2 * pr + 1]
                sbd_ref[s, pr] = jnp.concatenate(
                    [jnp.concatenate([sa, z], axis=1), jnp.concatenate([z, sb], axis=1)], axis=0)

    p_all = p_ref[...].astype(F32)
    chunks = [p_all[i * rows:(i + 1) * rows] for i in range(sub)]
    if nb == 1:
        befores = [carry_ref[...]] + [c[rows - 1:rows, :].reshape(1, 1, RWKV_PAD) for c in chunks]
        carry_ref[...] = befores[sub]
        states = [sbd_ref] * sub
    else:
        befores = [carry_ref[i * nb:(i + 1) * nb] for i in range(sub)]
        states = [sbd_ref.at[pl.ds(i * nb, nb)] for i in range(sub)]
    pre = [_rwkv_pre(chunks[i], befores[i], *wrefs[:-2], chunk=chunk, nb=nb) for i in range(sub)]
    for i in range(sub):
        y = _rwkv_mix(pre[i], states[i], *wrefs[-2:], chunk=chunk, nb=nb)
        y_ref[i * rows:(i + 1) * rows, :] = y.astype(y_ref.dtype)

    @pl.when(c_idx == pl.num_programs(1) - 1)
    def _():
        for s in range(n_seq):
            for pr in pairs:
                s_bd = sbd_ref[s, pr]
                sout_ref[s, 2 * pr] = s_bd[0:A_HD, 0:A_HD]
                sout_ref[s, 2 * pr + 1] = s_bd[A_HD:2 * A_HD, A_HD:2 * A_HD]


def _rwkv(proj, shift0, s0, wts, batch, seq, chunk, nb, sub):
    seqs = 1 if nb == 1 else nb * sub
    n_chunks = seq // (chunk * sub) if nb == 1 else 1
    assert nb == 1 or seq == chunk
    rows = nb * chunk * sub
    vec = lambda w: pl.BlockSpec(w.shape, lambda b, c: (0,) * w.ndim)
    st = pl.BlockSpec((seqs, A_HEADS, A_HD, A_HD), lambda b, c: (b, 0, 0, 0))
    return pl.pallas_call(
        functools.partial(_rwkv_kernel, chunk=chunk, nb=nb, sub=sub),
        grid=(batch // seqs, n_chunks),
        in_specs=[pl.BlockSpec((rows, RWKV_PAD), lambda b, c: (b * n_chunks + c, 0)),
                  pl.BlockSpec((seqs, 1, RWKV_PAD), lambda b, c: (b, 0, 0)), st]
                 + [vec(w) for w in wts],
        out_specs=[pl.BlockSpec((rows, A_W), lambda b, c: (b * n_chunks + c, 0)), st],
        out_shape=[jax.ShapeDtypeStruct((batch * seq, A_W), BF16),
                   jax.ShapeDtypeStruct((batch, A_HEADS, A_HD, A_HD), F32)],
        scratch_shapes=[pltpu.VMEM((seqs, A_HEADS // 2, LANES, LANES), F32),
                        pltpu.VMEM((seqs, 1, RWKV_PAD), F32)],
        compiler_params=_cparams(("parallel", "arbitrary")),
        name="rwkv7",
    )(proj, shift0.reshape(batch, 1, RWKV_PAD), s0, *wts)


def _ret_kernel(q0_ref, q1_ref, k0_ref, k1_ref, v0_ref, v1_ref, g0_ref, g1_ref, cos_ref, sin_ref,
                din_ref, dq_ref, dk_ref, dc_ref, r0_ref, y_ref, rout_ref, r_scr, *, chunk, nb):
    C = chunk
    half = RET_BLK // R_HD
    c_idx = pl.program_id(1)

    @pl.when(c_idx == 0)
    def _():
        r_scr[...] = r0_ref[...]

    even = lax.broadcasted_iota(jnp.int32, (1, R_HD), 1) % 2 == 0
    cos = cos_ref[...]
    sin = sin_ref[...]

    def rot(t):
        partner = jnp.where(even, pltpu.roll(t, R_HD - 1, axis=1), pltpu.roll(t, 1, axis=1))
        return t * cos + partner * sin

    def load(halves, s, h):
        return halves[h // half][s * C:(s + 1) * C, (h % half) * R_HD:(h % half + 1) * R_HD]

    f32 = lambda *refs: tuple(r[...].astype(F32) for r in refs)
    items = [(s, h) for s in range(nb) for h in range(R_HEADS)]
    q_in, k_in, v_in, g_in = f32(q0_ref, q1_ref), f32(k0_ref, k1_ref), f32(v0_ref, v1_ref), f32(g0_ref, g1_ref)
    q = [rot(load(q_in, s, h)) for s, h in items]
    k = [rot(load(k_in, s, h)) * (R_HD ** -0.5) for s, h in items]
    v = [load(v_in, s, h).astype(BF16) for s, h in items]
    state = [r_scr[s, h] for s, h in items]
    scores = [_bdot_nt(q[i], k[i]) * din_ref[h] for i, (s, h) in enumerate(items)]
    o = [_bdot(jnp.concatenate([scores[i], q[i] * dq_ref[h]], axis=1),
               jnp.concatenate([v[i], state[i].astype(BF16)], axis=0))
         for i, (s, h) in enumerate(items)]
    for i, (s, h) in enumerate(items):
        r_scr[s, h] = state[i] * dc_ref[h, 0:1, :] + _bdot_tn(k[i] * dk_ref[h], v[i])
    for i, (s, h) in enumerate(items):
        on = o[i] * lax.rsqrt(jnp.mean(o[i] * o[i], axis=-1, keepdims=True) + NORM_EPS)
        gate = load(g_in, s, h)
        y_ref[s * C:(s + 1) * C, h * R_HD:(h + 1) * R_HD] = (
            gate * _sigmoid(gate) * on).astype(y_ref.dtype)

    @pl.when(c_idx == pl.num_programs(1) - 1)
    def _():
        rout_ref[...] = r_scr[...]


def _ret_tables(chunk, pos0, seq):
    f32 = np.float32
    pos = f32(pos0) + np.arange(seq, dtype=f32)
    theta = f32(1.0) / (f32(10000.0) ** np.linspace(0.0, 1.0, R_HD // 2, dtype=f32))
    ang = pos[:, None] * theta[None, :]
    cos2 = np.repeat(np.cos(ang), 2, axis=1)
    sin2 = np.stack([-np.sin(ang), np.sin(ang)], axis=-1).reshape(seq, R_HD)
    lg = np.log1p(-(f32(2.0) ** (f32(-5.0) - np.arange(R_HEADS, dtype=f32))))
    idx = np.arange(chunk, dtype=f32)
    diff = idx[:, None] - idx[None, :]
    causal = diff >= 0
    din = np.where(causal, np.exp(np.where(causal, diff, f32(0.0))[None] * lg[:, None, None]), f32(0.0))
    bc = lambda t: np.broadcast_to(t[:, :, None], (R_HEADS, t.shape[1], R_HD))
    dq = bc(np.exp((idx + f32(1.0))[None, :] * lg[:, None]))
    dk = bc(np.exp((f32(chunk - 1.0) - idx)[None, :] * lg[:, None]))
    dc = np.broadcast_to(np.exp(f32(chunk) * lg)[:, None, None], (R_HEADS, 8, R_HD))
    return tuple(jnp.asarray(np.ascontiguousarray(t, dtype=f32))
                 for t in (cos2, sin2, din, dq, dk, dc))


def _retention(proj, r0, batch, seq, chunk, nb, pos0):
    n_chunks = seq // chunk
    cos2, sin2, din, dq, dk, dc = _ret_tables(chunk, pos0, seq)
    col0 = RWKV_PAD // RET_BLK
    rows = nb * chunk

    def sec(i):
        return pl.BlockSpec((rows, RET_BLK), lambda b, c: (b * n_chunks + c, col0 + i))

    tab = lambda t: pl.BlockSpec(t.shape, lambda b, c: (0, 0, 0))
    st = pl.BlockSpec((nb, R_HEADS, R_HD, R_HD), lambda b, c: (b, 0, 0, 0))
    return pl.pallas_call(
        functools.partial(_ret_kernel, chunk=chunk, nb=nb),
        grid=(batch // nb, n_chunks),
        in_specs=[sec(i) for i in range(8)]
                 + [pl.BlockSpec((chunk, R_HD), lambda b, c: (c, 0)),
                    pl.BlockSpec((chunk, R_HD), lambda b, c: (c, 0)),
                    tab(din), tab(dq), tab(dk), tab(dc), st],
        out_specs=[pl.BlockSpec((rows, R_W), lambda b, c: (b * n_chunks + c, 0)), st],
        out_shape=[jax.ShapeDtypeStruct((batch * seq, R_W), BF16),
                   jax.ShapeDtypeStruct((batch, R_HEADS, R_HD, R_HD), F32)],
        scratch_shapes=[pltpu.VMEM((nb, R_HEADS, R_HD, R_HD), F32)],
        compiler_params=_cparams(("parallel", "arbitrary")),
        name="retention",
    )(*([proj] * 8), cos2, sin2, din, dq, dk, dc, r0)


def _outproj_kernel(x_ref, ya_ref, yb_ref, wa_ref, wb_ref, g_ref, wr_ref,
                    x1_ref, h2_ref, ri_ref, rg_ref, *, parts):
    rows = x_ref.shape[0] // parts
    sl = [slice(i * rows, (i + 1) * rows) for i in range(parts)]
    x1s = [x_ref[s, :] + jnp.dot(ya_ref[s, :], wa_ref[...], preferred_element_type=F32)
           + jnp.dot(yb_ref[s, :], wb_ref[...], preferred_element_type=F32) for s in sl]
    for s, x1 in zip(sl, x1s):
        x1_ref[s, :] = x1
        ms = jnp.mean(x1 * x1, axis=-1, keepdims=True)
        h2 = x1 * lax.rsqrt(ms + NORM_EPS) * g_ref[...]
        h2_ref[s, :] = _pack_bf16_pairs(h2)
        ri_ref[s, :], rg_ref[s, :] = _route(_bdot(h2, wr_ref[...]))


def _route(logits):
    lane = lax.broadcasted_iota(jnp.int32, (1, LANES), 1)
    neg = -1e30
    big = LANES
    rmax = lambda t: jnp.max(t, axis=-1, keepdims=True)
    rmin = lambda t: jnp.min(t, axis=-1, keepdims=True)
    is_c = lane < N_GROUPS
    lc = jnp.where(is_c, logits, neg)
    mc = rmax(lc)
    grp = rmin(jnp.where(lc == mc, lane, big))
    p_grp = 1.0 / jnp.sum(jnp.where(is_c, jnp.exp(lc - mc), 0.0), axis=-1, keepdims=True)
    fine = lane - N_GROUPS
    in_g = (fine >= 0) & (fine < N_EXPERTS) & ((fine // E_PER_GROUP) == grp)
    lf = jnp.where(in_g, logits, neg)
    m1 = rmax(lf)
    i1 = rmin(jnp.where(lf == m1, lane, big))
    lf2 = jnp.where(lane == i1, neg, lf)
    m2 = rmax(lf2)
    i2 = rmin(jnp.where(lf2 == m2, lane, big))
    e2 = jnp.exp(m2 - m1)
    g1 = p_grp / (1.0 + e2)
    g2 = p_grp * e2 / (1.0 + e2)
    return (jnp.where(lane == 0, i1 - N_GROUPS, jnp.where(lane == 1, i2 - N_GROUPS, 0)),
            jnp.where(lane == 0, g1, jnp.where(lane == 1, g2, 0.0)))


def _outproj(x, ya, yb, wa, wb, g, wr, tm=512, parts=2):
    n, d = x.shape
    rowblk = lambda w: pl.BlockSpec((tm, w), lambda i: (i, 0))
    full = lambda w: pl.BlockSpec(w.shape, lambda i: (0, 0))
    return pl.pallas_call(
        functools.partial(_outproj_kernel, parts=parts),
        grid=(n // tm,),
        in_specs=[rowblk(d), rowblk(A_W), rowblk(R_W), full(wa), full(wb), full(g), full(wr)],
        out_specs=[rowblk(d), rowblk(d // 2), rowblk(LANES), rowblk(LANES)],
        out_shape=[jax.ShapeDtypeStruct((n, d), F32), jax.ShapeDtypeStruct((n, d // 2), jnp.uint32),
                   jax.ShapeDtypeStruct((n, LANES), jnp.int32),
                   jax.ShapeDtypeStruct((n, LANES), F32)],
        compiler_params=_cparams(("parallel",)),
        name="outproj_router",
    )(x, ya, yb, wa, wb, g, wr)


def _sort_kernel(ri_ref, tri_ref, dest_ref, info_ref, cnt_ref, base_ref):
    ph = pl.program_id(0)
    i = pl.program_id(1)
    lane = lax.broadcasted_iota(jnp.int32, (1, LANES), 1)
    ri = ri_ref[...]
    oh1 = (lane == ri[:, 0:1]).astype(F32)
    oh2 = (lane == ri[:, 1:2]).astype(F32)
    tot1 = jnp.sum(oh1, axis=0, keepdims=True)
    tot2 = jnp.sum(oh2, axis=0, keepdims=True)

    @pl.when(ph == 0)
    def _():
        @pl.when(i == 0)
        def _():
            cnt_ref[...] = jnp.zeros_like(cnt_ref)
        cnt_ref[...] += tot1 + tot2

    @pl.when(ph == 1)
    def _():
        @pl.when(i == 0)
        def _():
            nblk = jnp.floor((cnt_ref[...] + (MOE_BLK - 1)) * (1.0 / MOE_BLK))
            upper = (lax.broadcasted_iota(jnp.int32, (LANES, LANES), 0) <=
                     lax.broadcasted_iota(jnp.int32, (LANES, LANES), 1))
            pend = _bdot(jnp.broadcast_to(nblk, (8, LANES)), upper.astype(F32))[0:1]
            first = pend - nblk
            base_ref[...] = first * MOE_BLK
            srow = lax.broadcasted_iota(jnp.int32, (8, LANES), 0)
            info_ref[...] = jnp.where(srow == 0, nblk, jnp.where(srow == 1, first, 0.0)).astype(jnp.int32)

        base = base_ref[...]
        tri = tri_ref[...]
        pre1 = jnp.dot(tri, oh1.astype(BF16), preferred_element_type=F32)
        pre2 = jnp.dot(tri, oh2.astype(BF16), preferred_element_type=F32) + tot1
        d1 = jnp.sum(oh1 * (base + pre1), axis=1, keepdims=True)
        d2 = jnp.sum(oh2 * (base + pre2), axis=1, keepdims=True)
        dest_ref[...] = jnp.where(lane == 0, d1, jnp.where(lane == 1, d2, 0.0)).astype(jnp.int32)
        base_ref[...] = base + tot1 + tot2


def _sort(ri, tt=1024):
    n = ri.shape[0]
    tri = (jnp.arange(tt)[:, None] > jnp.arange(tt)[None, :]).astype(BF16)
    return pl.pallas_call(
        _sort_kernel,
        grid=(2, n // tt),
        in_specs=[pl.BlockSpec((tt, LANES), lambda ph, i: (i, 0)),
                  pl.BlockSpec((tt, tt), lambda ph, i: (0, 0))],
        out_specs=[pl.BlockSpec((tt, LANES), lambda ph, i: (i * ph, 0)),
                   pl.BlockSpec((8, LANES), lambda ph, i: (0, 0))],
        out_shape=[jax.ShapeDtypeStruct((n, LANES), jnp.int32),
                   jax.ShapeDtypeStruct((8, LANES), jnp.int32)],
        scratch_shapes=[pltpu.VMEM((1, LANES), F32), pltpu.VMEM((1, LANES), F32)],
        compiler_params=_cparams(("arbitrary", "arbitrary")),
        name="moe_sort",
    )(ri, tri)


def _row_copy(src_ref, src_row, dst_ref, dst_row, sem):
    return pltpu.make_async_copy(src_ref.at[pl.ds(src_row, 1)], dst_ref.at[pl.ds(dst_row, 1)], sem)


def _dispatch_kernel(d1_ref, d2_ref, h_ref, xs_in_ref, xs_ref, sem, *, tt):
    del xs_in_ref

    def issue(t, carry):
        _row_copy(h_ref, t, xs_ref, d1_ref[t], sem).start()
        _row_copy(h_ref, t, xs_ref, d2_ref[t], sem).start(priority=1)
        return carry

    def drain(t, carry):
        _row_copy(h_ref, t, xs_ref, d1_ref[t], sem).wait()
        _row_copy(h_ref, t, xs_ref, d2_ref[t], sem).wait()
        return carry

    lax.fori_loop(0, tt, issue, 0, unroll=8)
    lax.fori_loop(0, tt, drain, 0, unroll=8)


def _dispatch(d1, d2, h2, xs, tt=1024):
    n, d = h2.shape
    smem = pl.BlockSpec((tt,), lambda i: (i,), memory_space=pltpu.SMEM)
    anyspec = pl.BlockSpec(memory_space=pl.ANY)
    return pl.pallas_call(
        functools.partial(_dispatch_kernel, tt=tt),
        grid=(n // tt,),
        in_specs=[smem, smem, pl.BlockSpec((tt, d), lambda i: (i, 0)), anyspec],
        out_specs=anyspec,
        out_shape=jax.ShapeDtypeStruct(xs.shape, xs.dtype),
        scratch_shapes=[pltpu.SemaphoreType.DMA(())],
        input_output_aliases={3: 0},
        compiler_params=_cparams(("arbitrary",)),
        name="moe_dispatch",
    )(d1, d2, h2, xs)


def _expert_kernel(nb_ref, fb_ref, xs_hbm, wg_hbm, wu_hbm, wd_hbm, y_hbm,
                   wg_r, wu_r, wd_r, wg_s, wu_s, wd_s, xbuf, ybuf, sem_w, sem_in, sem_out, *, n_blk):
    e = pl.program_id(0)
    last = pl.num_programs(0) - 1
    n = nb_ref[e]
    b0 = fb_ref[e]
    n_used = fb_ref[last] + nb_ref[last]
    half = xbuf.shape[2]
    ring = wg_r.shape[0]

    def fetch(ex):
        slot = ex % ring
        return [pltpu.make_async_copy(src.at[ex], dst.at[slot], sem_w.at[slot])
                for src, dst in ((wg_hbm, wg_r), (wu_hbm, wu_r), (wd_hbm, wd_r))]

    @pl.when(e == 0)
    def _():
        for ex in range(ring - 1):
            for cp in fetch(ex):
                cp.start(priority=W_DMA_PRIORITY)

    @pl.when(e + ring - 1 <= last)
    def _():
        for cp in fetch(e + ring - 1):
            cp.start(priority=W_DMA_PRIORITY)

    for cp in fetch(e):
        cp.wait()
    w_slot = e % ring

    def load(blk, slot):
        return pltpu.make_async_copy(xs_hbm.at[pl.ds(blk * MOE_BLK, MOE_BLK)],
                                     xbuf.at[slot], sem_in.at[slot])

    def store(blk, slot):
        return pltpu.make_async_copy(ybuf.at[slot], y_hbm.at[pl.ds(blk * MOE_BLK, MOE_BLK)],
                                     sem_out.at[slot])

    @pl.when(n > 0)
    def _():
        @pl.when(b0 == 0)
        def _():
            load(0, 0).start()

        wg_s[...] = wg_r[w_slot].astype(BF16)
        wu_s[...] = wu_r[w_slot].astype(BF16)
        wd_s[...] = wd_r[w_slot].astype(BF16)

        def body(blk, carry):
            slot = blk % 2
            load(blk, slot).wait()

            @pl.when(blk + 1 < n_used)
            def _():
                load(blk + 1, 1 - slot).start()

            @pl.when(blk >= 2)
            def _():
                store(blk - 2, slot).wait()

            x_lo, x_hi = _unpack_bf16_pairs(xbuf[slot])
            x_lo, x_hi = x_lo.astype(BF16), x_hi.astype(BF16)
            mm = lambda w_s: (jnp.dot(x_lo, w_s[0:half], preferred_element_type=F32)
                              + jnp.dot(x_hi, w_s[half:2 * half], preferred_element_type=F32))
            hg = mm(wg_s)
            hu = mm(wu_s)
            act = (hg * _sigmoid(hg) * hu).astype(BF16)
            ybuf[slot] = _pack_bf16_pairs(jnp.dot(act, wd_s[...], preferred_element_type=F32))
            store(blk, slot).start()
            return carry

        lax.fori_loop(b0, b0 + n, body, 0)

    @pl.when(e == last)
    def _():
        @pl.when(n_used >= 2)
        def _():
            store(n_used - 2, n_used % 2).wait()

        @pl.when(n_used >= 1)
        def _():
            store(n_used - 1, (n_used - 1) % 2).wait()

        ybuf[0] = jnp.zeros(ybuf.shape[1:], ybuf.dtype)

        def fill(blk, carry):
            cp = store(blk, 0)
            cp.start()
            cp.wait()
            return carry

        lax.fori_loop(n_used, n_blk, fill, 0)


def _experts(n_blocks, first_block, xs, wg, wu, wd):
    p, half = xs.shape
    d = 2 * half
    anyspec = pl.BlockSpec(memory_space=pl.ANY)
    ring = W_AHEAD + 1
    grid_spec = pltpu.PrefetchScalarGridSpec(
        num_scalar_prefetch=2,
        grid=(N_EXPERTS,),
        in_specs=[anyspec] * 4,
        out_specs=anyspec,
        scratch_shapes=[pltpu.VMEM((ring, d, D_EXPERT), F32), pltpu.VMEM((ring, d, D_EXPERT), F32),
                        pltpu.VMEM((ring, D_EXPERT, d), F32),
                        pltpu.VMEM((d, D_EXPERT), BF16), pltpu.VMEM((d, D_EXPERT), BF16),
                        pltpu.VMEM((D_EXPERT, d), BF16),
                        pltpu.VMEM((2, MOE_BLK, half), jnp.uint32),
                        pltpu.VMEM((2, MOE_BLK, half), jnp.uint32),
                        pltpu.SemaphoreType.DMA((ring,)),
                        pltpu.SemaphoreType.DMA((2,)), pltpu.SemaphoreType.DMA((2,))])
    return pl.pallas_call(
        functools.partial(_expert_kernel, n_blk=p // MOE_BLK),
        grid_spec=grid_spec,
        out_shape=jax.ShapeDtypeStruct((p, half), jnp.uint32),
        compiler_params=_cparams(("arbitrary",)),
        name="moe_experts",
    )(n_blocks, first_block, xs, wg, wu, wd)


def _combine_kernel(d1_ref, d2_ref, d1n_ref, d2n_ref, x1_ref, rg_ref, g_ref, yb_ref, o_ref, buf, sem,
                    *, tt):
    i = pl.program_id(0)
    slot = i % 2

    def gather(a_ref, b_ref, s, wait):
        def body(t, carry):
            for j, idx in enumerate((a_ref, b_ref)):
                cp = _row_copy(yb_ref, idx[t], buf.at[s, j], t, sem.at[s])
                cp.wait() if wait else cp.start(priority=j)
            return carry
        lax.fori_loop(0, tt, body, 0, unroll=8)

    @pl.when(i == 0)
    def _():
        gather(d1_ref, d2_ref, 0, False)

    @pl.when(i + 1 < pl.num_programs(0))
    def _():
        gather(d1n_ref, d2n_ref, 1 - slot, False)

    gather(d1_ref, d2_ref, slot, True)
    rg = rg_ref[...]
    lo1, hi1 = _unpack_bf16_pairs(buf[slot, 0])
    lo2, hi2 = _unpack_bf16_pairs(buf[slot, 1])
    g1, g2 = rg[:, 0:1], rg[:, 1:2]
    x = x1_ref[...] + jnp.concatenate([g1 * lo1 + g2 * lo2, g1 * hi1 + g2 * hi2], axis=1)
    ms = jnp.mean(x * x, axis=-1, keepdims=True)
    o_ref[...] = x * lax.rsqrt(ms + NORM_EPS) * g_ref[...]


def _combine(d1, d2, x1, rg, g, yb, tt=256):
    n, d = x1.shape
    steps = n // tt
    smem = pl.BlockSpec((tt,), lambda i: (i,), memory_space=pltpu.SMEM)
    smem_next = pl.BlockSpec((tt,), lambda i: (jnp.minimum(i + 1, steps - 1),),
                             memory_space=pltpu.SMEM)
    rowblk = lambda w: pl.BlockSpec((tt, w), lambda i: (i, 0))
    return pl.pallas_call(
        functools.partial(_combine_kernel, tt=tt),
        grid=(steps,),
        in_specs=[smem, smem, smem_next, smem_next, rowblk(d), rowblk(LANES),
                  pl.BlockSpec((1, d), lambda i: (0, 0)), pl.BlockSpec(memory_space=pl.ANY)],
        out_specs=rowblk(d),
        out_shape=jax.ShapeDtypeStruct((n, d), F32),
        scratch_shapes=[pltpu.VMEM((2, 2, tt, d // 2), jnp.uint32), pltpu.SemaphoreType.DMA((2,))],
        compiler_params=_cparams(("arbitrary",)),
        name="moe_combine",
    )(d1, d2, d1, d2, x1, rg, g, yb)


def _pad_cols(x, width):
    return jnp.pad(x, ((0, 0), (0, width - x.shape[1])))


def _rwkv_layout(x):
    return jnp.concatenate([
        x[:, :OFF_WD],
        _pad_cols(x[:, 3 * A_W:3 * A_W + W_LORA], LANES),
        _pad_cols(x[:, 3 * A_W + W_LORA:3 * A_W + W_LORA + A_LORA], LANES),
        _pad_cols(x[:, 3 * A_W + W_LORA + A_LORA:RWKV_PROJ], 2 * LANES)], axis=1)


def _rwkv_unlayout(x):
    return jnp.concatenate([x[:, :OFF_WD + W_LORA], x[:, OFF_AD:OFF_AD + A_LORA],
                            x[:, OFF_GD:OFF_GD + G_LORA]], axis=1)


def _pad_rows(w, rows):
    return jnp.pad(w, ((0, rows - w.shape[0]), (0, 0)))


def kernel(x_prompt, x_sample, state_rwkv_shift, state_rwkv_wkv, state_retention, norm_mix_g, w_in, mu_shift, decay_w0, decay_up, iclr_a0, iclr_up, gate_up, k_k, k_a, r_k, lnx_g, lnx_b, w_out, norm_ffn_g, router_coarse, router_fine, exp_w_gate, exp_w_up, exp_w_down, norm_final_g):
    bp, tp, d = x_prompt.shape
    bs, ts, _ = x_sample.shape
    row = lambda t: t.reshape(1, -1)
    l = 0

    w_in_p = jnp.concatenate([_rwkv_layout(w_in[l][:, :RWKV_PROJ]), w_in[l][:, RWKV_PROJ:]],
                             axis=1).astype(BF16)
    rwkv_wts = (_rwkv_layout(row(mu_shift[l])), row(decay_w0[l]),
                _pad_rows(decay_up[l], LANES).astype(BF16), row(iclr_a0[l]),
                _pad_rows(iclr_up[l], LANES).astype(BF16),
                _pad_rows(gate_up[l], 2 * LANES).astype(BF16),
                row(k_k[l]), row(k_a[l]), row(r_k[l]), row(lnx_g[l]), row(lnx_b[l]))
    w_out_a = w_out[l][:A_W].astype(BF16)
    w_out_b = w_out[l][A_W:].astype(BF16)
    w_router = _pad_cols(jnp.concatenate(
        [router_coarse[l], jnp.transpose(router_fine[l], (1, 0, 2)).reshape(d, N_EXPERTS)], axis=1),
        LANES).astype(BF16)

    groups = (
        (x_prompt.reshape(bp * tp, d), bp, tp, WKV_CHUNK, RET_CHUNK, 1, 0.0,
         jnp.zeros((bp, RWKV_PAD), F32), jnp.zeros((bp, A_HEADS, A_HD, A_HD), F32),
         jnp.zeros((bp, R_HEADS, R_HD, R_HD), F32)),
        (x_sample.reshape(bs * ts, d), bs, ts, ts, math.gcd(ts, RET_CHUNK), 8, float(PAST_LEN),
         _rwkv_layout(state_rwkv_shift[l]), state_rwkv_wkv[l], state_retention[l]),
    )
    x_last = jnp.concatenate([x_prompt[:, -1, :], x_sample[:, -1, :]], axis=0)
    n_last = -(-(bp + bs) // 8) * 8
    shifts = _rwkv_unlayout(_inproj(_pad_rows(x_last, n_last), row(norm_mix_g[l]), w_in_p, F32,
                                    RWKV_PAD, n_last, RWKV_PAD // 2))
    new_shifts = (shifts[:bp], shifts[bp:bp + bs])

    mixed = []
    for (x, b, t, wkv_chunk, ret_chunk, nb, pos0, shift0, wkv0, ret0), new_shift in zip(groups, new_shifts):
        proj = _inproj(x, row(norm_mix_g[l]), w_in_p, BF16, PROJ_PAD, 1024, 1536)
        y_a, new_wkv = _rwkv(proj, shift0, wkv0, rwkv_wts, b, t, wkv_chunk, nb,
                             WKV_SUB if nb == 1 else WKV_GROUPS)
        y_b, new_ret = _retention(proj, ret0, b, t, ret_chunk, nb, pos0)
        x1, h2, ri, rg = _outproj(x, y_a, y_b, w_out_a, w_out_b, row(norm_ffn_g[l]), w_router)
        mixed.append((x1, h2, ri, rg, new_shift, new_wkv, new_ret))

    n_tok = [m[0].shape[0] for m in mixed]
    n_all = sum(n_tok)
    n_blk = -(-2 * n_all // MOE_BLK) + N_EXPERTS
    dest, info = _sort(jnp.concatenate([m[2] for m in mixed], axis=0))
    n_blocks, first_block = info[0, :N_EXPERTS], info[1, :N_EXPERTS]
    xs = jnp.zeros((n_blk * MOE_BLK, d // 2), jnp.uint32)
    off = 0
    dests = []
    for m, n in zip(mixed, n_tok):
        d1, d2 = dest[off:off + n, 0], dest[off:off + n, 1]
        dests.append((d1, d2))
        xs = _dispatch(d1, d2, m[1], xs)
        off += n
    yb = _experts(n_blocks, first_block, xs, exp_w_gate[l], exp_w_up[l], exp_w_down[l])
    outs = [_combine(d1, d2, m[0], m[3], row(norm_final_g), yb)
            for m, (d1, d2) in zip(mixed, dests)]

    (_, _, _, _, p_shift, p_wkv, p_ret), (_, _, _, _, s_shift, s_wkv, s_ret) = mixed
    return (outs[0].reshape(bp, tp, d), outs[1].reshape(bs, ts, d),
            p_shift[None], p_wkv[None], p_ret[None], s_shift[None], s_wkv[None], s_ret[None])
```
